```python
import math
import jax, jax.numpy as jnp
from jax import lax
import numpy as np

D_MODEL = 1024
BATCH = 4
SEQ = 8192
DEPTH = 1

D_MIX = D_MODEL
ATTN_QK_DIM = 32
ATTN_V_DIM = 2 * ATTN_QK_DIM
N_ATTN_HEADS = (D_MIX // 2) // ATTN_V_DIM
W_ATTN_QK = N_ATTN_HEADS * 2 * ATTN_QK_DIM
W_ATTN = N_ATTN_HEADS * ATTN_V_DIM
ROT_DIM = ATTN_QK_DIM // 4
ROPE_THETA = 500000.0
Q_BLOCK = 128
POS_OFFSET_MAX = 1024
S5_GROUP_CH = 16
S5_STATE = 64
W_S5 = D_MIX - W_ATTN
S5_GROUPS = W_S5 // S5_GROUP_CH
S5_MIN_STEP = 1e-3
S5_MAX_STEP = 1e-1
D_IN_PROJ = 2 * W_ATTN_QK + W_ATTN + W_S5
N_EXPERTS = 32
TOP_K = 4
D_FF = D_MODEL
SWIGLU_ALPHA = 1.702
SWIGLU_LIMIT = 7.0
MOE_BLOCK = 128
RMS_EPS = 1e-6

kernel_name = "hybrid_diffattn_s5_moe_block"


def rms_norm(x, w, eps=RMS_EPS):
    xf = x.astype(jnp.float32)
    y = xf * lax.rsqrt(jnp.mean(xf * xf, axis=-1, keepdims=True) + eps)
    return y.astype(x.dtype) * w


def rope_tables(positions):
    inv_freq = ROPE_THETA ** (-jnp.arange(0, ROT_DIM, 2, dtype=jnp.float32) / ROT_DIM)
    ang = positions.astype(jnp.float32)[..., None] * inv_freq
    return jnp.cos(ang), jnp.sin(ang)


def apply_partial_rope(t, cos, sin):
    half = ROT_DIM // 2
    cos = cos[:, :, None, None, :].astype(t.dtype)
    sin = sin[:, :, None, None, :].astype(t.dtype)
    t1 = t[..., :half]
    t2 = t[..., half:ROT_DIM]
    return jnp.concatenate([t1 * cos - t2 * sin, t2 * cos + t1 * sin, t[..., ROT_DIM:]], axis=-1)


def diff_attention(q, k, v, cos, sin, q_norm_w, k_norm_w, lam, subln_w, lambda_init):
    B, S, _ = q.shape
    H, d, dv = N_ATTN_HEADS, ATTN_QK_DIM, ATTN_V_DIM
    q = rms_norm(q.reshape(B, S, H, 2, d), q_norm_w)
    k = rms_norm(k.reshape(B, S, H, 2, d), k_norm_w)
    q = apply_partial_rope(q, cos, sin)
    k = apply_partial_rope(k, cos, sin)
    q = jnp.transpose(q, (0, 2, 3, 1, 4))
    k = jnp.transpose(k, (0, 2, 3, 1, 4))
    v = jnp.transpose(v.reshape(B, S, H, dv), (0, 2, 1, 3))
    n_qb = S // Q_BLOCK
    qb = jnp.moveaxis(q.reshape(B, H, 2, n_qb, Q_BLOCK, d), 3, 0)
    scale = 1.0 / math.sqrt(d)

    def block(q_blk):
        s = jnp.einsum('bhcqd,bhckd->bhcqk', q_blk, k).astype(jnp.float32) * scale
        p = jax.nn.softmax(s, axis=-1)
        a = p[:, :, 0] - lam * p[:, :, 1]
        return jnp.einsum('bhqk,bhkv->bhqv', a.astype(v.dtype), v)

    o = lax.map(block, qb)
    o = jnp.transpose(o, (1, 0, 3, 2, 4)).reshape(B, S, H, dv)
    o = rms_norm(o, subln_w) * (1.0 - lambda_init)
    return o.reshape(B, S, W_ATTN)


def complex_combine(e_i, e_j):
    ar_i, ai_i, br_i, bi_i = e_i
    ar_j, ai_j, br_j, bi_j = e_j
    return (ar_j * ar_i - ai_j * ai_i,
            ar_j * ai_i + ai_j * ar_i,
            ar_j * br_i - ai_j * bi_i + br_j,
            ar_j * bi_i + ai_j * br_i + bi_j)


def s5_mixer(u, lam_re, lam_im, log_step, b_re, b_im, cm_re, cm_im, d_skip, glu_w, glu_b, out_w):
    Bsz, S, _ = u.shape
    uf = u.astype(jnp.float32).reshape(Bsz, S, S5_GROUPS, S5_GROUP_CH)
    y = uf * d_skip.astype(jnp.float32).reshape(S5_GROUPS, S5_GROUP_CH)
    for di in range(2):
        lr = lam_re[di].astype(jnp.float32)
        li = lam_im[di].astype(jnp.float32)
        delta = jnp.exp(log_step[di].astype(jnp.float32))[:, None]
        mag = jnp.exp(lr * delta)
        abar_re = mag * jnp.cos(li * delta)
        abar_im = mag * jnp.sin(li * delta)
        den = lr * lr + li * li
        num_re = abar_re - 1.0
        f_re = (num_re * lr + abar_im * li) / den
        f_im = (abar_im * lr - num_re * li) / den
        br = b_re[di].astype(jnp.float32)
        bi = b_im[di].astype(jnp.float32)
        bbar_re = f_re[..., None] * br - f_im[..., None] * bi
        bbar_im = f_re[..., None] * bi + f_im[..., None] * br
        bu_re = jnp.einsum('bsgc,gpc->bsgp', uf, bbar_re)
        bu_im = jnp.einsum('bsgc,gpc->bsgp', uf, bbar_im)
        a_re = jnp.broadcast_to(abar_re, bu_re.shape)
        a_im = jnp.broadcast_to(abar_im, bu_im.shape)
        _, _, xr, xi = lax.associative_scan(complex_combine, (a_re, a_im, bu_re, bu_im),
                                            reverse=(di == 1), axis=1)
        y = y + (jnp.einsum('bsgp,gcp->bsgc', xr, cm_re[di].astype(jnp.float32))
                 - jnp.einsum('bsgp,gcp->bsgc', xi, cm_im[di].astype(jnp.float32)))
    y = jax.nn.gelu(y.reshape(Bsz, S, W_S5))
    y = y * jax.nn.sigmoid(y @ glu_w.astype(jnp.float32) + glu_b.astype(jnp.float32))
    return rms_norm(y.astype(u.dtype), out_w)


def expert_ffn(xb, w1, b1, w2, b2):
    z = xb @ w1 + b1
    x_glu = jnp.minimum(z[..., ::2], SWIGLU_LIMIT)
    x_lin = jnp.clip(z[..., 1::2], -SWIGLU_LIMIT, SWIGLU_LIMIT)
    act = x_glu * jax.nn.sigmoid(SWIGLU_ALPHA * x_glu) * (x_lin + 1.0)
    return act @ w2 + b2


def moe(h, router_w, router_b, mlp1_w, mlp1_b, mlp2_w, mlp2_b):
    B, S, D = h.shape
    T = B * S
    ht = h.reshape(T, D)
    logits = (ht @ router_w + router_b).astype(jnp.float32)
    top_vals, top_idx = lax.top_k(logits, TOP_K)
    gates = jax.nn.softmax(top_vals, axis=-1)
    n_assign = T * TOP_K
    flat_e = top_idx.reshape(-1).astype(jnp.int32)
    flat_tok = jnp.arange(n_assign, dtype=jnp.int32) // TOP_K
    flat_g = gates.reshape(-1)
    order = jnp.argsort(flat_e, stable=True)
    sorted_e = flat_e[order]
    sorted_tok = flat_tok[order]
    sorted_g = flat_g[order]
    counts = jnp.bincount(flat_e, length=N_EXPERTS).astype(jnp.int32)
    padded = ((counts + MOE_BLOCK - 1) // MOE_BLOCK) * MOE_BLOCK
    pad_end = jnp.cumsum(padded)
    pad_start = pad_end - padded
    start = jnp.cumsum(counts) - counts
    dest = pad_start[sorted_e] + jnp.arange(n_assign, dtype=jnp.int32) - start[sorted_e]
    n_rows = n_assign + N_EXPERTS * MOE_BLOCK
    n_blocks = n_rows // MOE_BLOCK
    xs = jnp.zeros((n_rows, D), h.dtype).at[dest].set(ht[sorted_tok])
    block_e = jnp.minimum(
        jnp.searchsorted(pad_end, jnp.arange(n_blocks, dtype=jnp.int32) * MOE_BLOCK, side='right'),
        N_EXPERTS - 1).astype(jnp.int32)

    def run(args):
        xb, e = args
        return expert_ffn(xb, mlp1_w[e], mlp1_b[e], mlp2_w[e], mlp2_b[e])

    ys = lax.map(run, (xs.reshape(n_blocks, MOE_BLOCK, D), block_e)).reshape(n_rows, D)
    y = ys[dest] * sorted_g[:, None].astype(ys.dtype)
    out = jnp.zeros((T, D), h.dtype).at[sorted_tok].add(y.astype(h.dtype))
    return out.reshape(B, S, D)


def setup_inputs(seed: int = 0) -> dict:
    key = jax.random.key(seed)
    ks = jax.random.split(key, 40)
    L = DEPTH
    f32 = jnp.float32

    def nrm(k, shape, s):
        return jax.random.normal(k, shape, f32) * s

    n_idx = jnp.arange(S5_STATE, dtype=f32)
    return {
        "x": nrm(ks[0], (BATCH, SEQ, D_MODEL), 1.0),
        "c": nrm(ks[1], (BATCH, D_MODEL), 1.0),
        "positions": (jnp.arange(SEQ, dtype=jnp.int32)[None, :]
                      + jax.random.randint(ks[2], (BATCH, 1), 0, POS_OFFSET_MAX, dtype=jnp.int32)),
        "ada_w": nrm(ks[3], (L, D_MODEL, 6 * D_MODEL), D_MODEL ** -0.5),
        "ada_b": nrm(ks[4], (L, 6 * D_MODEL), 0.02),
        "norm1_w": 1.0 + nrm(ks[5], (L, D_MODEL), 0.02),
        "w_in": nrm(ks[6], (L, D_MODEL, D_IN_PROJ), D_MODEL ** -0.5),
        "q_norm_w": 1.0 + nrm(ks[7], (L, ATTN_QK_DIM), 0.02),
        "k_norm_w": 1.0 + nrm(ks[8], (L, ATTN_QK_DIM), 0.02),
        "lambda_q1": nrm(ks[9], (L, ATTN_QK_DIM), 0.1),
        "lambda_k1": nrm(ks[10], (L, ATTN_QK_DIM), 0.1),
        "lambda_q2": nrm(ks[11], (L, ATTN_QK_DIM), 0.1),
        "lambda_k2": nrm(ks[12], (L, ATTN_QK_DIM), 0.1),
        "subln_w": 1.0 + nrm(ks[13], (L, ATTN_V_DIM), 0.02),
        "s5_lambda_re": -0.5 + nrm(ks[14], (L, 2, S5_GROUPS, S5_STATE), 0.01),
        "s5_lambda_im": math.pi * n_idx + nrm(ks[15], (L, 2, S5_GROUPS, S5_STATE), 0.01),
        "s5_log_step": jax.random.uniform(ks[16], (L, 2, S5_GROUPS), f32,
                                          math.log(S5_MIN_STEP), math.log(S5_MAX_STEP)),
        "s5_b_re": nrm(ks[17], (L, 2, S5_GROUPS, S5_STATE, S5_GROUP_CH), (2 * S5_GROUP_CH) ** -0.5),
        "s5_b_im": nrm(ks[18], (L, 2, S5_GROUPS, S5_STATE, S5_GROUP_CH), (2 * S5_GROUP_CH) ** -0.5),
        "s5_cmat_re": nrm(ks[19], (L, 2, S5_GROUPS, S5_GROUP_CH, S5_STATE), (2 * S5_STATE) ** -0.5),
        "s5_cmat_im": nrm(ks[20], (L, 2, S5_GROUPS, S5_GROUP_CH, S5_STATE), (2 * S5_STATE) ** -0.5),
        "s5_d": nrm(ks[21], (L, W_S5), 1.0),
        "s5_glu_w": nrm(ks[22], (L, W_S5, W_S5), W_S5 ** -0.5),
        "s5_glu_b": nrm(ks[23], (L, W_S5), 0.02),
        "s5_out_norm_w": 1.0 + nrm(ks[24], (L, W_S5), 0.02),
        "w_out": nrm(ks[25], (L, D_MIX, D_MODEL), D_MIX ** -0.5),
        "norm2_w": 1.0 + nrm(ks[26], (L, D_MODEL), 0.02),
        "router_w": nrm(ks[27], (L, D_MODEL, N_EXPERTS), D_MODEL ** -0.5),
        "router_b": nrm(ks[28], (L, N_EXPERTS), 0.01),
        "mlp1_w": nrm(ks[29], (L, N_EXPERTS, D_MODEL, 2 * D_FF), D_MODEL ** -0.5),
        "mlp1_b": nrm(ks[30], (L, N_EXPERTS, 2 * D_FF), 0.02),
        "mlp2_w": nrm(ks[31], (L, N_EXPERTS, D_FF, D_MODEL), D_FF ** -0.5),
        "mlp2_b": nrm(ks[32], (L, N_EXPERTS, D_MODEL), 0.02),
    }


def reference(x, c, positions, ada_w, ada_b, norm1_w, w_in, q_norm_w, k_norm_w,
              lambda_q1, lambda_k1, lambda_q2, lambda_k2, subln_w,
              s5_lambda_re, s5_lambda_im, s5_log_step, s5_b_re, s5_b_im, s5_cmat_re, s5_cmat_im,
              s5_d, s5_glu_w, s5_glu_b, s5_out_norm_w, w_out, norm2_w,
              router_w, router_b, mlp1_w, mlp1_b, mlp2_w, mlp2_b):
    cos, sin = rope_tables(positions)
    c_act = jax.nn.silu(c)
    for l in range(DEPTH):
        mod = c_act @ ada_w[l] + ada_b[l]
        sh1, sc1, g1, sh2, sc2, g2 = jnp.split(mod[:, None, :], 6, axis=-1)
        h = rms_norm(x, norm1_w[l]) * (1.0 + sc1) + sh1
        proj = h @ w_in[l]
        q, k, v, u = jnp.split(proj, [W_ATTN_QK, 2 * W_ATTN_QK, 2 * W_ATTN_QK + W_ATTN], axis=-1)
        lambda_init = 0.8 - 0.6 * math.exp(-0.3 * l)
        lam = (jnp.exp(jnp.sum(lambda_q1[l].astype(jnp.float32) * lambda_k1[l].astype(jnp.float32)))
               - jnp.exp(jnp.sum(lambda_q2[l].astype(jnp.float32) * lambda_k2[l].astype(jnp.float32)))
               + lambda_init)
        a_out = diff_attention(q, k, v, cos, sin, q_norm_w[l], k_norm_w[l], lam, subln_w[l], lambda_init)
        s_out = s5_mixer(u, s5_lambda_re[l], s5_lambda_im[l], s5_log_step[l], s5_b_re[l], s5_b_im[l],
                         s5_cmat_re[l], s5_cmat_im[l], s5_d[l], s5_glu_w[l], s5_glu_b[l],
                         s5_out_norm_w[l])
        mix = jnp.concatenate([a_out, s_out], axis=-1) @ w_out[l]
        x = x + g1 * mix
        h2 = rms_norm(x, norm2_w[l]) * (1.0 + sc2) + sh2
        x = x + g2 * moe(h2, router_w[l], router_b[l], mlp1_w[l], mlp1_b[l], mlp2_w[l], mlp2_b[l])
    return x
```

```python
import functools
import math

import jax
import jax.numpy as jnp
from jax import lax
from jax.experimental import pallas as pl
from jax.experimental.pallas import tpu as pltpu

F32 = jnp.float32
BF16 = jnp.bfloat16

D_MODEL = 1024
QK_DIM = 32
V_DIM = 64
N_HEADS = 8
W_QK = N_HEADS * 2 * QK_DIM
W_ATTN = N_HEADS * V_DIM
ROT_DIM = QK_DIM // 4
ROPE_THETA = 500000.0
S5_CH = 16
S5_STATE = 64
W_S5 = 512
S5_GROUPS = W_S5 // S5_CH
D_IN_PROJ = 2 * W_QK + W_ATTN + W_S5
N_EXPERTS = 32
TOP_K = 4
D_FF = D_MODEL
SWIGLU_ALPHA = 1.702
SWIGLU_LIMIT = 7.0
RMS_EPS = 1e-6
LAMBDA_INIT = 0.8 - 0.6 * math.exp(-0.3 * 0)

LANES = 128
SUBLANES = 8
VMEM_LIMIT = 56 * 1024 * 1024

GROUPS_PER_BLOCK = LANES // S5_CH
STATE_LANES = GROUPS_PER_BLOCK * S5_STATE
MOE_ROWS = 512


def _cparams(sem):
    return pltpu.CompilerParams(dimension_semantics=sem, vmem_limit_bytes=VMEM_LIMIT)


def _rms(x, eps=RMS_EPS):
    return x * lax.rsqrt(jnp.mean(x * x, axis=-1, keepdims=True) + eps)


def _ada_kernel(c_ref, w_ref, b_ref, o_ref):
    c = c_ref[...]
    ca = c * jax.nn.sigmoid(c)
    o_ref[...] = jnp.dot(ca, w_ref[...], preferred_element_type=F32,
                         precision=lax.Precision.HIGHEST) + b_ref[...]


def _ada(c_pad, w, b):
    rows, d = c_pad.shape
    n = w.shape[1]
    tn = 1536
    return pl.pallas_call(
        _ada_kernel,
        grid=(n // tn,),
        in_specs=[pl.BlockSpec((rows, d), lambda j: (0, 0)),
                  pl.BlockSpec((d, tn), lambda j: (0, j)),
                  pl.BlockSpec((1, tn), lambda j: (0, j))],
        out_specs=pl.BlockSpec((rows, tn), lambda j: (0, j)),
        out_shape=jax.ShapeDtypeStruct((rows, n), F32),
        compiler_params=_cparams(("arbitrary",)),
        name="ada",
    )(c_pad, w, b)


def _inproj_kernel(x_ref, mod_ref, n1w_ref, pos_ref, invf_ref, win_ref, qw_ref, kw_ref, gm_ref,
                   q_ref, kt_ref, v_ref, u_ref, *, q_scale):
    x = x_ref[0]
    tm = x.shape[0]
    sh1 = mod_ref[0, 0:1, :]
    sc1 = mod_ref[0, 1:2, :]
    h = (_rms(x) * n1w_ref[...]) * (1.0 + sc1) + sh1
    proj = jnp.dot(h.astype(BF16), win_ref[...], preferred_element_type=F32)

    ang = pos_ref[0] * invf_ref[...]
    cos = jnp.cos(ang)
    sin = jnp.sin(ang)
    d_in_head = lax.broadcasted_iota(jnp.int32, (1, LANES), 1) % QK_DIM
    half = ROT_DIM // 2
    s_lo = jnp.where(d_in_head < half, -sin, 0.0)
    s_hi = jnp.where((d_in_head >= half) & (d_in_head < ROT_DIM), sin, 0.0)
    reps = W_QK // LANES
    cos = jnp.concatenate([cos] * reps, axis=1)
    s_lo = jnp.concatenate([s_lo] * reps, axis=1)
    s_hi = jnp.concatenate([s_hi] * reps, axis=1)

    def qk_norm_rope(t, w_ref, scale):
        ssq = jnp.dot((t * t).astype(BF16), gm_ref[...], preferred_element_type=F32)
        tn = t * lax.rsqrt(ssq * (1.0 / QK_DIM) + RMS_EPS) * w_ref[...]
        r = (tn * cos + pltpu.roll(tn, W_QK - half, 1) * s_lo + pltpu.roll(tn, half, 1) * s_hi)
        return r * scale

    q = qk_norm_rope(proj[:, 0:W_QK], qw_ref, q_scale)
    k = qk_norm_rope(proj[:, W_QK:2 * W_QK], kw_ref, 1.0)
    v = proj[:, 2 * W_QK:2 * W_QK + W_ATTN]
    u_ref[0] = proj[:, 2 * W_QK + W_ATTN:]
    kt = k.T
    ones_col = jnp.where(lax.broadcasted_iota(jnp.int32, (tm, V_DIM), 1) == 0, 1.0, 0.0)
    hw = 2 * QK_DIM
    for hd in range(N_HEADS):
        q_ref[0, hd] = q[:, hd * hw:(hd + 1) * hw].astype(BF16)
        kt_ref[0, hd] = kt[hd * hw:(hd + 1) * hw, :].astype(BF16)
        v_ref[0, hd] = jnp.concatenate(
            [v[:, hd * V_DIM:(hd + 1) * V_DIM], ones_col], axis=1).astype(BF16)


def _inproj(x, mod3, n1w, pos3, invf, win_bf, qw, kw, gmat, tm):
    B, S, D = x.shape
    q_scale = math.log2(math.e) / math.sqrt(QK_DIM)
    kern = functools.partial(_inproj_kernel, q_scale=q_scale)
    const2 = lambda b, i: (0, 0)
    return pl.pallas_call(
        kern,
        grid=(B, S // tm),
        in_specs=[pl.BlockSpec((1, tm, D), lambda b, i: (b, i, 0)),
                  pl.BlockSpec((1, 6, D), lambda b, i: (b, 0, 0)),
                  pl.BlockSpec((1, D), const2),
                  pl.BlockSpec((1, tm, 1), lambda b, i: (b, i, 0)),
                  pl.BlockSpec((1, LANES), const2),
                  pl.BlockSpec((D, D_IN_PROJ), const2),
                  pl.BlockSpec((1, W_QK), const2),
                  pl.BlockSpec((1, W_QK), const2),
                  pl.BlockSpec((W_QK, W_QK), const2)],
        out_specs=[pl.BlockSpec((1, N_HEADS, tm, 2 * QK_DIM), lambda b, i: (b, 0, i, 0)),
                   pl.BlockSpec((1, N_HEADS, 2 * QK_DIM, tm), lambda b, i: (b, 0, 0, i)),
                   pl.BlockSpec((1, N_HEADS, tm, 2 * V_DIM), lambda b, i: (b, 0, i, 0)),
                   pl.BlockSpec((1, tm, W_S5), lambda b, i: (b, i, 0))],
        out_shape=[jax.ShapeDtypeStruct((B, N_HEADS, S, 2 * QK_DIM), BF16),
                   jax.ShapeDtypeStruct((B, N_HEADS, 2 * QK_DIM, S), BF16),
                   jax.ShapeDtypeStruct((B, N_HEADS, S, 2 * V_DIM), BF16),
                   jax.ShapeDtypeStruct((B, S, W_S5), F32)],
        compiler_params=_cparams(("parallel", "parallel")),
        name="inproj",
    )(x, mod3, n1w, pos3, invf, win_bf, qw, kw, gmat)


def _attn_kernel(lam_ref, sw_ref, q_ref, kt_ref, v_ref, o_ref, *, tk, heads_per_step):
    S = kt_ref.shape[3]
    tq = q_ref.shape[2]
    lp = lam_ref[...]
    lam = (jnp.exp(jnp.sum(lp[0:1] * lp[1:2], axis=-1, keepdims=True))
           - jnp.exp(jnp.sum(lp[2:3] * lp[3:4], axis=-1, keepdims=True)) + LAMBDA_INIT)
    outs = []
    for hh in range(heads_per_step):
        q = q_ref[0, hh]
        maps = []
        for c in range(2):
            qc = q[:, c * QK_DIM:(c + 1) * QK_DIM]

            def body(kb, carry, hh=hh, c=c, qc=qc):
                m, acc = carry
                off = pl.multiple_of(kb * tk, tk)
                kblk = kt_ref[0, hh, c * QK_DIM:(c + 1) * QK_DIM, pl.ds(off, tk)]
                s = jnp.dot(qc, kblk, preferred_element_type=F32)
                m_new = jnp.maximum(m, jnp.max(s, axis=-1, keepdims=True))
                alpha = jnp.exp2(m - m_new)
                p = jnp.exp2(s - m_new).astype(BF16)
                vblk = v_ref[0, hh, pl.ds(off, tk), :]
                acc = alpha * acc + jnp.dot(p, vblk, preferred_element_type=F32)
                return m_new, acc

            m0 = jnp.full((tq, 1), -jnp.inf, F32)
            acc0 = jnp.zeros((tq, 2 * V_DIM), F32)
            _, acc = lax.fori_loop(0, S // tk, body, (m0, acc0))
            maps.append(acc[:, :V_DIM] / acc[:, V_DIM:V_DIM + 1])
        o = maps[0] - lam * maps[1]
        outs.append(_rms(o) * sw_ref[...] * (1.0 - LAMBDA_INIT))
    o_ref[0] = jnp.concatenate(outs, axis=1).astype(o_ref.dtype)


def _attn(lam_p, sw, q, kt, v, tq, tk):
    B, H, S, _ = q.shape
    hps = 2
    kern = functools.partial(_attn_kernel, tk=tk, heads_per_step=hps)
    return pl.pallas_call(
        kern,
        grid=(B, H // hps, S // tq),
        in_specs=[pl.BlockSpec((4, QK_DIM), lambda b, h, i: (0, 0)),
                  pl.BlockSpec((1, V_DIM), lambda b, h, i: (0, 0)),
                  pl.BlockSpec((1, hps, tq, 2 * QK_DIM), lambda b, h, i: (b, h, i, 0)),
                  pl.BlockSpec((1, hps, 2 * QK_DIM, S), lambda b, h, i: (b, h, 0, 0)),
                  pl.BlockSpec((1, hps, S, 2 * V_DIM), lambda b, h, i: (b, h, 0, 0))],
        out_specs=pl.BlockSpec((1, tq, hps * V_DIM), lambda b, h, i: (b, i, h)),
        out_shape=jax.ShapeDtypeStruct((B, S, W_ATTN), BF16),
        compiler_params=_cparams(("parallel", "parallel", "parallel")),
        name="attn",
    )(lam_p, sw, q, kt, v)


def _s5_kernel(u_ref, d_ref, bm_ref, cm_ref, tr_ref, ti_ref, y_ref, xr_s, xi_s, *, rows):
    S = u_ref.shape[1]
    R = rows
    ng = R // SUBLANES
    nchunk = S // R
    P = STATE_LANES
    for di in range(2):
        bm = bm_ref[di, 0]
        cm = cm_ref[di, 0]
        tab_r = [tr_ref[di, 0, t] for t in range(4)]
        tab_i = [ti_ref[di, 0, t] for t in range(4)]

        def chunk_body(ci, carry, di=di, bm=bm, cm=cm, tab_r=tab_r, tab_i=tab_i):
            c = ci if di == 0 else nchunk - 1 - ci
            r0 = pl.multiple_of(c * R, R)
            u = u_ref[0, pl.ds(r0, R), :]
            bu = jnp.dot(u.astype(BF16), bm, preferred_element_type=F32)
            xr = bu[:, :P].reshape(ng, SUBLANES, P)
            xi = bu[:, P:].reshape(ng, SUBLANES, P)
            for t, k in enumerate((1, 2, 4)):
                sh = k if di == 0 else SUBLANES - k
                sr = pltpu.roll(xr, sh, 1)
                si = pltpu.roll(xi, sh, 1)
                ar = tab_r[t][None]
                ai = tab_i[t][None]
                xr, xi = xr + ar * sr - ai * si, xi + ar * si + ai * sr
            xr_s[...] = xr.reshape(R, P)
            xi_s[...] = xi.reshape(R, P)
            a8r = tab_r[3]
            a8i = tab_i[3]
            edge = SUBLANES - 1 if di == 0 else 0

            def row_body(jj, cc):
                cr, cim = cc
                j = jj if di == 0 else ng - 1 - jj
                o = pl.multiple_of(j * SUBLANES, SUBLANES)
                vr = xr_s[pl.ds(o, SUBLANES), :]
                vi = xi_s[pl.ds(o, SUBLANES), :]
                nr = vr + a8r * cr - a8i * cim
                ni = vi + a8r * cim + a8i * cr
                xr_s[pl.ds(o, SUBLANES), :] = nr
                xi_s[pl.ds(o, SUBLANES), :] = ni
                return nr[edge:edge + 1, :], ni[edge:edge + 1, :]

            carry = lax.fori_loop(0, ng, row_body, carry, unroll=4)
            yc = (jnp.dot(xr_s[...].astype(BF16), cm[:P], preferred_element_type=F32)
                  + jnp.dot(xi_s[...].astype(BF16), cm[P:], preferred_element_type=F32))
            if di == 0:
                y_ref[0, pl.ds(r0, R), :] = u * d_ref[0] + yc
            else:
                y_ref[0, pl.ds(r0, R), :] = y_ref[0, pl.ds(r0, R), :] + yc
            return carry

        zero = jnp.zeros((1, P), F32)
        lax.fori_loop(0, nchunk, chunk_body, (zero, zero))


def _s5(u, d3, bmat, cmat, tab_r, tab_i, rows):
    B, S, W = u.shape
    nb = W // LANES
    P = STATE_LANES
    kern = functools.partial(_s5_kernel, rows=rows)
    return pl.pallas_call(
        kern,
        grid=(B, nb),
        in_specs=[pl.BlockSpec((1, S, LANES), lambda b, g: (b, 0, g)),
                  pl.BlockSpec((1, 1, LANES), lambda b, g: (g, 0, 0)),
                  pl.BlockSpec((2, 1, LANES, 2 * P), lambda b, g: (0, g, 0, 0)),
                  pl.BlockSpec((2, 1, 2 * P, LANES), lambda b, g: (0, g, 0, 0)),
                  pl.BlockSpec((2, 1, 4, SUBLANES, P), lambda b, g: (0, g, 0, 0, 0)),
                  pl.BlockSpec((2, 1, 4, SUBLANES, P), lambda b, g: (0, g, 0, 0, 0))],
        out_specs=pl.BlockSpec((1, S, LANES), lambda b, g: (b, 0, g)),
        out_shape=jax.ShapeDtypeStruct((B, S, W), F32),
        scratch_shapes=[pltpu.VMEM((rows, P), F32), pltpu.VMEM((rows, P), F32)],
        compiler_params=_cparams(("parallel", "parallel")),
        name="s5",
    )(u, d3, bmat, cmat, tab_r, tab_i)


def _s5_tables(lam_re, lam_im, log_step, b_re, b_im, cm_re, cm_im):
    nb = S5_GROUPS // GROUPS_PER_BLOCK
    eye = jnp.eye(GROUPS_PER_BLOCK, dtype=F32)
    bmats, cmats, tabs_r, tabs_i = [], [], [], []
    for di in range(2):
        lr = lam_re[di].astype(F32)
        li = lam_im[di].astype(F32)
        delta = jnp.exp(log_step[di].astype(F32))[:, None]
        mag = jnp.exp(lr * delta)
        a_re = mag * jnp.cos(li * delta)
        a_im = mag * jnp.sin(li * delta)
        den = lr * lr + li * li
        num_re = a_re - 1.0
        f_re = (num_re * lr + a_im * li) / den
        f_im = (a_im * lr - num_re * li) / den
        br = b_re[di].astype(F32)
        bi = b_im[di].astype(F32)
        bbar_re = f_re[..., None] * br - f_im[..., None] * bi
        bbar_im = f_re[..., None] * bi + f_im[..., None] * br

        def blockdiag_in(bb):
            bb = bb.reshape(nb, GROUPS_PER_BLOCK, S5_STATE, S5_CH)
            m = jnp.einsum('bgpc,gh->bgchp', bb, eye)
            return m.reshape(nb, LANES, STATE_LANES)

        def blockdiag_out(cc):
            cc = cc.reshape(nb, GROUPS_PER_BLOCK, S5_CH, S5_STATE)
            m = jnp.einsum('bgcp,gh->bgphc', cc, eye)
            return m.reshape(nb, STATE_LANES, LANES)

        bmats.append(jnp.concatenate([blockdiag_in(bbar_re), blockdiag_in(bbar_im)], axis=2))
        cmats.append(jnp.concatenate([blockdiag_out(cm_re[di].astype(F32)),
                                      -blockdiag_out(cm_im[di].astype(F32))], axis=1))
        pr, pi = [a_re], [a_im]
        for _ in range(SUBLANES - 1):
            pr, pi = (pr + [pr[-1] * a_re - pi[-1] * a_im],
                      pi + [pr[-1] * a_im + pi[-1] * a_re])
        pr = jnp.stack(pr).reshape(SUBLANES, nb, STATE_LANES)
        pi = jnp.stack(pi).reshape(SUBLANES, nb, STATE_LANES)
        row = jnp.arange(SUBLANES)[:, None, None]
        tr, ti = [], []
        for k in (1, 2, 4):
            keep = (row >= k) if di == 0 else (row < SUBLANES - k)
            tr.append(jnp.where(keep, pr[k - 1][None], 0.0))
            ti.append(jnp.where(keep, pi[k - 1][None], 0.0))
        if di == 0:
            tr.append(pr)
            ti.append(pi)
        else:
            tr.append(pr[::-1])
            ti.append(pi[::-1])
        tabs_r.append(jnp.transpose(jnp.stack(tr), (2, 0, 1, 3)))
        tabs_i.append(jnp.transpose(jnp.stack(ti), (2, 0, 1, 3)))
    return (jnp.stack(bmats).astype(BF16), jnp.stack(cmats).astype(BF16),
            jnp.stack(tabs_r), jnp.stack(tabs_i))


def _post_kernel(a_ref, ys_ref, x_ref, mod_ref, gluw_ref, glub_ref, onw_ref, wout_ref, n2w_ref,
                 rw_ref, rb_ref, x1_ref, h2_ref, eidx_ref, gate_ref, rank_ref, cnt_ref, carry_s):
    first = (pl.program_id(0) == 0) & (pl.program_id(1) == 0)

    @pl.when(first)
    def _():
        carry_s[...] = jnp.zeros_like(carry_s)

    tm = x_ref.shape[1]
    y = ys_ref[0]
    y = 0.5 * y * (1.0 + jnp.tanh(math.sqrt(2.0 / math.pi) * (y + 0.044715 * (y * y * y))))
    g = jnp.dot(y.astype(BF16), gluw_ref[...], preferred_element_type=F32) + glub_ref[...]
    y = y * jax.nn.sigmoid(g)
    s = _rms(y) * onw_ref[...]
    mix = (jnp.dot(a_ref[0], wout_ref[0:W_ATTN, :], preferred_element_type=F32)
           + jnp.dot(s.astype(BF16), wout_ref[W_ATTN:, :], preferred_element_type=F32))
    g1 = mod_ref[0, 2:3, :]
    sh2 = mod_ref[0, 3:4, :]
    sc2 = mod_ref[0, 4:5, :]
    x1 = x_ref[0] + g1 * mix
    x1_ref[0] = x1
    h2 = (_rms(x1) * n2w_ref[...]) * (1.0 + sc2) + sh2
    h2_ref[0] = h2
    logits = jnp.dot(h2, rw_ref[...], preferred_element_type=F32,
                     precision=lax.Precision.HIGHEST) + rb_ref[...]
    lane = lax.broadcasted_iota(jnp.int32, (tm, LANES), 1)
    neg = jnp.float32(-jnp.inf)
    work = jnp.where(lane < N_EXPERTS, logits, neg)
    vals, idxs = [], []
    for _ in range(TOP_K):
        m = jnp.max(work, axis=-1, keepdims=True)
        idx = jnp.min(jnp.where(work == m, lane, LANES), axis=-1, keepdims=True)
        vals.append(m)
        idxs.append(idx)
        work = jnp.where(lane == idx, neg, work)
    es = [jnp.exp(v - vals[0]) for v in vals]
    den = es[0] + es[1] + es[2] + es[3]
    onehot = jnp.zeros((tm, LANES), F32)
    for idx in idxs:
        onehot = onehot + jnp.where(lane == idx, 1.0, 0.0)
    r_i = lax.broadcasted_iota(jnp.int32, (tm, tm), 0)
    c_i = lax.broadcasted_iota(jnp.int32, (tm, tm), 1)
    ltri = jnp.where(c_i < r_i, 1.0, 0.0).astype(BF16)
    before = jnp.dot(ltri, onehot.astype(BF16), preferred_element_type=F32) + carry_s[...]
    lane4 = lax.broadcasted_iota(jnp.int32, (tm, TOP_K), 1)
    e_out = jnp.zeros((tm, TOP_K), jnp.int32)
    g_out = jnp.zeros((tm, TOP_K), F32)
    r_out = jnp.zeros((tm, TOP_K), jnp.int32)
    for j in range(TOP_K):
        rk = jnp.sum(jnp.where(lane == idxs[j], before, 0.0), axis=-1, keepdims=True)
        e_out = jnp.where(lane4 == j, idxs[j], e_out)
        g_out = jnp.where(lane4 == j, es[j] / den, g_out)
        r_out = jnp.where(lane4 == j, rk.astype(jnp.int32), r_out)
    eidx_ref[0] = e_out
    gate_ref[0] = g_out
    rank_ref[0] = r_out
    carry_s[...] = carry_s[...] + jnp.sum(onehot, axis=0, keepdims=True)
    cnt_ref[...] = carry_s[...]


def _post(a, ys, x, mod3, gluw, glub, onw, wout, n2w, rw, rb, tm):
    B, S, D = x.shape
    c2 = lambda b, i: (0, 0)
    tok = lambda b, i: (b, i, 0)
    return pl.pallas_call(
        _post_kernel,
        grid=(B, S // tm),
        in_specs=[pl.BlockSpec((1, tm, W_ATTN), tok),
                  pl.BlockSpec((1, tm, W_S5), tok),
                  pl.BlockSpec((1, tm, D), tok),
                  pl.BlockSpec((1, 6, D), lambda b, i: (b, 0, 0)),
                  pl.BlockSpec((W_S5, W_S5), c2),
                  pl.BlockSpec((1, W_S5), c2),
                  pl.BlockSpec((1, W_S5), c2),
                  pl.BlockSpec((D, D), c2),
                  pl.BlockSpec((1, D), c2),
                  pl.BlockSpec((D, LANES), c2),
                  pl.BlockSpec((1, LANES), c2)],
        out_specs=[pl.BlockSpec((1, tm, D), tok),
                   pl.BlockSpec((1, tm, D), tok),
                   pl.BlockSpec((1, tm, TOP_K), tok),
                   pl.BlockSpec((1, tm, TOP_K), tok),
                   pl.BlockSpec((1, tm, TOP_K), tok),
                   pl.BlockSpec((1, LANES), c2)],
        out_shape=[jax.ShapeDtypeStruct((B, S, D), F32),
                   jax.ShapeDtypeStruct((B, S, D), F32),
                   jax.ShapeDtypeStruct((B, S, TOP_K), jnp.int32),
                   jax.ShapeDtypeStruct((B, S, TOP_K), F32),
                   jax.ShapeDtypeStruct((B, S, TOP_K), jnp.int32),
                   jax.ShapeDtypeStruct((1, LANES), F32)],
        scratch_shapes=[pltpu.VMEM((1, LANES), F32)],
        compiler_params=_cparams(("arbitrary", "arbitrary")),
        name="post",
    )(a, ys, x, mod3, gluw, glub, onw, wout, n2w, rw, rb)


def _dispatch_kernel(dest_ref, h2_hbm, xs_in_hbm, xs_hbm, sem, *, tg):
    del xs_in_hbm
    base = pl.program_id(0) * tg

    def row_copy(t, d):
        return pltpu.make_async_copy(h2_hbm.at[pl.ds(base + t, 1)], xs_hbm.at[pl.ds(d, 1)], sem)

    def issue(t, _):
        for j in range(TOP_K):
            row_copy(t, dest_ref[0, 0, t * TOP_K + j]).start()
        return 0

    lax.fori_loop(0, tg, issue, 0)

    def drain(t, _):
        for j in range(TOP_K):
            row_copy(t, dest_ref[0, 0, t * TOP_K + j]).wait()
        return 0

    lax.fori_loop(0, tg, drain, 0)


def _dispatch(dest3, h2, xs0, tg):
    T, D = h2.shape
    kern = functools.partial(_dispatch_kernel, tg=tg)
    return pl.pallas_call(
        kern,
        grid=(T // tg,),
        in_specs=[pl.BlockSpec((1, 1, tg * TOP_K), lambda i: (i, 0, 0), memory_space=pltpu.SMEM),
                  pl.BlockSpec(memory_space=pl.ANY),
                  pl.BlockSpec(memory_space=pl.ANY)],
        out_specs=pl.BlockSpec(memory_space=pl.ANY),
        out_shape=jax.ShapeDtypeStruct(xs0.shape, xs0.dtype),
        scratch_shapes=[pltpu.SemaphoreType.DMA(())],
        input_output_aliases={2: 0},
        compiler_params=_cparams(("arbitrary",)),
        name="dispatch",
    )(dest3, h2, xs0)


def _expert_kernel(be_ref, na_ref, x_ref, wg_ref, wl_ref, bg_ref, bl_ref, w2_ref, b2_ref, y_ref):
    i = pl.program_id(0)

    @pl.when(i < na_ref[0])
    def _():
        x = x_ref[...].astype(BF16)
        zg = jnp.dot(x, wg_ref[0], preferred_element_type=F32) + bg_ref[0]
        zl = jnp.dot(x, wl_ref[0], preferred_element_type=F32) + bl_ref[0]
        xg = jnp.minimum(zg, SWIGLU_LIMIT)
        xl = jnp.clip(zl, -SWIGLU_LIMIT, SWIGLU_LIMIT)
        act = xg * jax.nn.sigmoid(SWIGLU_ALPHA * xg) * (xl + 1.0)
        y_ref[...] = jnp.dot(act.astype(BF16), w2_ref[0], preferred_element_type=F32) + b2_ref[0]


def _experts(block_e, n_active, xs, wg, wl, bg, bl, w2, b2):
    n_rows, D = xs.shape
    nblk = n_rows // MOE_ROWS

    def row_map(i, be, na):
        return (jnp.minimum(i, na[0] - 1), 0)

    def w_map(i, be, na):
        return (be[i], 0, 0)

    grid_spec = pltpu.PrefetchScalarGridSpec(
        num_scalar_prefetch=2,
        grid=(nblk,),
        in_specs=[pl.BlockSpec((MOE_ROWS, D), row_map),
                  pl.BlockSpec((1, D, D_FF), w_map),
                  pl.BlockSpec((1, D, D_FF), w_map),
                  pl.BlockSpec((1, 1, D_FF), w_map),
                  pl.BlockSpec((1, 1, D_FF), w_map),
                  pl.BlockSpec((1, D_FF, D), w_map),
                  pl.BlockSpec((1, 1, D), w_map)],
        out_specs=pl.BlockSpec((MOE_ROWS, D), row_map),
    )
    return pl.pallas_call(
        _expert_kernel,
        grid_spec=grid_spec,
        out_shape=jax.ShapeDtypeStruct((n_rows, D), F32),
        compiler_params=_cparams(("arbitrary",)),
        name="experts",
    )(block_e, n_active, xs, wg, wl, bg, bl, w2, b2)


def _combine_kernel(dest_ref, gate_ref, x1_ref, mod_ref, ys_hbm, o_ref, buf, sem, *, tc):
    def row_copy(t, j, d):
        return pltpu.make_async_copy(ys_hbm.at[pl.ds(d, 1)], buf.at[j, pl.ds(t, 1)], sem)

    def issue(t, _):
        for j in range(TOP_K):
            row_copy(t, j, dest_ref[0, 0, t * TOP_K + j]).start()
        return 0

    lax.fori_loop(0, tc, issue, 0)

    def drain(t, _):
        for j in range(TOP_K):
            row_copy(t, j, dest_ref[0, 0, t * TOP_K + j]).wait()
        return 0

    lax.fori_loop(0, tc, drain, 0)
    gates = gate_ref[...]
    acc = gates[:, 0:1] * buf[0]
    for j in range(1, TOP_K):
        acc = acc + gates[:, j:j + 1] * buf[j]
    o_ref[...] = x1_ref[...] + mod_ref[0, 5:6, :] * acc


def _combine(dest3, gates, x1, mod3, ys, tc, tiles_per_batch):
    T, D = x1.shape
    kern = functools.partial(_combine_kernel, tc=tc)
    return pl.pallas_call(
        kern,
        grid=(T // tc,),
        in_specs=[pl.BlockSpec((1, 1, tc * TOP_K), lambda i: (i, 0, 0), memory_space=pltpu.SMEM),
                  pl.BlockSpec((tc, TOP_K), lambda i: (i, 0)),
                  pl.BlockSpec((tc, D), lambda i: (i, 0)),
                  pl.BlockSpec((1, 6, D), lambda i: (i // tiles_per_batch, 0, 0)),
                  pl.BlockSpec(memory_space=pl.ANY)],
        out_specs=pl.BlockSpec((tc, D), lambda i: (i, 0)),
        out_shape=jax.ShapeDtypeStruct((T, D), F32),
        scratch_shapes=[pltpu.VMEM((TOP_K, tc, D), F32), pltpu.SemaphoreType.DMA(())],
        compiler_params=_cparams(("arbitrary",)),
        name="combine",
    )(dest3, gates, x1, mod3, ys)


def kernel(x, c, positions, ada_w, ada_b, norm1_w, w_in, q_norm_w, k_norm_w, lambda_q1, lambda_k1,
           lambda_q2, lambda_k2, subln_w, s5_lambda_re, s5_lambda_im, s5_log_step, s5_b_re, s5_b_im,
           s5_cmat_re, s5_cmat_im, s5_d, s5_glu_w, s5_glu_b, s5_out_norm_w, w_out, norm2_w,
           router_w, router_b, mlp1_w, mlp1_b, mlp2_w, mlp2_b):
    B, S, D = x.shape
    T = B * S
    l = 0

    c_pad = jnp.pad(c, ((0, SUBLANES - B % SUBLANES if B % SUBLANES else 0), (0, 0)))
    mod = _ada(c_pad, ada_w[l], ada_b[l][None, :])[:B]
    mod3 = mod.reshape(B, 6, D)

    tm = min(512, S)
    inv_freq = ROPE_THETA ** (-jnp.arange(0, ROT_DIM, 2, dtype=F32) / ROT_DIM)
    d_in_head = jnp.arange(LANES) % QK_DIM
    invf = jnp.where(d_in_head < ROT_DIM, inv_freq[d_in_head % (ROT_DIM // 2)], 0.0)[None, :]
    pos3 = positions.astype(F32)[..., None]
    gmat = jnp.kron(jnp.eye(W_QK // QK_DIM, dtype=F32), jnp.ones((QK_DIM, QK_DIM), F32)).astype(BF16)
    qw = jnp.tile(q_norm_w[l], W_QK // QK_DIM)[None, :]
    kw = jnp.tile(k_norm_w[l], W_QK // QK_DIM)[None, :]
    q, kt, v, u = _inproj(x, mod3, norm1_w[l][None, :], pos3, invf, w_in[l].astype(BF16),
                          qw, kw, gmat, tm)

    lam_p = jnp.stack([lambda_q1[l], lambda_k1[l], lambda_q2[l], lambda_k2[l]]).astype(F32)
    a_out = _attn(lam_p, subln_w[l][None, :], q, kt, v, tq=min(256, S), tk=min(1024, S))

    bmat, cmat, tab_r, tab_i = _s5_tables(s5_lambda_re[l], s5_lambda_im[l], s5_log_step[l],
                                          s5_b_re[l], s5_b_im[l], s5_cmat_re[l], s5_cmat_im[l])
    d3 = s5_d[l].astype(F32).reshape(W_S5 // LANES, 1, LANES)
    y_s5 = _s5(u, d3, bmat, cmat, tab_r, tab_i, rows=min(256, S))

    rw = jnp.pad(router_w[l], ((0, 0), (0, LANES - N_EXPERTS)))
    rb = jnp.pad(router_b[l], (0, LANES - N_EXPERTS))[None, :]
    tp = min(256, S)
    x1, h2, eidx, gates, rank, counts = _post(
        a_out, y_s5, x, mod3, s5_glu_w[l].astype(BF16), s5_glu_b[l][None, :],
        s5_out_norm_w[l][None, :], w_out[l].astype(BF16), norm2_w[l][None, :], rw, rb, tp)

    counts = counts[0, :N_EXPERTS].astype(jnp.int32)
    padded = ((counts + MOE_ROWS - 1) // MOE_ROWS) * MOE_ROWS
    pad_end = jnp.cumsum(padded)
    pad_start = pad_end - padded
    eflat = eidx.reshape(T * TOP_K)
    dest = pad_start[eflat] + rank.reshape(T * TOP_K)
    n_rows = T * TOP_K + N_EXPERTS * MOE_ROWS
    nblk = n_rows // MOE_ROWS
    block_e = jnp.minimum(
        jnp.searchsorted(pad_end, jnp.arange(nblk, dtype=jnp.int32) * MOE_ROWS, side='right'),
        N_EXPERTS - 1).astype(jnp.int32)
    n_active = (pad_end[-1] // MOE_ROWS).astype(jnp.int32)[None]

    tg = min(512, T)
    xs0 = jnp.zeros((n_rows, D), F32)
    xs = _dispatch(dest.reshape(T // tg, 1, tg * TOP_K), h2.reshape(T, D), xs0, tg)

    w1 = mlp1_w[l]
    wg = w1[:, :, 0::2].astype(BF16)
    wl = w1[:, :, 1::2].astype(BF16)
    bg = mlp1_b[l][:, None, 0::2]
    bl = mlp1_b[l][:, None, 1::2]
    ys = _experts(block_e, n_active, xs, wg, wl, bg, bl, mlp2_w[l].astype(BF16),
                  mlp2_b[l][:, None, :])

    tc = min(256, S)
    out = _combine(dest.reshape(T // tc, 1, tc * TOP_K), gates.reshape(T, TOP_K),
                   x1.reshape(T, D), mod3, ys, tc, S // tc)
    return out.reshape(B, S, D)
```

```python
import functools
import math

import jax
import jax.numpy as jnp
from jax import lax
from jax.experimental import pallas as pl
from jax.experimental.pallas import tpu as pltpu

F32 = jnp.float32
BF16 = jnp.bfloat16

D_MODEL = 1024
QK_DIM = 32
V_DIM = 64
N_HEADS = 8
W_QK = N_HEADS * 2 * QK_DIM
W_ATTN = N_HEADS * V_DIM
ROT_DIM = QK_DIM // 4
ROPE_THETA = 500000.0
S5_CH = 16
S5_STATE = 64
W_S5 = 512
S5_GROUPS = W_S5 // S5_CH
D_IN_PROJ = 2 * W_QK + W_ATTN + W_S5
N_EXPERTS = 32
TOP_K = 4
D_FF = D_MODEL
SWIGLU_ALPHA = 1.702
SWIGLU_LIMIT = 7.0
RMS_EPS = 1e-6
LAMBDA_INIT = 0.8 - 0.6 * math.exp(-0.3 * 0)

LANES = 128
SUBLANES = 8
VMEM_LIMIT = 56 * 1024 * 1024

GROUPS_PER_BLOCK = LANES // S5_CH
STATE_LANES = GROUPS_PER_BLOCK * S5_STATE
MOE_ROWS = 512
VT_ROWS = V_DIM + 16


def _cparams(sem):
    return pltpu.CompilerParams(dimension_semantics=sem, vmem_limit_bytes=VMEM_LIMIT)


def _rms(x, eps=RMS_EPS):
    return x * lax.rsqrt(jnp.mean(x * x, axis=-1, keepdims=True) + eps)


def _ada_kernel(c_ref, w_ref, b_ref, o_ref):
    c = c_ref[...]
    ca = c * jax.nn.sigmoid(c)
    o_ref[...] = jnp.dot(ca, w_ref[...], preferred_element_type=F32,
                         precision=lax.Precision.HIGHEST) + b_ref[...]


def _ada(c_pad, w, b):
    rows, d = c_pad.shape
    n = w.shape[1]
    tn = 1536
    return pl.pallas_call(
        _ada_kernel,
        grid=(n // tn,),
        in_specs=[pl.BlockSpec((rows, d), lambda j: (0, 0)),
                  pl.BlockSpec((d, tn), lambda j: (0, j)),
                  pl.BlockSpec((1, tn), lambda j: (0, j))],
        out_specs=pl.BlockSpec((rows, tn), lambda j: (0, j)),
        out_shape=jax.ShapeDtypeStruct((rows, n), F32),
        compiler_params=_cparams(("arbitrary",)),
        name="ada",
    )(c_pad, w, b)


def _inproj_kernel(x_ref, mod_ref, n1w_ref, pos_ref, invf_ref, win_ref, qw_ref, kw_ref, gm_ref,
                   qt_ref, k_ref, vt_ref, u_ref, *, q_scale):
    x = x_ref[0]
    tm = x.shape[0]
    sh1 = mod_ref[0, 0:1, :]
    sc1 = mod_ref[0, 1:2, :]
    h = (_rms(x) * n1w_ref[...]) * (1.0 + sc1) + sh1
    proj = jnp.dot(h.astype(BF16), win_ref[...], preferred_element_type=F32)

    ang = pos_ref[0] * invf_ref[...]
    cos = jnp.cos(ang)
    sin = jnp.sin(ang)
    d_in_head = lax.broadcasted_iota(jnp.int32, (1, LANES), 1) % QK_DIM
    half = ROT_DIM // 2
    s_lo = jnp.where(d_in_head < half, -sin, 0.0)
    s_hi = jnp.where((d_in_head >= half) & (d_in_head < ROT_DIM), sin, 0.0)
    reps = W_QK // LANES
    cos = jnp.concatenate([cos] * reps, axis=1)
    s_lo = jnp.concatenate([s_lo] * reps, axis=1)
    s_hi = jnp.concatenate([s_hi] * reps, axis=1)

    def qk_norm_rope(t, w_ref, scale):
        ssq = jnp.dot((t * t).astype(BF16), gm_ref[...], preferred_element_type=F32)
        tn = t * lax.rsqrt(ssq * (1.0 / QK_DIM) + RMS_EPS) * w_ref[...]
        r = (tn * cos + pltpu.roll(tn, W_QK - half, 1) * s_lo + pltpu.roll(tn, half, 1) * s_hi)
        return r * scale

    q = qk_norm_rope(proj[:, 0:W_QK], qw_ref, q_scale)
    k = qk_norm_rope(proj[:, W_QK:2 * W_QK], kw_ref, 1.0)
    v = proj[:, 2 * W_QK:2 * W_QK + W_ATTN]
    u_ref[0] = proj[:, 2 * W_QK + W_ATTN:]
    qt = q.T
    vt = v.T
    hw = 2 * QK_DIM
    row = lax.broadcasted_iota(jnp.int32, (hw, tm), 0)
    tail = jnp.where(lax.broadcasted_iota(jnp.int32, (VT_ROWS - V_DIM, tm), 0) == 0,
                     1.0, 0.0).astype(BF16)
    for hd in range(N_HEADS):
        qh = qt[hd * hw:(hd + 1) * hw, :]
        qt_ref[0, hd, 0] = jnp.where(row < QK_DIM, qh, 0.0).astype(BF16)
        qt_ref[0, hd, 1] = jnp.where(row >= QK_DIM, qh, 0.0).astype(BF16)
        k_ref[0, hd] = k[:, hd * hw:(hd + 1) * hw].astype(BF16)
        vt_ref[0, hd, 0:V_DIM, :] = vt[hd * V_DIM:(hd + 1) * V_DIM, :].astype(BF16)
        vt_ref[0, hd, V_DIM:, :] = tail


def _inproj(x, mod3, n1w, pos3, invf, win_bf, qw, kw, gmat, tm):
    B, S, D = x.shape
    q_scale = math.log2(math.e) / math.sqrt(QK_DIM)
    kern = functools.partial(_inproj_kernel, q_scale=q_scale)
    const2 = lambda b, i: (0, 0)
    return pl.pallas_call(
        kern,
        grid=(B, S // tm),
        in_specs=[pl.BlockSpec((1, tm, D), lambda b, i: (b, i, 0)),
                  pl.BlockSpec((1, 6, D), lambda b, i: (b, 0, 0)),
                  pl.BlockSpec((1, D), const2),
                  pl.BlockSpec((1, tm, 1), lambda b, i: (b, i, 0)),
                  pl.BlockSpec((1, LANES), const2),
                  pl.BlockSpec((D, D_IN_PROJ), const2),
                  pl.BlockSpec((1, W_QK), const2),
                  pl.BlockSpec((1, W_QK), const2),
                  pl.BlockSpec((W_QK, W_QK), const2)],
        out_specs=[pl.BlockSpec((1, N_HEADS, 2, 2 * QK_DIM, tm), lambda b, i: (b, 0, 0, 0, i)),
                   pl.BlockSpec((1, N_HEADS, tm, 2 * QK_DIM), lambda b, i: (b, 0, i, 0)),
                   pl.BlockSpec((1, N_HEADS, VT_ROWS, tm), lambda b, i: (b, 0, 0, i)),
                   pl.BlockSpec((1, tm, W_S5), lambda b, i: (b, i, 0))],
        out_shape=[jax.ShapeDtypeStruct((B, N_HEADS, 2, 2 * QK_DIM, S), BF16),
                   jax.ShapeDtypeStruct((B, N_HEADS, S, 2 * QK_DIM), BF16),
                   jax.ShapeDtypeStruct((B, N_HEADS, VT_ROWS, S), BF16),
                   jax.ShapeDtypeStruct((B, S, W_S5), F32)],
        compiler_params=_cparams(("arbitrary", "arbitrary")),
        name="inproj",
    )(x, mod3, n1w, pos3, invf, win_bf, qw, kw, gmat)


def _attn_kernel(lam_ref, sw_ref, qt_ref, k_ref, vt_ref, o_ref, st_a, st_b, pt_a, pt_b, acc_s, *,
                 tk, heads_per_step):
    S = k_ref.shape[2]
    tq = qt_ref.shape[4]
    lp = lam_ref[...]
    lam = (jnp.exp(jnp.sum(lp[0:1] * lp[1:2], axis=-1, keepdims=True))
           - jnp.exp(jnp.sum(lp[2:3] * lp[3:4], axis=-1, keepdims=True)) + LAMBDA_INIT)
    chains = [(hh, c) for hh in range(heads_per_step) for c in range(2)]
    qts = [qt_ref[0, hh, c] for hh, c in chains]

    def scores(kb, st_buf):
        off = pl.multiple_of(kb * tk, tk)
        for ci, ((hh, c), qt) in enumerate(zip(chains, qts)):
            st_buf[ci] = jnp.dot(k_ref[0, hh, pl.ds(off, tk), :], qt,
                                 preferred_element_type=F32)

    def softmax(st_buf, pt_buf, ms):
        m_new, alphas = [], []
        for ci, m in enumerate(ms):
            mn = jnp.maximum(m, jnp.max(st_buf[ci], axis=0, keepdims=True))
            alphas.append(jnp.exp2(m - mn))
            pt_buf[ci] = jnp.exp2(st_buf[ci] - mn).astype(BF16)
            m_new.append(mn)
        return tuple(m_new), tuple(alphas)

    def accumulate(kb, pt_buf, alphas):
        off = pl.multiple_of(kb * tk, tk)
        for ci, ((hh, c), alpha) in enumerate(zip(chains, alphas)):
            pv = jnp.dot(vt_ref[0, hh, :, pl.ds(off, tk)], pt_buf[ci],
                         preferred_element_type=F32)
            acc_s[ci] = alpha * acc_s[ci] + pv

    def step(kb, st_cur, pt_cur, st_nxt, pt_prev, ms, alphas_prev):
        accumulate(kb - 1, pt_prev, alphas_prev)
        scores(kb + 1, st_nxt)
        return softmax(st_cur, pt_cur, ms)

    def body(jj, carry):
        ms, alphas = carry
        kb = 2 * jj + 1
        ms, alphas = step(kb, st_b, pt_b, st_a, pt_a, ms, alphas)
        return step(kb + 1, st_a, pt_a, st_b, pt_b, ms, alphas)

    nkb = S // tk
    assert nkb % 2 == 0
    acc_s[...] = jnp.zeros_like(acc_s)
    m0 = tuple(jnp.full((1, tq), -jnp.inf, F32) for _ in chains)
    scores(0, st_a)
    scores(1, st_b)
    ms, alphas = softmax(st_a, pt_a, m0)
    ms, alphas = lax.fori_loop(0, nkb // 2 - 1, body, (ms, alphas))
    accumulate(nkb - 2, pt_a, alphas)
    ms, alphas = softmax(st_b, pt_b, ms)
    accumulate(nkb - 1, pt_b, alphas)
    res = tuple((None, acc_s[ci]) for ci in range(len(chains)))
    outs = []
    for hh in range(heads_per_step):
        a0 = res[2 * hh][1]
        a1 = res[2 * hh + 1][1]
        ot = (a0[:V_DIM, :] / a0[V_DIM:V_DIM + 1, :]
              - lam * (a1[:V_DIM, :] / a1[V_DIM:V_DIM + 1, :]))
        ot = ot * lax.rsqrt(jnp.mean(ot * ot, axis=0, keepdims=True) + RMS_EPS)
        outs.append(ot)
    o = jnp.concatenate(outs, axis=0).T
    o_ref[0] = (o * sw_ref[...] * (1.0 - LAMBDA_INIT)).astype(o_ref.dtype)


def _attn(lam_p, sw2, qt, k, vt, tq, tk):
    B, H, S, _ = k.shape
    hps = 2
    kern = functools.partial(_attn_kernel, tk=tk, heads_per_step=hps)
    return pl.pallas_call(
        kern,
        grid=(B, H // hps, S // tq),
        in_specs=[pl.BlockSpec((4, QK_DIM), lambda b, h, i: (0, 0)),
                  pl.BlockSpec((1, hps * V_DIM), lambda b, h, i: (0, 0)),
                  pl.BlockSpec((1, hps, 2, 2 * QK_DIM, tq), lambda b, h, i: (b, h, 0, 0, i)),
                  pl.BlockSpec((1, hps, S, 2 * QK_DIM), lambda b, h, i: (b, h, 0, 0)),
                  pl.BlockSpec((1, hps, VT_ROWS, S), lambda b, h, i: (b, h, 0, 0))],
        out_specs=pl.BlockSpec((1, tq, hps * V_DIM), lambda b, h, i: (b, i, h)),
        out_shape=jax.ShapeDtypeStruct((B, S, W_ATTN), BF16),
        scratch_shapes=[pltpu.VMEM((2 * hps, tk, tq), F32), pltpu.VMEM((2 * hps, tk, tq), F32),
                        pltpu.VMEM((2 * hps, tk, tq), BF16), pltpu.VMEM((2 * hps, tk, tq), BF16),
                        pltpu.VMEM((2 * hps, VT_ROWS, tq), F32)],
        compiler_params=_cparams(("arbitrary", "arbitrary", "arbitrary")),
        name="attn",
    )(lam_p, sw2, qt, k, vt)


def _s5_kernel(u_ref, d_ref, bm_ref, cm_ref, tr_ref, ti_ref, y_ref, xr_s, xi_s, *, rows):
    S = u_ref.shape[1]
    R = rows
    ng = R // SUBLANES
    nchunk = S // R
    P = STATE_LANES
    for di in range(2):
        bm = bm_ref[di, 0]
        cm = cm_ref[di, 0]
        tab_r = [tr_ref[di, 0, t] for t in range(4)]
        tab_i = [ti_ref[di, 0, t] for t in range(4)]

        def chunk_body(ci, carry, di=di, bm=bm, cm=cm, tab_r=tab_r, tab_i=tab_i):
            c = ci if di == 0 else nchunk - 1 - ci
            r0 = pl.multiple_of(c * R, R)
            u = u_ref[0, pl.ds(r0, R), :]
            bu = jnp.dot(u.astype(BF16), bm, preferred_element_type=F32)
            xr = bu[:, :P].reshape(ng, SUBLANES, P)
            xi = bu[:, P:].reshape(ng, SUBLANES, P)
            for t, k in enumerate((1, 2, 4)):
                sh = k if di == 0 else SUBLANES - k
                sr = pltpu.roll(xr, sh, 1)
                si = pltpu.roll(xi, sh, 1)
                ar = tab_r[t][None]
                ai = tab_i[t][None]
                xr, xi = xr + ar * sr - ai * si, xi + ar * si + ai * sr
            xr_s[...] = xr.reshape(R, P)
            xi_s[...] = xi.reshape(R, P)
            a8r = tab_r[3]
            a8i = tab_i[3]
            edge = SUBLANES - 1 if di == 0 else 0

            def row_body(jj, cc):
                cr, cim = cc
                j = jj if di == 0 else ng - 1 - jj
                o = pl.multiple_of(j * SUBLANES, SUBLANES)
                vr = xr_s[pl.ds(o, SUBLANES), :]
                vi = xi_s[pl.ds(o, SUBLANES), :]
                nr = vr + a8r * cr - a8i * cim
                ni = vi + a8r * cim + a8i * cr
                xr_s[pl.ds(o, SUBLANES), :] = nr
                xi_s[pl.ds(o, SUBLANES), :] = ni
                return nr[edge:edge + 1, :], ni[edge:edge + 1, :]

            carry = lax.fori_loop(0, ng, row_body, carry, unroll=4)
            yc = (jnp.dot(xr_s[...].astype(BF16), cm[:P], preferred_element_type=F32)
                  + jnp.dot(xi_s[...].astype(BF16), cm[P:], preferred_element_type=F32))
            if di == 0:
                y_ref[0, pl.ds(r0, R), :] = u * d_ref[0] + yc
            else:
                y_ref[0, pl.ds(r0, R), :] = y_ref[0, pl.ds(r0, R), :] + yc
            return carry

        zero = jnp.zeros((1, P), F32)
        lax.fori_loop(0, nchunk, chunk_body, (zero, zero))


def _s5(u, d3, bmat, cmat, tab_r, tab_i, rows):
    B, S, W = u.shape
    nb = W // LANES
    P = STATE_LANES
    kern = functools.partial(_s5_kernel, rows=rows)
    return pl.pallas_call(
        kern,
        grid=(B, nb),
        in_specs=[pl.BlockSpec((1, S, LANES), lambda b, g: (b, 0, g)),
                  pl.BlockSpec((1, 1, LANES), lambda b, g: (g, 0, 0)),
                  pl.BlockSpec((2, 1, LANES, 2 * P), lambda b, g: (0, g, 0, 0)),
                  pl.BlockSpec((2, 1, 2 * P, LANES), lambda b, g: (0, g, 0, 0)),
                  pl.BlockSpec((2, 1, 4, SUBLANES, P), lambda b, g: (0, g, 0, 0, 0)),
                  pl.BlockSpec((2, 1, 4, SUBLANES, P), lambda b, g: (0, g, 0, 0, 0))],
        out_specs=pl.BlockSpec((1, S, LANES), lambda b, g: (b, 0, g)),
        out_shape=jax.ShapeDtypeStruct((B, S, W), F32),
        scratch_shapes=[pltpu.VMEM((rows, P), F32), pltpu.VMEM((rows, P), F32)],
        compiler_params=_cparams(("arbitrary", "arbitrary")),
        name="s5",
    )(u, d3, bmat, cmat, tab_r, tab_i)


def _s5_tables(lam_re, lam_im, log_step, b_re, b_im, cm_re, cm_im):
    nb = S5_GROUPS // GROUPS_PER_BLOCK
    eye = jnp.eye(GROUPS_PER_BLOCK, dtype=F32)
    bmats, cmats, tabs_r, tabs_i = [], [], [], []
    for di in range(2):
        lr = lam_re[di].astype(F32)
        li = lam_im[di].astype(F32)
        delta = jnp.exp(log_step[di].astype(F32))[:, None]
        mag = jnp.exp(lr * delta)
        a_re = mag * jnp.cos(li * delta)
        a_im = mag * jnp.sin(li * delta)
        den = lr * lr + li * li
        num_re = a_re - 1.0
        f_re = (num_re * lr + a_im * li) / den
        f_im = (a_im * lr - num_re * li) / den
        br = b_re[di].astype(F32)
        bi = b_im[di].astype(F32)
        bbar_re = f_re[..., None] * br - f_im[..., None] * bi
        bbar_im = f_re[..., None] * bi + f_im[..., None] * br

        def blockdiag_in(bb):
            bb = bb.reshape(nb, GROUPS_PER_BLOCK, S5_STATE, S5_CH)
            m = jnp.einsum('bgpc,gh->bgchp', bb, eye)
            return m.reshape(nb, LANES, STATE_LANES)

        def blockdiag_out(cc):
            cc = cc.reshape(nb, GROUPS_PER_BLOCK, S5_CH, S5_STATE)
            m = jnp.einsum('bgcp,gh->bgphc', cc, eye)
            return m.reshape(nb, STATE_LANES, LANES)

        bmats.append(jnp.concatenate([blockdiag_in(bbar_re), blockdiag_in(bbar_im)], axis=2))
        cmats.append(jnp.concatenate([blockdiag_out(cm_re[di].astype(F32)),
                                      -blockdiag_out(cm_im[di].astype(F32))], axis=1))
        pr, pi = [a_re], [a_im]
        for _ in range(SUBLANES - 1):
            pr, pi = (pr + [pr[-1] * a_re - pi[-1] * a_im],
                      pi + [pr[-1] * a_im + pi[-1] * a_re])
        pr = jnp.stack(pr).reshape(SUBLANES, nb, STATE_LANES)
        pi = jnp.stack(pi).reshape(SUBLANES, nb, STATE_LANES)
        row = jnp.arange(SUBLANES)[:, None, None]
        tr, ti = [], []
        for k in (1, 2, 4):
            keep = (row >= k) if di == 0 else (row < SUBLANES - k)
            tr.append(jnp.where(keep, pr[k - 1][None], 0.0))
            ti.append(jnp.where(keep, pi[k - 1][None], 0.0))
        if di == 0:
            tr.append(pr)
            ti.append(pi)
        else:
            tr.append(pr[::-1])
            ti.append(pi[::-1])
        tabs_r.append(jnp.transpose(jnp.stack(tr), (2, 0, 1, 3)))
        tabs_i.append(jnp.transpose(jnp.stack(ti), (2, 0, 1, 3)))
    return (jnp.stack(bmats).astype(BF16), jnp.stack(cmats).astype(BF16),
            jnp.stack(tabs_r), jnp.stack(tabs_i))


def _post_kernel(a_ref, ys_ref, x_ref, mod_ref, gluw_ref, glub_ref, onw_ref, wout_ref, n2w_ref,
                 rw_ref, rb_ref, x1_ref, h2_ref, eidx_ref, gate_ref, rank_ref, cnt_ref, carry_s):
    first = (pl.program_id(0) == 0) & (pl.program_id(1) == 0)

    @pl.when(first)
    def _():
        carry_s[...] = jnp.zeros_like(carry_s)

    tm = x_ref.shape[1]
    y = ys_ref[0]
    y = 0.5 * y * (1.0 + jnp.tanh(math.sqrt(2.0 / math.pi) * (y + 0.044715 * (y * y * y))))
    g = jnp.dot(y.astype(BF16), gluw_ref[...], preferred_element_type=F32) + glub_ref[...]
    y = y * jax.nn.sigmoid(g)
    s = _rms(y) * onw_ref[...]
    mix = (jnp.dot(a_ref[0], wout_ref[0:W_ATTN, :], preferred_element_type=F32)
           + jnp.dot(s.astype(BF16), wout_ref[W_ATTN:, :], preferred_element_type=F32))
    g1 = mod_ref[0, 2:3, :]
    sh2 = mod_ref[0, 3:4, :]
    sc2 = mod_ref[0, 4:5, :]
    x1 = x_ref[0] + g1 * mix
    x1_ref[0] = x1
    h2 = (_rms(x1) * n2w_ref[...]) * (1.0 + sc2) + sh2
    h2_ref[0] = h2
    logits = jnp.dot(h2, rw_ref[...], preferred_element_type=F32,
                     precision=lax.Precision.HIGHEST) + rb_ref[...]
    lane = lax.broadcasted_iota(jnp.int32, (tm, LANES), 1)
    neg = jnp.float32(-jnp.inf)
    work = jnp.where(lane < N_EXPERTS, logits, neg)
    vals, idxs = [], []
    for _ in range(TOP_K):
        m = jnp.max(work, axis=-1, keepdims=True)
        idx = jnp.min(jnp.where(work == m, lane, LANES), axis=-1, keepdims=True)
        vals.append(m)
        idxs.append(idx)
        work = jnp.where(lane == idx, neg, work)
    es = [jnp.exp(v - vals[0]) for v in vals]
    den = es[0] + es[1] + es[2] + es[3]
    onehot = jnp.zeros((tm, LANES), F32)
    for idx in idxs:
        onehot = onehot + jnp.where(lane == idx, 1.0, 0.0)
    r_i = lax.broadcasted_iota(jnp.int32, (tm, tm), 0)
    c_i = lax.broadcasted_iota(jnp.int32, (tm, tm), 1)
    ltri = jnp.where(c_i < r_i, 1.0, 0.0).astype(BF16)
    before = jnp.dot(ltri, onehot.astype(BF16), preferred_element_type=F32) + carry_s[...]
    lane4 = lax.broadcasted_iota(jnp.int32, (tm, TOP_K), 1)
    e_out = jnp.zeros((tm, TOP_K), jnp.int32)
    g_out = jnp.zeros((tm, TOP_K), F32)
    r_out = jnp.zeros((tm, TOP_K), jnp.int32)
    for j in range(TOP_K):
        rk = jnp.sum(jnp.where(lane == idxs[j], before, 0.0), axis=-1, keepdims=True)
        e_out = jnp.where(lane4 == j, idxs[j], e_out)
        g_out = jnp.where(lane4 == j, es[j] / den, g_out)
        r_out = jnp.where(lane4 == j, rk.astype(jnp.int32), r_out)
    eidx_ref[0] = e_out
    gate_ref[0] = g_out
    rank_ref[0] = r_out
    carry_s[...] = carry_s[...] + jnp.sum(onehot, axis=0, keepdims=True)
    cnt_ref[...] = carry_s[...]


def _post(a, ys, x, mod3, gluw, glub, onw, wout, n2w, rw, rb, tm):
    B, S, D = x.shape
    c2 = lambda b, i: (0, 0)
    tok = lambda b, i: (b, i, 0)
    return pl.pallas_call(
        _post_kernel,
        grid=(B, S // tm),
        in_specs=[pl.BlockSpec((1, tm, W_ATTN), tok),
                  pl.BlockSpec((1, tm, W_S5), tok),
                  pl.BlockSpec((1, tm, D), tok),
                  pl.BlockSpec((1, 6, D), lambda b, i: (b, 0, 0)),
                  pl.BlockSpec((W_S5, W_S5), c2),
                  pl.BlockSpec((1, W_S5), c2),
                  pl.BlockSpec((1, W_S5), c2),
                  pl.BlockSpec((D, D), c2),
                  pl.BlockSpec((1, D), c2),
                  pl.BlockSpec((D, LANES), c2),
                  pl.BlockSpec((1, LANES), c2)],
        out_specs=[pl.BlockSpec((1, tm, D), tok),
                   pl.BlockSpec((1, tm, D), tok),
                   pl.BlockSpec((1, tm, TOP_K), tok),
                   pl.BlockSpec((1, tm, TOP_K), tok),
                   pl.BlockSpec((1, tm, TOP_K), tok),
                   pl.BlockSpec((1, LANES), c2)],
        out_shape=[jax.ShapeDtypeStruct((B, S, D), F32),
                   jax.ShapeDtypeStruct((B, S, D), F32),
                   jax.ShapeDtypeStruct((B, S, TOP_K), jnp.int32),
                   jax.ShapeDtypeStruct((B, S, TOP_K), F32),
                   jax.ShapeDtypeStruct((B, S, TOP_K), jnp.int32),
                   jax.ShapeDtypeStruct((1, LANES), F32)],
        scratch_shapes=[pltpu.VMEM((1, LANES), F32)],
        compiler_params=_cparams(("arbitrary", "arbitrary")),
        name="post",
    )(a, ys, x, mod3, gluw, glub, onw, wout, n2w, rw, rb)


def _dispatch_kernel(dest_ref, h2_ref, xs_in_hbm, xs_hbm, sem, *, tg):
    del xs_in_hbm

    def row_copy(t, d):
        return pltpu.make_async_copy(h2_ref.at[pl.ds(t, 1)], xs_hbm.at[pl.ds(d, 1)], sem)

    def issue(t, _):
        for j in range(TOP_K):
            row_copy(t, dest_ref[0, 0, t * TOP_K + j]).start()
        return 0

    lax.fori_loop(0, tg, issue, 0)

    def drain(t, _):
        for j in range(TOP_K):
            row_copy(t, dest_ref[0, 0, t * TOP_K + j]).wait()
        return 0

    lax.fori_loop(0, tg, drain, 0)


def _dispatch(dest3, h2, xs0, tg):
    T, D = h2.shape
    kern = functools.partial(_dispatch_kernel, tg=tg)
    return pl.pallas_call(
        kern,
        grid=(T // tg,),
        in_specs=[pl.BlockSpec((1, 1, tg * TOP_K), lambda i: (i, 0, 0), memory_space=pltpu.SMEM),
                  pl.BlockSpec((tg, D), lambda i: (i, 0)),
                  pl.BlockSpec(memory_space=pl.ANY)],
        out_specs=pl.BlockSpec(memory_space=pl.ANY),
        out_shape=jax.ShapeDtypeStruct(xs0.shape, xs0.dtype),
        scratch_shapes=[pltpu.SemaphoreType.DMA(())],
        input_output_aliases={2: 0},
        compiler_params=_cparams(("arbitrary",)),
        name="dispatch",
    )(dest3, h2, xs0)


PERM_CHUNK = 2 * LANES


def _expert_kernel(be_ref, na_ref, x_ref, w1_ref, bg_ref, bl_ref, w2_ref, b2_ref, perm_ref, y_ref,
                   wg_s, wl_s, w2_s):
    i = pl.program_id(0)
    active = i < na_ref[0]
    prev = be_ref[jnp.maximum(i - 1, 0)]
    changed = active & ((i == 0) | (be_ref[i] != prev))

    @pl.when(changed)
    def _():
        perm = perm_ref[...]
        for j in range(2 * D_FF // PERM_CHUNK):
            chunk = w1_ref[0, :, j * PERM_CHUNK:(j + 1) * PERM_CHUNK].astype(BF16)
            sep = jnp.dot(chunk, perm, preferred_element_type=F32).astype(BF16)
            wg_s[:, j * LANES:(j + 1) * LANES] = sep[:, :LANES]
            wl_s[:, j * LANES:(j + 1) * LANES] = sep[:, LANES:]
        w2_s[...] = w2_ref[0].astype(BF16)

    @pl.when(active)
    def _():
        x = x_ref[...].astype(BF16)
        zg = jnp.dot(x, wg_s[...], preferred_element_type=F32) + bg_ref[0]
        zl = jnp.dot(x, wl_s[...], preferred_element_type=F32) + bl_ref[0]
        xg = jnp.minimum(zg, SWIGLU_LIMIT)
        xl = jnp.clip(zl, -SWIGLU_LIMIT, SWIGLU_LIMIT)
        act = xg * jax.nn.sigmoid(SWIGLU_ALPHA * xg) * (xl + 1.0)
        y_ref[...] = jnp.dot(act.astype(BF16), w2_s[...], preferred_element_type=F32) + b2_ref[0]


def _experts(block_e, n_active, xs, w1, bg, bl, w2, b2, perm):
    n_rows, D = xs.shape
    nblk = n_rows // MOE_ROWS

    def row_map(i, be, na):
        return (jnp.minimum(i, na[0] - 1), 0)

    def w_map(i, be, na):
        return (be[i], 0, 0)

    grid_spec = pltpu.PrefetchScalarGridSpec(
        num_scalar_prefetch=2,
        grid=(nblk,),
        in_specs=[pl.BlockSpec((MOE_ROWS, D), row_map),
                  pl.BlockSpec((1, D, 2 * D_FF), w_map),
                  pl.BlockSpec((1, 1, D_FF), w_map),
                  pl.BlockSpec((1, 1, D_FF), w_map),
                  pl.BlockSpec((1, D_FF, D), w_map),
                  pl.BlockSpec((1, 1, D), w_map),
                  pl.BlockSpec((PERM_CHUNK, PERM_CHUNK), lambda i, be, na: (0, 0))],
        out_specs=pl.BlockSpec((MOE_ROWS, D), row_map),
        scratch_shapes=[pltpu.VMEM((D, D_FF), BF16), pltpu.VMEM((D, D_FF), BF16),
                        pltpu.VMEM((D_FF, D), BF16)],
    )
    return pl.pallas_call(
        _expert_kernel,
        grid_spec=grid_spec,
        out_shape=jax.ShapeDtypeStruct((n_rows, D), F32),
        compiler_params=_cparams(("arbitrary",)),
        name="experts",
    )(block_e, n_active, xs, w1, bg, bl, w2, b2, perm)


def _combine_kernel(dest_ref, gate_ref, x1_ref, mod_ref, ys_hbm, o_ref, buf, sem, *, tc):
    def row_copy(t, j, d):
        return pltpu.make_async_copy(ys_hbm.at[pl.ds(d, 1)], buf.at[j, pl.ds(t, 1)], sem)

    def issue(t, _):
        for j in range(TOP_K):
            row_copy(t, j, dest_ref[0, 0, t * TOP_K + j]).start()
        return 0

    lax.fori_loop(0, tc, issue, 0)

    def drain(t, _):
        for j in range(TOP_K):
            row_copy(t, j, dest_ref[0, 0, t * TOP_K + j]).wait()
        return 0

    lax.fori_loop(0, tc, drain, 0)
    gates = gate_ref[...]
    acc = gates[:, 0:1] * buf[0]
    for j in range(1, TOP_K):
        acc = acc + gates[:, j:j + 1] * buf[j]
    o_ref[...] = x1_ref[...] + mod_ref[0, 5:6, :] * acc


def _combine(dest3, gates, x1, mod3, ys, tc, tiles_per_batch):
    T, D = x1.shape
    kern = functools.partial(_combine_kernel, tc=tc)
    return pl.pallas_call(
        kern,
        grid=(T // tc,),
        in_specs=[pl.BlockSpec((1, 1, tc * TOP_K), lambda i: (i, 0, 0), memory_space=pltpu.SMEM),
                  pl.BlockSpec((tc, TOP_K), lambda i: (i, 0)),
                  pl.BlockSpec((tc, D), lambda i: (i, 0)),
                  pl.BlockSpec((1, 6, D), lambda i: (i // tiles_per_batch, 0, 0)),
                  pl.BlockSpec(memory_space=pl.ANY)],
        out_specs=pl.BlockSpec((tc, D), lambda i: (i, 0)),
        out_shape=jax.ShapeDtypeStruct((T, D), F32),
        scratch_shapes=[pltpu.VMEM((TOP_K, tc, D), F32), pltpu.SemaphoreType.DMA(())],
        compiler_params=_cparams(("arbitrary",)),
        name="combine",
    )(dest3, gates, x1, mod3, ys)


def kernel(x, c, positions, ada_w, ada_b, norm1_w, w_in, q_norm_w, k_norm_w, lambda_q1, lambda_k1,
           lambda_q2, lambda_k2, subln_w, s5_lambda_re, s5_lambda_im, s5_log_step, s5_b_re, s5_b_im,
           s5_cmat_re, s5_cmat_im, s5_d, s5_glu_w, s5_glu_b, s5_out_norm_w, w_out, norm2_w,
           router_w, router_b, mlp1_w, mlp1_b, mlp2_w, mlp2_b):
    B, S, D = x.shape
    T = B * S
    l = 0

    c_pad = jnp.pad(c, ((0, SUBLANES - B % SUBLANES if B % SUBLANES else 0), (0, 0)))
    mod = _ada(c_pad, ada_w[l], ada_b[l][None, :])[:B]
    mod3 = mod.reshape(B, 6, D)

    tm = min(512, S)
    inv_freq = ROPE_THETA ** (-jnp.arange(0, ROT_DIM, 2, dtype=F32) / ROT_DIM)
    d_in_head = jnp.arange(LANES) % QK_DIM
    invf = jnp.where(d_in_head < ROT_DIM, inv_freq[d_in_head % (ROT_DIM // 2)], 0.0)[None, :]
    pos3 = positions.astype(F32)[..., None]
    gmat = jnp.kron(jnp.eye(W_QK // QK_DIM, dtype=F32), jnp.ones((QK_DIM, QK_DIM), F32)).astype(BF16)
    qw = jnp.tile(q_norm_w[l], W_QK // QK_DIM)[None, :]
    kw = jnp.tile(k_norm_w[l], W_QK // QK_DIM)[None, :]
    qt, k, vt, u = _inproj(x, mod3, norm1_w[l][None, :], pos3, invf, w_in[l].astype(BF16),
                          qw, kw, gmat, tm)

    lam_p = jnp.stack([lambda_q1[l], lambda_k1[l], lambda_q2[l], lambda_k2[l]]).astype(F32)
    sw2 = jnp.tile(subln_w[l], 2)[None, :]
    a_out = _attn(lam_p, sw2, qt, k, vt, tq=min(256, S), tk=min(256, S // 2))

    bmat, cmat, tab_r, tab_i = _s5_tables(s5_lambda_re[l], s5_lambda_im[l], s5_log_step[l],
                                          s5_b_re[l], s5_b_im[l], s5_cmat_re[l], s5_cmat_im[l])
    d3 = s5_d[l].astype(F32).reshape(W_S5 // LANES, 1, LANES)
    y_s5 = _s5(u, d3, bmat, cmat, tab_r, tab_i, rows=min(256, S))

    rw = jnp.pad(router_w[l], ((0, 0), (0, LANES - N_EXPERTS)))
    rb = jnp.pad(router_b[l], (0, LANES - N_EXPERTS))[None, :]
    tp = min(256, S)
    x1, h2, eidx, gates, rank, counts = _post(
        a_out, y_s5, x, mod3, s5_glu_w[l].astype(BF16), s5_glu_b[l][None, :],
        s5_out_norm_w[l][None, :], w_out[l].astype(BF16), norm2_w[l][None, :], rw, rb, tp)

    counts = counts[0, :N_EXPERTS].astype(jnp.int32)
    padded = ((counts + MOE_ROWS - 1) // MOE_ROWS) * MOE_ROWS
    pad_end = jnp.cumsum(padded)
    pad_start = pad_end - padded
    eflat = eidx.reshape(T * TOP_K)
    dest = pad_start[eflat] + rank.reshape(T * TOP_K)
    n_rows = T * TOP_K + N_EXPERTS * MOE_ROWS
    nblk = n_rows // MOE_ROWS
    block_e = jnp.minimum(
        jnp.searchsorted(pad_end, jnp.arange(nblk, dtype=jnp.int32) * MOE_ROWS, side='right'),
        N_EXPERTS - 1).astype(jnp.int32)
    n_active = (pad_end[-1] // MOE_ROWS).astype(jnp.int32)[None]

    tg = min(512, T)
    xs0 = jnp.zeros((n_rows, D), F32)
    xs = _dispatch(dest.reshape(T // tg, 1, tg * TOP_K), h2.reshape(T, D), xs0, tg)

    bg = mlp1_b[l][:, None, 0::2]
    bl = mlp1_b[l][:, None, 1::2]
    src = jnp.arange(PERM_CHUNK)
    perm = (jnp.arange(PERM_CHUNK)[None, :] == ((src % 2) * LANES + src // 2)[:, None]).astype(BF16)
    ys = _experts(block_e, n_active, xs, mlp1_w[l], bg, bl, mlp2_w[l], mlp2_b[l][:, None, :], perm)

    tc = min(256, S)
    out = _combine(dest.reshape(T // tc, 1, tc * TOP_K), gates.reshape(T, TOP_K),
                   x1.reshape(T, D), mod3, ys, tc, S // tc)
    return out.reshape(B, S, D)
```

```python
import functools
import math

import jax
import jax.numpy as jnp
from jax import lax
from jax.experimental import pallas as pl
from jax.experimental.pallas import tpu as pltpu

F32 = jnp.float32
BF16 = jnp.bfloat16

D_MODEL = 1024
QK_DIM = 32
V_DIM = 64
N_HEADS = 8
W_QK = N_HEADS * 2 * QK_DIM
W_ATTN = N_HEADS * V_DIM
ROT_DIM = QK_DIM // 4
ROPE_THETA = 500000.0
S5_CH = 16
S5_STATE = 64
W_S5 = 512
S5_GROUPS = W_S5 // S5_CH
D_IN_PROJ = 2 * W_QK + W_ATTN + W_S5
N_EXPERTS = 32
TOP_K = 4
D_FF = D_MODEL
SWIGLU_ALPHA = 1.702
SWIGLU_LIMIT = 7.0
RMS_EPS = 1e-6
LAMBDA_INIT = 0.8 - 0.6 * math.exp(-0.3 * 0)

LANES = 128
SUBLANES = 8
VMEM_LIMIT = 56 * 1024 * 1024

GROUPS_PER_BLOCK = LANES // S5_CH
STATE_LANES = GROUPS_PER_BLOCK * S5_STATE
MOE_ROWS = 512
VT_ROWS = V_DIM + 16


def _cparams(sem):
    return pltpu.CompilerParams(dimension_semantics=sem, vmem_limit_bytes=VMEM_LIMIT)


def _rms(x, eps=RMS_EPS):
    return x * lax.rsqrt(jnp.mean(x * x, axis=-1, keepdims=True) + eps)


def _ada_kernel(c_ref, w_ref, b_ref, o_ref):
    c = c_ref[...]
    ca = c * jax.nn.sigmoid(c)
    o_ref[...] = jnp.dot(ca, w_ref[...], preferred_element_type=F32,
                         precision=lax.Precision.HIGHEST) + b_ref[...]


def _ada(c_pad, w, b):
    rows, d = c_pad.shape
    n = w.shape[1]
    tn = 1536
    return pl.pallas_call(
        _ada_kernel,
        grid=(n // tn,),
        in_specs=[pl.BlockSpec((rows, d), lambda j: (0, 0)),
                  pl.BlockSpec((d, tn), lambda j: (0, j)),
                  pl.BlockSpec((1, tn), lambda j: (0, j))],
        out_specs=pl.BlockSpec((rows, tn), lambda j: (0, j)),
        out_shape=jax.ShapeDtypeStruct((rows, n), F32),
        compiler_params=_cparams(("arbitrary",)),
        name="ada",
    )(c_pad, w, b)


def _inproj_kernel(x_ref, mod_ref, n1w_ref, pos_ref, invf_ref, win_ref, qw_ref, kw_ref, gm_ref,
                   qt_ref, k_ref, vt_ref, u_ref, *, q_scale):
    x = x_ref[0]
    tm = x.shape[0]
    sh1 = mod_ref[0, 0:1, :]
    sc1 = mod_ref[0, 1:2, :]
    h = (_rms(x) * n1w_ref[...]) * (1.0 + sc1) + sh1
    proj = jnp.dot(h.astype(BF16), win_ref[...], preferred_element_type=F32)

    ang = pos_ref[0] * invf_ref[...]
    cos = jnp.cos(ang)
    sin = jnp.sin(ang)
    d_in_head = lax.broadcasted_iota(jnp.int32, (1, LANES), 1) % QK_DIM
    half = ROT_DIM // 2
    s_lo = jnp.where(d_in_head < half, -sin, 0.0)
    s_hi = jnp.where((d_in_head >= half) & (d_in_head < ROT_DIM), sin, 0.0)
    reps = W_QK // LANES
    cos = jnp.concatenate([cos] * reps, axis=1)
    s_lo = jnp.concatenate([s_lo] * reps, axis=1)
    s_hi = jnp.concatenate([s_hi] * reps, axis=1)

    def qk_norm_rope(t, w_ref, scale):
        ssq = jnp.dot((t * t).astype(BF16), gm_ref[...], preferred_element_type=F32)
        tn = t * lax.rsqrt(ssq * (1.0 / QK_DIM) + RMS_EPS) * w_ref[...]
        r = (tn * cos + pltpu.roll(tn, W_QK - half, 1) * s_lo + pltpu.roll(tn, half, 1) * s_hi)
        return r * scale

    q = qk_norm_rope(proj[:, 0:W_QK], qw_ref, q_scale)
    k = qk_norm_rope(proj[:, W_QK:2 * W_QK], kw_ref, 1.0)
    v = proj[:, 2 * W_QK:2 * W_QK + W_ATTN]
    u_ref[0] = proj[:, 2 * W_QK + W_ATTN:]
    qt = q.T
    vt = v.T
    hw = 2 * QK_DIM
    row = lax.broadcasted_iota(jnp.int32, (hw, tm), 0)
    tail = jnp.where(lax.broadcasted_iota(jnp.int32, (VT_ROWS - V_DIM, tm), 0) == 0,
                     1.0, 0.0).astype(BF16)
    for hd in range(N_HEADS):
        qh = qt[hd * hw:(hd + 1) * hw, :]
        qt_ref[0, hd, 0] = jnp.where(row < QK_DIM, qh, 0.0).astype(BF16)
        qt_ref[0, hd, 1] = jnp.where(row >= QK_DIM, qh, 0.0).astype(BF16)
        k_ref[0, hd] = k[:, hd * hw:(hd + 1) * hw].astype(BF16)
        vt_ref[0, hd, 0:V_DIM, :] = vt[hd * V_DIM:(hd + 1) * V_DIM, :].astype(BF16)
        vt_ref[0, hd, V_DIM:, :] = tail


def _inproj(x, mod3, n1w, pos3, invf, win_bf, qw, kw, gmat, tm):
    B, S, D = x.shape
    q_scale = math.log2(math.e) / math.sqrt(QK_DIM)
    kern = functools.partial(_inproj_kernel, q_scale=q_scale)
    const2 = lambda b, i: (0, 0)
    return pl.pallas_call(
        kern,
        grid=(B, S // tm),
        in_specs=[pl.BlockSpec((1, tm, D), lambda b, i: (b, i, 0)),
                  pl.BlockSpec((1, 6, D), lambda b, i: (b, 0, 0)),
                  pl.BlockSpec((1, D), const2),
                  pl.BlockSpec((1, tm, 1), lambda b, i: (b, i, 0)),
                  pl.BlockSpec((1, LANES), const2),
                  pl.BlockSpec((D, D_IN_PROJ), const2),
                  pl.BlockSpec((1, W_QK), const2),
                  pl.BlockSpec((1, W_QK), const2),
                  pl.BlockSpec((W_QK, W_QK), const2)],
        out_specs=[pl.BlockSpec((1, N_HEADS, 2, 2 * QK_DIM, tm), lambda b, i: (b, 0, 0, 0, i)),
                   pl.BlockSpec((1, N_HEADS, tm, 2 * QK_DIM), lambda b, i: (b, 0, i, 0)),
                   pl.BlockSpec((1, N_HEADS, VT_ROWS, tm), lambda b, i: (b, 0, 0, i)),
                   pl.BlockSpec((1, tm, W_S5), lambda b, i: (b, i, 0))],
        out_shape=[jax.ShapeDtypeStruct((B, N_HEADS, 2, 2 * QK_DIM, S), BF16),
                   jax.ShapeDtypeStruct((B, N_HEADS, S, 2 * QK_DIM), BF16),
                   jax.ShapeDtypeStruct((B, N_HEADS, VT_ROWS, S), BF16),
                   jax.ShapeDtypeStruct((B, S, W_S5), F32)],
        compiler_params=_cparams(("arbitrary", "arbitrary")),
        name="inproj",
    )(x, mod3, n1w, pos3, invf, win_bf, qw, kw, gmat)


def _attn_kernel(lam_ref, sw_ref, qt_ref, k_ref, vt_ref, o_ref, st_a, st_b, pt_a, pt_b, acc_s, *,
                 tk, heads_per_step):
    S = k_ref.shape[2]
    tq = qt_ref.shape[4]
    lp = lam_ref[...]
    lam = (jnp.exp(jnp.sum(lp[0:1] * lp[1:2], axis=-1, keepdims=True))
           - jnp.exp(jnp.sum(lp[2:3] * lp[3:4], axis=-1, keepdims=True)) + LAMBDA_INIT)
    chains = [(hh, c) for hh in range(heads_per_step) for c in range(2)]
    qts = [qt_ref[0, hh, c] for hh, c in chains]

    def scores(kb, st_buf):
        off = pl.multiple_of(kb * tk, tk)
        for ci, ((hh, c), qt) in enumerate(zip(chains, qts)):
            st_buf[ci] = jnp.dot(k_ref[0, hh, pl.ds(off, tk), :], qt,
                                 preferred_element_type=F32).astype(BF16)

    def softmax(st_buf, pt_buf, ms):
        m_new, alphas = [], []
        for ci, m in enumerate(ms):
            mn = jnp.maximum(m, jnp.max(st_buf[ci], axis=0, keepdims=True).astype(F32))
            alphas.append(jnp.exp2(m - mn))
            pt_buf[ci] = jnp.exp2(st_buf[ci] - mn.astype(BF16))
            m_new.append(mn)
        return tuple(m_new), tuple(alphas)

    def accumulate(kb, pt_buf, alphas):
        off = pl.multiple_of(kb * tk, tk)
        for ci, ((hh, c), alpha) in enumerate(zip(chains, alphas)):
            pv = jnp.dot(vt_ref[0, hh, :, pl.ds(off, tk)], pt_buf[ci],
                         preferred_element_type=F32)
            acc_s[ci] = alpha * acc_s[ci] + pv

    def step(kb, st_cur, pt_cur, st_nxt, pt_prev, ms, alphas_prev):
        accumulate(kb - 1, pt_prev, alphas_prev)
        scores(kb + 1, st_nxt)
        return softmax(st_cur, pt_cur, ms)

    def body(jj, carry):
        ms, alphas = carry
        kb = 2 * jj + 1
        ms, alphas = step(kb, st_b, pt_b, st_a, pt_a, ms, alphas)
        return step(kb + 1, st_a, pt_a, st_b, pt_b, ms, alphas)

    nkb = S // tk
    assert nkb % 2 == 0
    acc_s[...] = jnp.zeros_like(acc_s)
    m0 = tuple(jnp.full((1, tq), -jnp.inf, F32) for _ in chains)
    scores(0, st_a)
    scores(1, st_b)
    ms, alphas = softmax(st_a, pt_a, m0)
    ms, alphas = lax.fori_loop(0, nkb // 2 - 1, body, (ms, alphas))
    accumulate(nkb - 2, pt_a, alphas)
    ms, alphas = softmax(st_b, pt_b, ms)
    accumulate(nkb - 1, pt_b, alphas)
    res = tuple((None, acc_s[ci]) for ci in range(len(chains)))
    outs = []
    for hh in range(heads_per_step):
        a0 = res[2 * hh][1]
        a1 = res[2 * hh + 1][1]
        ot = (a0[:V_DIM, :] / a0[V_DIM:V_DIM + 1, :]
              - lam * (a1[:V_DIM, :] / a1[V_DIM:V_DIM + 1, :]))
        ot = ot * lax.rsqrt(jnp.mean(ot * ot, axis=0, keepdims=True) + RMS_EPS)
        outs.append(ot)
    o = jnp.concatenate(outs, axis=0).T
    o_ref[0] = (o * sw_ref[...] * (1.0 - LAMBDA_INIT)).astype(o_ref.dtype)


def _attn(lam_p, sw2, qt, k, vt, tq, tk):
    B, H, S, _ = k.shape
    hps = 2
    kern = functools.partial(_attn_kernel, tk=tk, heads_per_step=hps)
    return pl.pallas_call(
        kern,
        grid=(B, H // hps, S // tq),
        in_specs=[pl.BlockSpec((4, QK_DIM), lambda b, h, i: (0, 0)),
                  pl.BlockSpec((1, hps * V_DIM), lambda b, h, i: (0, 0)),
                  pl.BlockSpec((1, hps, 2, 2 * QK_DIM, tq), lambda b, h, i: (b, h, 0, 0, i)),
                  pl.BlockSpec((1, hps, S, 2 * QK_DIM), lambda b, h, i: (b, h, 0, 0)),
                  pl.BlockSpec((1, hps, VT_ROWS, S), lambda b, h, i: (b, h, 0, 0))],
        out_specs=pl.BlockSpec((1, tq, hps * V_DIM), lambda b, h, i: (b, i, h)),
        out_shape=jax.ShapeDtypeStruct((B, S, W_ATTN), BF16),
        scratch_shapes=[pltpu.VMEM((2 * hps, tk, tq), BF16), pltpu.VMEM((2 * hps, tk, tq), BF16),
                        pltpu.VMEM((2 * hps, tk, tq), BF16), pltpu.VMEM((2 * hps, tk, tq), BF16),
                        pltpu.VMEM((2 * hps, VT_ROWS, tq), F32)],
        compiler_params=_cparams(("arbitrary", "arbitrary", "arbitrary")),
        name="attn",
    )(lam_p, sw2, qt, k, vt)


def _s5_kernel(u_ref, d_ref, bm_ref, cm_ref, tr_ref, ti_ref, y_ref, xr_s, xi_s, *, rows):
    S = u_ref.shape[1]
    R = rows
    ng = R // SUBLANES
    nchunk = S // R
    P = STATE_LANES
    for di in range(2):
        bm = bm_ref[di, 0]
        cm = cm_ref[di, 0]
        tab_r = [tr_ref[di, 0, t] for t in range(4)]
        tab_i = [ti_ref[di, 0, t] for t in range(4)]

        def chunk_body(ci, carry, di=di, bm=bm, cm=cm, tab_r=tab_r, tab_i=tab_i):
            c = ci if di == 0 else nchunk - 1 - ci
            r0 = pl.multiple_of(c * R, R)
            u = u_ref[0, pl.ds(r0, R), :]
            bu = jnp.dot(u.astype(BF16), bm, preferred_element_type=F32)
            xr = bu[:, :P].reshape(ng, SUBLANES, P)
            xi = bu[:, P:].reshape(ng, SUBLANES, P)
            for t, k in enumerate((1, 2, 4)):
                sh = k if di == 0 else SUBLANES - k
                sr = pltpu.roll(xr, sh, 1)
                si = pltpu.roll(xi, sh, 1)
                ar = tab_r[t][None]
                ai = tab_i[t][None]
                xr, xi = xr + ar * sr - ai * si, xi + ar * si + ai * sr
            xr_s[...] = xr.reshape(R, P)
            xi_s[...] = xi.reshape(R, P)
            a8r = tab_r[3]
            a8i = tab_i[3]
            edge = SUBLANES - 1 if di == 0 else 0

            def row_body(jj, cc):
                cr, cim = cc
                j = jj if di == 0 else ng - 1 - jj
                o = pl.multiple_of(j * SUBLANES, SUBLANES)
                vr = xr_s[pl.ds(o, SUBLANES), :]
                vi = xi_s[pl.ds(o, SUBLANES), :]
                nr = vr + a8r * cr - a8i * cim
                ni = vi + a8r * cim + a8i * cr
                xr_s[pl.ds(o, SUBLANES), :] = nr
                xi_s[pl.ds(o, SUBLANES), :] = ni
                return nr[edge:edge + 1, :], ni[edge:edge + 1, :]

            carry = lax.fori_loop(0, ng, row_body, carry, unroll=4)
            yc = (jnp.dot(xr_s[...].astype(BF16), cm[:P], preferred_element_type=F32)
                  + jnp.dot(xi_s[...].astype(BF16), cm[P:], preferred_element_type=F32))
            if di == 0:
                y_ref[0, pl.ds(r0, R), :] = u * d_ref[0] + yc
            else:
                y_ref[0, pl.ds(r0, R), :] = y_ref[0, pl.ds(r0, R), :] + yc
            return carry

        zero = jnp.zeros((1, P), F32)
        lax.fori_loop(0, nchunk, chunk_body, (zero, zero))


def _s5(u, d3, bmat, cmat, tab_r, tab_i, rows):
    B, S, W = u.shape
    nb = W // LANES
    P = STATE_LANES
    kern = functools.partial(_s5_kernel, rows=rows)
    return pl.pallas_call(
        kern,
        grid=(B, nb),
        in_specs=[pl.BlockSpec((1, S, LANES), lambda b, g: (b, 0, g)),
                  pl.BlockSpec((1, 1, LANES), lambda b, g: (g, 0, 0)),
                  pl.BlockSpec((2, 1, LANES, 2 * P), lambda b, g: (0, g, 0, 0)),
                  pl.BlockSpec((2, 1, 2 * P, LANES), lambda b, g: (0, g, 0, 0)),
                  pl.BlockSpec((2, 1, 4, SUBLANES, P), lambda b, g: (0, g, 0, 0, 0)),
                  pl.BlockSpec((2, 1, 4, SUBLANES, P), lambda b, g: (0, g, 0, 0, 0))],
        out_specs=pl.BlockSpec((1, S, LANES), lambda b, g: (b, 0, g)),
        out_shape=jax.ShapeDtypeStruct((B, S, W), F32),
        scratch_shapes=[pltpu.VMEM((rows, P), F32), pltpu.VMEM((rows, P), F32)],
        compiler_params=_cparams(("arbitrary", "arbitrary")),
        name="s5",
    )(u, d3, bmat, cmat, tab_r, tab_i)


def _s5_tables(lam_re, lam_im, log_step, b_re, b_im, cm_re, cm_im):
    nb = S5_GROUPS // GROUPS_PER_BLOCK
    eye = jnp.eye(GROUPS_PER_BLOCK, dtype=F32)
    bmats, cmats, tabs_r, tabs_i = [], [], [], []
    for di in range(2):
        lr = lam_re[di].astype(F32)
        li = lam_im[di].astype(F32)
        delta = jnp.exp(log_step[di].astype(F32))[:, None]
        mag = jnp.exp(lr * delta)
        a_re = mag * jnp.cos(li * delta)
        a_im = mag * jnp.sin(li * delta)
        den = lr * lr + li * li
        num_re = a_re - 1.0
        f_re = (num_re * lr + a_im * li) / den
        f_im = (a_im * lr - num_re * li) / den
        br = b_re[di].astype(F32)
        bi = b_im[di].astype(F32)
        bbar_re = f_re[..., None] * br - f_im[..., None] * bi
        bbar_im = f_re[..., None] * bi + f_im[..., None] * br

        def blockdiag_in(bb):
            bb = bb.reshape(nb, GROUPS_PER_BLOCK, S5_STATE, S5_CH)
            m = jnp.einsum('bgpc,gh->bgchp', bb, eye)
            return m.reshape(nb, LANES, STATE_LANES)

        def blockdiag_out(cc):
            cc = cc.reshape(nb, GROUPS_PER_BLOCK, S5_CH, S5_STATE)
            m = jnp.einsum('bgcp,gh->bgphc', cc, eye)
            return m.reshape(nb, STATE_LANES, LANES)

        bmats.append(jnp.concatenate([blockdiag_in(bbar_re), blockdiag_in(bbar_im)], axis=2))
        cmats.append(jnp.concatenate([blockdiag_out(cm_re[di].astype(F32)),
                                      -blockdiag_out(cm_im[di].astype(F32))], axis=1))
        pr, pi = [a_re], [a_im]
        for _ in range(SUBLANES - 1):
            pr, pi = (pr + [pr[-1] * a_re - pi[-1] * a_im],
                      pi + [pr[-1] * a_im + pi[-1] * a_re])
        pr = jnp.stack(pr).reshape(SUBLANES, nb, STATE_LANES)
        pi = jnp.stack(pi).reshape(SUBLANES, nb, STATE_LANES)
        row = jnp.arange(SUBLANES)[:, None, None]
        tr, ti = [], []
        for k in (1, 2, 4):
            keep = (row >= k) if di == 0 else (row < SUBLANES - k)
            tr.append(jnp.where(keep, pr[k - 1][None], 0.0))
            ti.append(jnp.where(keep, pi[k - 1][None], 0.0))
        if di == 0:
            tr.append(pr)
            ti.append(pi)
        else:
            tr.append(pr[::-1])
            ti.append(pi[::-1])
        tabs_r.append(jnp.transpose(jnp.stack(tr), (2, 0, 1, 3)))
        tabs_i.append(jnp.transpose(jnp.stack(ti), (2, 0, 1, 3)))
    return (jnp.stack(bmats).astype(BF16), jnp.stack(cmats).astype(BF16),
            jnp.stack(tabs_r), jnp.stack(tabs_i))


def _post_kernel(a_ref, ys_ref, x_ref, mod_ref, gluw_ref, glub_ref, onw_ref, wout_ref, n2w_ref,
                 rw_ref, rb_ref, x1_ref, h2_ref, eidx_ref, gate_ref, rank_ref, cnt_ref, carry_s):
    first = (pl.program_id(0) == 0) & (pl.program_id(1) == 0)

    @pl.when(first)
    def _():
        carry_s[...] = jnp.zeros_like(carry_s)

    tm = x_ref.shape[1]
    y = ys_ref[0]
    y = 0.5 * y * (1.0 + jnp.tanh(math.sqrt(2.0 / math.pi) * (y + 0.044715 * (y * y * y))))
    g = jnp.dot(y.astype(BF16), gluw_ref[...], preferred_element_type=F32) + glub_ref[...]
    y = y * jax.nn.sigmoid(g)
    s = _rms(y) * onw_ref[...]
    mix = (jnp.dot(a_ref[0], wout_ref[0:W_ATTN, :], preferred_element_type=F32)
           + jnp.dot(s.astype(BF16), wout_ref[W_ATTN:, :], preferred_element_type=F32))
    g1 = mod_ref[0, 2:3, :]
    sh2 = mod_ref[0, 3:4, :]
    sc2 = mod_ref[0, 4:5, :]
    x1 = x_ref[0] + g1 * mix
    x1_ref[0] = x1
    h2 = (_rms(x1) * n2w_ref[...]) * (1.0 + sc2) + sh2
    h2_ref[0] = h2
    logits = jnp.dot(h2, rw_ref[...], preferred_element_type=F32,
                     precision=lax.Precision.HIGHEST) + rb_ref[...]
    lane = lax.broadcasted_iota(jnp.int32, (tm, LANES), 1)
    neg = jnp.float32(-jnp.inf)
    work = jnp.where(lane < N_EXPERTS, logits, neg)
    vals, idxs = [], []
    for _ in range(TOP_K):
        m = jnp.max(work, axis=-1, keepdims=True)
        idx = jnp.min(jnp.where(work == m, lane, LANES), axis=-1, keepdims=True)
        vals.append(m)
        idxs.append(idx)
        work = jnp.where(lane == idx, neg, work)
    es = [jnp.exp(v - vals[0]) for v in vals]
    den = es[0] + es[1] + es[2] + es[3]
    onehot = jnp.zeros((tm, LANES), F32)
    for idx in idxs:
        onehot = onehot + jnp.where(lane == idx, 1.0, 0.0)
    r_i = lax.broadcasted_iota(jnp.int32, (tm, tm), 0)
    c_i = lax.broadcasted_iota(jnp.int32, (tm, tm), 1)
    ltri = jnp.where(c_i < r_i, 1.0, 0.0).astype(BF16)
    before = jnp.dot(ltri, onehot.astype(BF16), preferred_element_type=F32) + carry_s[...]
    lane4 = lax.broadcasted_iota(jnp.int32, (tm, TOP_K), 1)
    e_out = jnp.zeros((tm, TOP_K), jnp.int32)
    g_out = jnp.zeros((tm, TOP_K), F32)
    r_out = jnp.zeros((tm, TOP_K), jnp.int32)
    for j in range(TOP_K):
        rk = jnp.sum(jnp.where(lane == idxs[j], before, 0.0), axis=-1, keepdims=True)
        e_out = jnp.where(lane4 == j, idxs[j], e_out)
        g_out = jnp.where(lane4 == j, es[j] / den, g_out)
        r_out = jnp.where(lane4 == j, rk.astype(jnp.int32), r_out)
    eidx_ref[0] = e_out
    gate_ref[0] = g_out
    rank_ref[0] = r_out
    carry_s[...] = carry_s[...] + jnp.sum(onehot, axis=0, keepdims=True)
    cnt_ref[...] = carry_s[...]


def _post(a, ys, x, mod3, gluw, glub, onw, wout, n2w, rw, rb, tm):
    B, S, D = x.shape
    c2 = lambda b, i: (0, 0)
    tok = lambda b, i: (b, i, 0)
    return pl.pallas_call(
        _post_kernel,
        grid=(B, S // tm),
        in_specs=[pl.BlockSpec((1, tm, W_ATTN), tok),
                  pl.BlockSpec((1, tm, W_S5), tok),
                  pl.BlockSpec((1, tm, D), tok),
                  pl.BlockSpec((1, 6, D), lambda b, i: (b, 0, 0)),
                  pl.BlockSpec((W_S5, W_S5), c2),
                  pl.BlockSpec((1, W_S5), c2),
                  pl.BlockSpec((1, W_S5), c2),
                  pl.BlockSpec((D, D), c2),
                  pl.BlockSpec((1, D), c2),
                  pl.BlockSpec((D, LANES), c2),
                  pl.BlockSpec((1, LANES), c2)],
        out_specs=[pl.BlockSpec((1, tm, D), tok),
                   pl.BlockSpec((1, tm, D), tok),
                   pl.BlockSpec((1, tm, TOP_K), tok),
                   pl.BlockSpec((1, tm, TOP_K), tok),
                   pl.BlockSpec((1, tm, TOP_K), tok),
                   pl.BlockSpec((1, LANES), c2)],
        out_shape=[jax.ShapeDtypeStruct((B, S, D), F32),
                   jax.ShapeDtypeStruct((B, S, D), F32),
                   jax.ShapeDtypeStruct((B, S, TOP_K), jnp.int32),
                   jax.ShapeDtypeStruct((B, S, TOP_K), F32),
                   jax.ShapeDtypeStruct((B, S, TOP_K), jnp.int32),
                   jax.ShapeDtypeStruct((1, LANES), F32)],
        scratch_shapes=[pltpu.VMEM((1, LANES), F32)],
        compiler_params=_cparams(("arbitrary", "arbitrary")),
        name="post",
    )(a, ys, x, mod3, gluw, glub, onw, wout, n2w, rw, rb)


def _dispatch_kernel(pend_ref, padded_ref, dest_ref, h2_ref, xs_hbm, zero_s, sem, zsem, *, tg):
    @pl.when(pl.program_id(0) == 0)
    def _():
        zero_s[...] = jnp.zeros_like(zero_s)

        def zero_copy(e):
            start = pl.multiple_of(pend_ref[e] - MOE_ROWS, MOE_ROWS)
            return pltpu.make_async_copy(zero_s, xs_hbm.at[pl.ds(start, MOE_ROWS)], zsem)

        for e in range(N_EXPERTS):
            @pl.when(padded_ref[e] > 0)
            def _():
                zero_copy(e).start()
        for e in range(N_EXPERTS):
            @pl.when(padded_ref[e] > 0)
            def _():
                zero_copy(e).wait()

    def row_copy(t, d):
        return pltpu.make_async_copy(h2_ref.at[pl.ds(t, 1)], xs_hbm.at[pl.ds(d, 1)], sem)

    def issue(t, _):
        for j in range(TOP_K):
            row_copy(t, dest_ref[0, 0, t * TOP_K + j]).start()
        return 0

    lax.fori_loop(0, tg, issue, 0, unroll=2)
    for _ in range(TOP_K):
        pltpu.make_async_copy(h2_ref, xs_hbm.at[pl.ds(0, tg)], sem).wait()


def _dispatch(pad_end, padded, dest3, h2, n_rows, tg):
    T, D = h2.shape
    kern = functools.partial(_dispatch_kernel, tg=tg)
    grid_spec = pltpu.PrefetchScalarGridSpec(
        num_scalar_prefetch=2,
        grid=(T // tg,),
        in_specs=[pl.BlockSpec((1, 1, tg * TOP_K), lambda i, pe, pd: (i, 0, 0),
                               memory_space=pltpu.SMEM),
                  pl.BlockSpec((tg, D), lambda i, pe, pd: (i, 0))],
        out_specs=pl.BlockSpec(memory_space=pl.ANY),
        scratch_shapes=[pltpu.VMEM((MOE_ROWS, D), F32), pltpu.SemaphoreType.DMA(()),
                        pltpu.SemaphoreType.DMA(())],
    )
    return pl.pallas_call(
        kern,
        grid_spec=grid_spec,
        out_shape=jax.ShapeDtypeStruct((n_rows, D), F32),
        compiler_params=_cparams(("arbitrary",)),
        name="dispatch",
    )(pad_end, padded, dest3, h2)


PERM_CHUNK = 2 * LANES


def _expert_kernel(be_ref, na_ref, x_ref, w1_ref, bg_ref, bl_ref, w2_ref, b2_ref, perm_ref, y_ref,
                   wg_s, wl_s, w2_s):
    i = pl.program_id(0)
    active = i < na_ref[0]
    prev = be_ref[jnp.maximum(i - 1, 0)]
    changed = active & ((i == 0) | (be_ref[i] != prev))

    @pl.when(changed)
    def _():
        perm = perm_ref[...]
        for j in range(2 * D_FF // PERM_CHUNK):
            chunk = w1_ref[0, :, j * PERM_CHUNK:(j + 1) * PERM_CHUNK].astype(BF16)
            sep = jnp.dot(chunk, perm, preferred_element_type=F32).astype(BF16)
            wg_s[:, j * LANES:(j + 1) * LANES] = sep[:, :LANES]
            wl_s[:, j * LANES:(j + 1) * LANES] = sep[:, LANES:]
        w2_s[...] = w2_ref[0].astype(BF16)

    @pl.when(active)
    def _():
        x = x_ref[...].astype(BF16)
        zg = jnp.dot(x, wg_s[...], preferred_element_type=F32) + bg_ref[0]
        zl = jnp.dot(x, wl_s[...], preferred_element_type=F32) + bl_ref[0]
        xg = jnp.minimum(zg, SWIGLU_LIMIT)
        xl = jnp.clip(zl, -SWIGLU_LIMIT, SWIGLU_LIMIT)
        act = xg * jax.nn.sigmoid(SWIGLU_ALPHA * xg) * (xl + 1.0)
        y_ref[...] = jnp.dot(act.astype(BF16), w2_s[...], preferred_element_type=F32) + b2_ref[0]


def _experts(block_e, n_active, xs, w1, bg, bl, w2, b2, perm):
    n_rows, D = xs.shape
    nblk = n_rows // MOE_ROWS

    def row_map(i, be, na):
        return (jnp.minimum(i, na[0] - 1), 0)

    def w_map(i, be, na):
        return (be[i], 0, 0)

    grid_spec = pltpu.PrefetchScalarGridSpec(
        num_scalar_prefetch=2,
        grid=(nblk,),
        in_specs=[pl.BlockSpec((MOE_ROWS, D), row_map),
                  pl.BlockSpec((1, D, 2 * D_FF), w_map),
                  pl.BlockSpec((1, 1, D_FF), w_map),
                  pl.BlockSpec((1, 1, D_FF), w_map),
                  pl.BlockSpec((1, D_FF, D), w_map),
                  pl.BlockSpec((1, 1, D), w_map),
                  pl.BlockSpec((PERM_CHUNK, PERM_CHUNK), lambda i, be, na: (0, 0))],
        out_specs=pl.BlockSpec((MOE_ROWS, D), row_map),
        scratch_shapes=[pltpu.VMEM((D, D_FF), BF16), pltpu.VMEM((D, D_FF), BF16),
                        pltpu.VMEM((D_FF, D), BF16)],
    )
    return pl.pallas_call(
        _expert_kernel,
        grid_spec=grid_spec,
        out_shape=jax.ShapeDtypeStruct((n_rows, D), F32),
        compiler_params=_cparams(("arbitrary",)),
        name="experts",
    )(block_e, n_active, xs, w1, bg, bl, w2, b2, perm)


def _combine_kernel(dest_ref, gate_ref, x1_ref, mod_ref, ys_hbm, o_ref, buf, sem, *, tc):
    def row_copy(t, j, d):
        return pltpu.make_async_copy(ys_hbm.at[pl.ds(d, 1)], buf.at[j, pl.ds(t, 1)], sem)

    def issue(t, _):
        for j in range(TOP_K):
            row_copy(t, j, dest_ref[0, 0, t * TOP_K + j]).start()
        return 0

    lax.fori_loop(0, tc, issue, 0, unroll=2)
    for j in range(TOP_K):
        pltpu.make_async_copy(ys_hbm.at[pl.ds(0, tc)], buf.at[j], sem).wait()
    gates = gate_ref[...]
    acc = gates[:, 0:1] * buf[0]
    for j in range(1, TOP_K):
        acc = acc + gates[:, j:j + 1] * buf[j]
    o_ref[...] = x1_ref[...] + mod_ref[0, 5:6, :] * acc


def _combine(dest3, gates, x1, mod3, ys, tc, tiles_per_batch):
    T, D = x1.shape
    kern = functools.partial(_combine_kernel, tc=tc)
    return pl.pallas_call(
        kern,
        grid=(T // tc,),
        in_specs=[pl.BlockSpec((1, 1, tc * TOP_K), lambda i: (i, 0, 0), memory_space=pltpu.SMEM),
                  pl.BlockSpec((tc, TOP_K), lambda i: (i, 0)),
                  pl.BlockSpec((tc, D), lambda i: (i, 0)),
                  pl.BlockSpec((1, 6, D), lambda i: (i // tiles_per_batch, 0, 0)),
                  pl.BlockSpec(memory_space=pl.ANY)],
        out_specs=pl.BlockSpec((tc, D), lambda i: (i, 0)),
        out_shape=jax.ShapeDtypeStruct((T, D), F32),
        scratch_shapes=[pltpu.VMEM((TOP_K, tc, D), F32), pltpu.SemaphoreType.DMA(())],
        compiler_params=_cparams(("arbitrary",)),
        name="combine",
    )(dest3, gates, x1, mod3, ys)


def kernel(x, c, positions, ada_w, ada_b, norm1_w, w_in, q_norm_w, k_norm_w, lambda_q1, lambda_k1,
           lambda_q2, lambda_k2, subln_w, s5_lambda_re, s5_lambda_im, s5_log_step, s5_b_re, s5_b_im,
           s5_cmat_re, s5_cmat_im, s5_d, s5_glu_w, s5_glu_b, s5_out_norm_w, w_out, norm2_w,
           router_w, router_b, mlp1_w, mlp1_b, mlp2_w, mlp2_b):
    B, S, D = x.shape
    T = B * S
    l = 0

    c_pad = jnp.pad(c, ((0, SUBLANES - B % SUBLANES if B % SUBLANES else 0), (0, 0)))
    mod = _ada(c_pad, ada_w[l], ada_b[l][None, :])[:B]
    mod3 = mod.reshape(B, 6, D)

    tm = min(512, S)
    inv_freq = ROPE_THETA ** (-jnp.arange(0, ROT_DIM, 2, dtype=F32) / ROT_DIM)
    d_in_head = jnp.arange(LANES) % QK_DIM
    invf = jnp.where(d_in_head < ROT_DIM, inv_freq[d_in_head % (ROT_DIM // 2)], 0.0)[None, :]
    pos3 = positions.astype(F32)[..., None]
    gmat = jnp.kron(jnp.eye(W_QK // QK_DIM, dtype=F32), jnp.ones((QK_DIM, QK_DIM), F32)).astype(BF16)
    qw = jnp.tile(q_norm_w[l], W_QK // QK_DIM)[None, :]
    kw = jnp.tile(k_norm_w[l], W_QK // QK_DIM)[None, :]
    qt, k, vt, u = _inproj(x, mod3, norm1_w[l][None, :], pos3, invf, w_in[l].astype(BF16),
                          qw, kw, gmat, tm)

    lam_p = jnp.stack([lambda_q1[l], lambda_k1[l], lambda_q2[l], lambda_k2[l]]).astype(F32)
    sw2 = jnp.tile(subln_w[l], 2)[None, :]
    a_out = _attn(lam_p, sw2, qt, k, vt, tq=min(512, S), tk=min(256, S // 2))

    bmat, cmat, tab_r, tab_i = _s5_tables(s5_lambda_re[l], s5_lambda_im[l], s5_log_step[l],
                                          s5_b_re[l], s5_b_im[l], s5_cmat_re[l], s5_cmat_im[l])
    d3 = s5_d[l].astype(F32).reshape(W_S5 // LANES, 1, LANES)
    y_s5 = _s5(u, d3, bmat, cmat, tab_r, tab_i, rows=min(256, S))

    rw = jnp.pad(router_w[l], ((0, 0), (0, LANES - N_EXPERTS)))
    rb = jnp.pad(router_b[l], (0, LANES - N_EXPERTS))[None, :]
    tp = min(256, S)
    x1, h2, eidx, gates, rank, counts = _post(
        a_out, y_s5, x, mod3, s5_glu_w[l].astype(BF16), s5_glu_b[l][None, :],
        s5_out_norm_w[l][None, :], w_out[l].astype(BF16), norm2_w[l][None, :], rw, rb, tp)

    counts = counts[0, :N_EXPERTS].astype(jnp.int32)
    padded = ((counts + MOE_ROWS - 1) // MOE_ROWS) * MOE_ROWS
    pad_end = jnp.cumsum(padded)
    pad_start = pad_end - padded
    eflat = eidx.reshape(T * TOP_K)
    dest = pad_start[eflat] + rank.reshape(T * TOP_K)
    n_rows = T * TOP_K + N_EXPERTS * MOE_ROWS
    nblk = n_rows // MOE_ROWS
    blk_row = jnp.arange(nblk, dtype=jnp.int32)[:, None] * MOE_ROWS
    block_e = jnp.minimum(jnp.sum((blk_row >= pad_end[None, :]).astype(jnp.int32), axis=1),
                          N_EXPERTS - 1)
    n_active = (pad_end[-1] // MOE_ROWS).astype(jnp.int32)[None]

    tg = min(512, T)
    xs = _dispatch(pad_end.astype(jnp.int32), padded, dest.reshape(T // tg, 1, tg * TOP_K),
                   h2.reshape(T, D), n_rows, tg)

    bg = mlp1_b[l][:, None, 0::2]
    bl = mlp1_b[l][:, None, 1::2]
    src = jnp.arange(PERM_CHUNK)
    perm = (jnp.arange(PERM_CHUNK)[None, :] == ((src % 2) * LANES + src // 2)[:, None]).astype(BF16)
    ys = _experts(block_e, n_active, xs, mlp1_w[l], bg, bl, mlp2_w[l], mlp2_b[l][:, None, :], perm)

    tc = min(256, S)
    out = _combine(dest.reshape(T // tc, 1, tc * TOP_K), gates.reshape(T, TOP_K),
                   x1.reshape(T, D), mod3, ys, tc, S // tc)
    return out.reshape(B, S, D)
```

```python
import functools
import math

import jax
import jax.numpy as jnp
from jax import lax
from jax.experimental import pallas as pl
from jax.experimental.pallas import tpu as pltpu

F32 = jnp.float32
BF16 = jnp.bfloat16

D_MODEL = 1024
QK_DIM = 32
V_DIM = 64
N_HEADS = 8
W_QK = N_HEADS * 2 * QK_DIM
W_ATTN = N_HEADS * V_DIM
ROT_DIM = QK_DIM // 4
ROPE_THETA = 500000.0
S5_CH = 16
S5_STATE = 64
W_S5 = 512
S5_GROUPS = W_S5 // S5_CH
D_IN_PROJ = 2 * W_QK + W_ATTN + W_S5
N_EXPERTS = 32
TOP_K = 4
D_FF = D_MODEL
SWIGLU_ALPHA = 1.702
SWIGLU_LIMIT = 7.0
RMS_EPS = 1e-6
LAMBDA_INIT = 0.8 - 0.6 * math.exp(-0.3 * 0)

LANES = 128
SUBLANES = 8
VMEM_LIMIT = 56 * 1024 * 1024

GROUPS_PER_BLOCK = LANES // S5_CH
STATE_LANES = GROUPS_PER_BLOCK * S5_STATE
MOE_ROWS = 512
VT_ROWS = V_DIM + 16


def _cparams(sem):
    return pltpu.CompilerParams(dimension_semantics=sem, vmem_limit_bytes=VMEM_LIMIT)


def _rms(x, eps=RMS_EPS):
    return x * lax.rsqrt(jnp.mean(x * x, axis=-1, keepdims=True) + eps)


def _ada_kernel(c_ref, w_ref, b_ref, o_ref):
    c = c_ref[...]
    ca = c * jax.nn.sigmoid(c)
    o_ref[...] = jnp.dot(ca, w_ref[...], preferred_element_type=F32,
                         precision=lax.Precision.HIGHEST) + b_ref[...]


def _ada(c_pad, w, b):
    rows, d = c_pad.shape
    n = w.shape[1]
    tn = 1536
    return pl.pallas_call(
        _ada_kernel,
        grid=(n // tn,),
        in_specs=[pl.BlockSpec((rows, d), lambda j: (0, 0)),
                  pl.BlockSpec((d, tn), lambda j: (0, j)),
                  pl.BlockSpec((1, tn), lambda j: (0, j))],
        out_specs=pl.BlockSpec((rows, tn), lambda j: (0, j)),
        out_shape=jax.ShapeDtypeStruct((rows, n), F32),
        compiler_params=_cparams(("arbitrary",)),
        name="ada",
    )(c_pad, w, b)


def _inproj_kernel(x_ref, mod_ref, n1w_ref, pos_ref, invf_ref, win_ref, qw_ref, kw_ref, gm_ref,
                   qt_ref, k_ref, vt_ref, u_ref, *, q_scale):
    x = x_ref[0]
    tm = x.shape[0]
    sh1 = mod_ref[0, 0:1, :]
    sc1 = mod_ref[0, 1:2, :]
    h = (_rms(x) * n1w_ref[...]) * (1.0 + sc1) + sh1
    proj = jnp.dot(h.astype(BF16), win_ref[...], preferred_element_type=F32)

    ang = pos_ref[0] * invf_ref[...]
    cos = jnp.cos(ang)
    sin = jnp.sin(ang)
    d_in_head = lax.broadcasted_iota(jnp.int32, (1, LANES), 1) % QK_DIM
    half = ROT_DIM // 2
    s_lo = jnp.where(d_in_head < half, -sin, 0.0)
    s_hi = jnp.where((d_in_head >= half) & (d_in_head < ROT_DIM), sin, 0.0)
    reps = W_QK // LANES
    cos = jnp.concatenate([cos] * reps, axis=1)
    s_lo = jnp.concatenate([s_lo] * reps, axis=1)
    s_hi = jnp.concatenate([s_hi] * reps, axis=1)

    def qk_norm_rope(t, w_ref, scale):
        ssq = jnp.dot((t * t).astype(BF16), gm_ref[...], preferred_element_type=F32)
        tn = t * lax.rsqrt(ssq * (1.0 / QK_DIM) + RMS_EPS) * w_ref[...]
        r = (tn * cos + pltpu.roll(tn, W_QK - half, 1) * s_lo + pltpu.roll(tn, half, 1) * s_hi)
        return r * scale

    q = qk_norm_rope(proj[:, 0:W_QK], qw_ref, q_scale)
    k = qk_norm_rope(proj[:, W_QK:2 * W_QK], kw_ref, 1.0)
    v = proj[:, 2 * W_QK:2 * W_QK + W_ATTN]
    u_ref[0] = proj[:, 2 * W_QK + W_ATTN:]
    qt = q.T
    vt = v.T
    hw = 2 * QK_DIM
    row = lax.broadcasted_iota(jnp.int32, (hw, tm), 0)
    tail = jnp.where(lax.broadcasted_iota(jnp.int32, (VT_ROWS - V_DIM, tm), 0) == 0,
                     1.0, 0.0).astype(BF16)
    for hd in range(N_HEADS):
        qh = qt[hd * hw:(hd + 1) * hw, :]
        qt_ref[0, hd, 0] = jnp.where(row < QK_DIM, qh, 0.0).astype(BF16)
        qt_ref[0, hd, 1] = jnp.where(row >= QK_DIM, qh, 0.0).astype(BF16)
        k_ref[0, hd] = k[:, hd * hw:(hd + 1) * hw].astype(BF16)
        vt_ref[0, hd, 0:V_DIM, :] = vt[hd * V_DIM:(hd + 1) * V_DIM, :].astype(BF16)
        vt_ref[0, hd, V_DIM:, :] = tail


def _inproj(x, mod3, n1w, pos3, invf, win_bf, qw, kw, gmat, tm):
    B, S, D = x.shape
    q_scale = math.log2(math.e) / math.sqrt(QK_DIM)
    kern = functools.partial(_inproj_kernel, q_scale=q_scale)
    const2 = lambda b, i: (0, 0)
    return pl.pallas_call(
        kern,
        grid=(B, S // tm),
        in_specs=[pl.BlockSpec((1, tm, D), lambda b, i: (b, i, 0)),
                  pl.BlockSpec((1, 6, D), lambda b, i: (b, 0, 0)),
                  pl.BlockSpec((1, D), const2),
                  pl.BlockSpec((1, tm, 1), lambda b, i: (b, i, 0)),
                  pl.BlockSpec((1, LANES), const2),
                  pl.BlockSpec((D, D_IN_PROJ), const2),
                  pl.BlockSpec((1, W_QK), const2),
                  pl.BlockSpec((1, W_QK), const2),
                  pl.BlockSpec((W_QK, W_QK), const2)],
        out_specs=[pl.BlockSpec((1, N_HEADS, 2, 2 * QK_DIM, tm), lambda b, i: (b, 0, 0, 0, i)),
                   pl.BlockSpec((1, N_HEADS, tm, 2 * QK_DIM), lambda b, i: (b, 0, i, 0)),
                   pl.BlockSpec((1, N_HEADS, VT_ROWS, tm), lambda b, i: (b, 0, 0, i)),
                   pl.BlockSpec((1, tm, W_S5), lambda b, i: (b, i, 0))],
        out_shape=[jax.ShapeDtypeStruct((B, N_HEADS, 2, 2 * QK_DIM, S), BF16),
                   jax.ShapeDtypeStruct((B, N_HEADS, S, 2 * QK_DIM), BF16),
                   jax.ShapeDtypeStruct((B, N_HEADS, VT_ROWS, S), BF16),
                   jax.ShapeDtypeStruct((B, S, W_S5), F32)],
        compiler_params=_cparams(("arbitrary", "arbitrary")),
        name="inproj",
    )(x, mod3, n1w, pos3, invf, win_bf, qw, kw, gmat)


def _attn_kernel(lam_ref, sw_ref, qt_ref, k_ref, vt_ref, o_ref, st_a, st_b, pt_a, pt_b, acc_s, *,
                 tk, heads_per_step):
    S = k_ref.shape[2]
    tq = qt_ref.shape[4]
    lp = lam_ref[...]
    lam = (jnp.exp(jnp.sum(lp[0:1] * lp[1:2], axis=-1, keepdims=True))
           - jnp.exp(jnp.sum(lp[2:3] * lp[3:4], axis=-1, keepdims=True)) + LAMBDA_INIT)
    chains = [(hh, c) for hh in range(heads_per_step) for c in range(2)]
    qts = [qt_ref[0, hh, c] for hh, c in chains]

    def scores(kb, st_buf):
        off = pl.multiple_of(kb * tk, tk)
        for ci, ((hh, c), qt) in enumerate(zip(chains, qts)):
            st_buf[ci] = jnp.dot(k_ref[0, hh, pl.ds(off, tk), :], qt,
                                 preferred_element_type=F32).astype(BF16)

    def softmax(st_buf, pt_buf, ms):
        m_new, alphas = [], []
        for ci, m in enumerate(ms):
            mn = jnp.maximum(m, jnp.max(st_buf[ci], axis=0, keepdims=True).astype(F32))
            alphas.append(jnp.exp2(m - mn))
            pt_buf[ci] = jnp.exp2(st_buf[ci] - mn.astype(BF16))
            m_new.append(mn)
        return tuple(m_new), tuple(alphas)

    def accumulate(kb, pt_buf, alphas):
        off = pl.multiple_of(kb * tk, tk)
        for ci, ((hh, c), alpha) in enumerate(zip(chains, alphas)):
            pv = jnp.dot(vt_ref[0, hh, :, pl.ds(off, tk)], pt_buf[ci],
                         preferred_element_type=F32)
            acc_s[ci] = alpha * acc_s[ci] + pv

    def step(kb, st_cur, pt_cur, st_nxt, pt_prev, ms, alphas_prev):
        accumulate(kb - 1, pt_prev, alphas_prev)
        scores(kb + 1, st_nxt)
        return softmax(st_cur, pt_cur, ms)

    def body(jj, carry):
        ms, alphas = carry
        kb = 2 * jj + 1
        ms, alphas = step(kb, st_b, pt_b, st_a, pt_a, ms, alphas)
        return step(kb + 1, st_a, pt_a, st_b, pt_b, ms, alphas)

    nkb = S // tk
    assert nkb % 2 == 0
    acc_s[...] = jnp.zeros_like(acc_s)
    m0 = tuple(jnp.full((1, tq), -jnp.inf, F32) for _ in chains)
    scores(0, st_a)
    scores(1, st_b)
    ms, alphas = softmax(st_a, pt_a, m0)
    ms, alphas = lax.fori_loop(0, nkb // 2 - 1, body, (ms, alphas))
    accumulate(nkb - 2, pt_a, alphas)
    ms, alphas = softmax(st_b, pt_b, ms)
    accumulate(nkb - 1, pt_b, alphas)
    res = tuple((None, acc_s[ci]) for ci in range(len(chains)))
    outs = []
    for hh in range(heads_per_step):
        a0 = res[2 * hh][1]
        a1 = res[2 * hh + 1][1]
        ot = (a0[:V_DIM, :] / a0[V_DIM:V_DIM + 1, :]
              - lam * (a1[:V_DIM, :] / a1[V_DIM:V_DIM + 1, :]))
        ot = ot * lax.rsqrt(jnp.mean(ot * ot, axis=0, keepdims=True) + RMS_EPS)
        outs.append(ot)
    o = jnp.concatenate(outs, axis=0).T
    o_ref[0] = (o * sw_ref[...] * (1.0 - LAMBDA_INIT)).astype(o_ref.dtype)


def _attn(lam_p, sw2, qt, k, vt, tq, tk):
    B, H, S, _ = k.shape
    hps = 2
    kern = functools.partial(_attn_kernel, tk=tk, heads_per_step=hps)
    return pl.pallas_call(
        kern,
        grid=(B, H // hps, S // tq),
        in_specs=[pl.BlockSpec((4, QK_DIM), lambda b, h, i: (0, 0)),
                  pl.BlockSpec((1, hps * V_DIM), lambda b, h, i: (0, 0)),
                  pl.BlockSpec((1, hps, 2, 2 * QK_DIM, tq), lambda b, h, i: (b, h, 0, 0, i)),
                  pl.BlockSpec((1, hps, S, 2 * QK_DIM), lambda b, h, i: (b, h, 0, 0)),
                  pl.BlockSpec((1, hps, VT_ROWS, S), lambda b, h, i: (b, h, 0, 0))],
        out_specs=pl.BlockSpec((1, tq, hps * V_DIM), lambda b, h, i: (b, i, h)),
        out_shape=jax.ShapeDtypeStruct((B, S, W_ATTN), BF16),
        scratch_shapes=[pltpu.VMEM((2 * hps, tk, tq), BF16), pltpu.VMEM((2 * hps, tk, tq), BF16),
                        pltpu.VMEM((2 * hps, tk, tq), BF16), pltpu.VMEM((2 * hps, tk, tq), BF16),
                        pltpu.VMEM((2 * hps, VT_ROWS, tq), F32)],
        compiler_params=_cparams(("arbitrary", "arbitrary", "arbitrary")),
        name="attn",
    )(lam_p, sw2, qt, k, vt)


def _s5_kernel(u_ref, d_ref, bm_ref, cm_ref, tab_ref, y_ref, xr_s, xi_s, *, rows):
    S = u_ref.shape[1]
    R = rows
    seg = R // SUBLANES
    nchunk = S // R
    P = STATE_LANES

    def cmul_add(ar, ai, xr, xi, br, bi):
        return ar * xr - ai * xi + br, ar * xi + ai * xr + bi

    for di in range(2):
        bm = bm_ref[di, 0]
        cm = cm_ref[di, 0]
        ar = tab_ref[di, 0, 0:1, :]
        ai = tab_ref[di, 0, 1:2, :]
        asr = tab_ref[di, 0, 2:3, :]
        asi = tab_ref[di, 0, 3:4, :]
        steps = list(range(seg)) if di == 0 else list(range(seg - 1, -1, -1))
        segs = list(range(SUBLANES)) if di == 0 else list(range(SUBLANES - 1, -1, -1))

        def chunk_body(ci, carry, di=di, bm=bm, cm=cm, ar=ar, ai=ai, asr=asr, asi=asi,
                       steps=steps, segs=segs):
            c = ci if di == 0 else nchunk - 1 - ci
            r0 = pl.multiple_of(c * R, R)
            u = jnp.concatenate([u_ref[0, pl.ds(r0 + j, SUBLANES, stride=seg), :]
                                 for j in range(seg)], axis=0)
            bu = jnp.dot(u.astype(BF16), bm, preferred_element_type=F32)
            ar8 = jnp.broadcast_to(ar, (SUBLANES, P))
            ai8 = jnp.broadcast_to(ai, (SUBLANES, P))
            zero8 = jnp.zeros((SUBLANES, P), F32)
            xr, xi = zero8, zero8
            for j in steps:
                xr, xi = cmul_add(ar8, ai8, xr, xi, bu[j * SUBLANES:(j + 1) * SUBLANES, :P],
                                  bu[j * SUBLANES:(j + 1) * SUBLANES, P:])
            cr, cim = carry
            ent_r, ent_i = [None] * SUBLANES, [None] * SUBLANES
            for s_ in segs:
                ent_r[s_], ent_i[s_] = cr, cim
                cr, cim = cmul_add(asr, asi, cr, cim, xr[s_:s_ + 1, :], xi[s_:s_ + 1, :])
            xr = jnp.concatenate(ent_r, axis=0)
            xi = jnp.concatenate(ent_i, axis=0)
            for j in steps:
                xr, xi = cmul_add(ar8, ai8, xr, xi, bu[j * SUBLANES:(j + 1) * SUBLANES, :P],
                                  bu[j * SUBLANES:(j + 1) * SUBLANES, P:])
                xr_s[j * SUBLANES:(j + 1) * SUBLANES, :] = xr
                xi_s[j * SUBLANES:(j + 1) * SUBLANES, :] = xi
            yc = (jnp.dot(xr_s[...].astype(BF16), cm[:P], preferred_element_type=F32)
                  + jnp.dot(xi_s[...].astype(BF16), cm[P:], preferred_element_type=F32))
            if di == 0:
                yc = yc + u * d_ref[0]
            for j in range(seg):
                rows_j = pl.ds(r0 + j, SUBLANES, stride=seg)
                piece = yc[j * SUBLANES:(j + 1) * SUBLANES, :]
                if di == 0:
                    y_ref[0, rows_j, :] = piece
                else:
                    y_ref[0, rows_j, :] = y_ref[0, rows_j, :] + piece
            return cr, cim

        zero = jnp.zeros((1, P), F32)
        lax.fori_loop(0, nchunk, chunk_body, (zero, zero))


def _s5(u, d3, bmat, cmat, tab, rows):
    B, S, W = u.shape
    nb = W // LANES
    P = STATE_LANES
    kern = functools.partial(_s5_kernel, rows=rows)
    return pl.pallas_call(
        kern,
        grid=(B, nb),
        in_specs=[pl.BlockSpec((1, S, LANES), lambda b, g: (b, 0, g)),
                  pl.BlockSpec((1, 1, LANES), lambda b, g: (g, 0, 0)),
                  pl.BlockSpec((2, 1, LANES, 2 * P), lambda b, g: (0, g, 0, 0)),
                  pl.BlockSpec((2, 1, 2 * P, LANES), lambda b, g: (0, g, 0, 0)),
                  pl.BlockSpec((2, 1, 4, P), lambda b, g: (0, g, 0, 0))],
        out_specs=pl.BlockSpec((1, S, LANES), lambda b, g: (b, 0, g)),
        out_shape=jax.ShapeDtypeStruct((B, S, W), F32),
        scratch_shapes=[pltpu.VMEM((rows, P), F32), pltpu.VMEM((rows, P), F32)],
        compiler_params=_cparams(("arbitrary", "arbitrary")),
        name="s5",
    )(u, d3, bmat, cmat, tab)


def _s5_tables(lam_re, lam_im, log_step, b_re, b_im, cm_re, cm_im, seg):
    nb = S5_GROUPS // GROUPS_PER_BLOCK
    eye = jnp.eye(GROUPS_PER_BLOCK, dtype=F32)
    bmats, cmats, tabs = [], [], []
    for di in range(2):
        lr = lam_re[di].astype(F32)
        li = lam_im[di].astype(F32)
        delta = jnp.exp(log_step[di].astype(F32))[:, None]
        mag = jnp.exp(lr * delta)
        a_re = mag * jnp.cos(li * delta)
        a_im = mag * jnp.sin(li * delta)
        den = lr * lr + li * li
        num_re = a_re - 1.0
        f_re = (num_re * lr + a_im * li) / den
        f_im = (a_im * lr - num_re * li) / den
        br = b_re[di].astype(F32)
        bi = b_im[di].astype(F32)
        bbar_re = f_re[..., None] * br - f_im[..., None] * bi
        bbar_im = f_re[..., None] * bi + f_im[..., None] * br

        def blockdiag_in(bb):
            bb = bb.reshape(nb, GROUPS_PER_BLOCK, S5_STATE, S5_CH)
            m = jnp.einsum('bgpc,gh->bgchp', bb, eye)
            return m.reshape(nb, LANES, STATE_LANES)

        def blockdiag_out(cc):
            cc = cc.reshape(nb, GROUPS_PER_BLOCK, S5_CH, S5_STATE)
            m = jnp.einsum('bgcp,gh->bgphc', cc, eye)
            return m.reshape(nb, STATE_LANES, LANES)

        bmats.append(jnp.concatenate([blockdiag_in(bbar_re), blockdiag_in(bbar_im)], axis=2))
        cmats.append(jnp.concatenate([blockdiag_out(cm_re[di].astype(F32)),
                                      -blockdiag_out(cm_im[di].astype(F32))], axis=1))
        sr, si = a_re, a_im
        n = 1
        while n < seg:
            sr, si = sr * sr - si * si, 2.0 * sr * si
            n *= 2
        assert n == seg
        tabs.append(jnp.stack([a_re, a_im, sr, si]).reshape(4, nb, STATE_LANES)
                    .transpose(1, 0, 2))
    return jnp.stack(bmats).astype(BF16), jnp.stack(cmats).astype(BF16), jnp.stack(tabs)


def _post_kernel(a_ref, ys_ref, x_ref, mod_ref, gluw_ref, glub_ref, onw_ref, wout_ref, n2w_ref,
                 rw_ref, rb_ref, x1_ref, h2_ref, eidx_ref, gate_ref, rank_ref, cnt_ref, carry_s):
    first = (pl.program_id(0) == 0) & (pl.program_id(1) == 0)

    @pl.when(first)
    def _():
        carry_s[...] = jnp.zeros_like(carry_s)

    tm = x_ref.shape[1]
    y = ys_ref[0]
    y = 0.5 * y * (1.0 + jnp.tanh(math.sqrt(2.0 / math.pi) * (y + 0.044715 * (y * y * y))))
    g = jnp.dot(y.astype(BF16), gluw_ref[...], preferred_element_type=F32) + glub_ref[...]
    y = y * jax.nn.sigmoid(g)
    s = _rms(y) * onw_ref[...]
    mix = (jnp.dot(a_ref[0], wout_ref[0:W_ATTN, :], preferred_element_type=F32)
           + jnp.dot(s.astype(BF16), wout_ref[W_ATTN:, :], preferred_element_type=F32))
    g1 = mod_ref[0, 2:3, :]
    sh2 = mod_ref[0, 3:4, :]
    sc2 = mod_ref[0, 4:5, :]
    x1 = x_ref[0] + g1 * mix
    x1_ref[0] = x1
    h2 = (_rms(x1) * n2w_ref[...]) * (1.0 + sc2) + sh2
    h2_ref[0] = h2
    logits = jnp.dot(h2, rw_ref[...], preferred_element_type=F32,
                     precision=lax.Precision.HIGHEST) + rb_ref[...]
    lane = lax.broadcasted_iota(jnp.int32, (tm, LANES), 1)
    neg = jnp.float32(-jnp.inf)
    work = jnp.where(lane < N_EXPERTS, logits, neg)
    vals, idxs = [], []
    for _ in range(TOP_K):
        m = jnp.max(work, axis=-1, keepdims=True)
        idx = jnp.min(jnp.where(work == m, lane, LANES), axis=-1, keepdims=True)
        vals.append(m)
        idxs.append(idx)
        work = jnp.where(lane == idx, neg, work)
    es = [jnp.exp(v - vals[0]) for v in vals]
    den = es[0] + es[1] + es[2] + es[3]
    onehot = jnp.zeros((tm, LANES), F32)
    for idx in idxs:
        onehot = onehot + jnp.where(lane == idx, 1.0, 0.0)
    r_i = lax.broadcasted_iota(jnp.int32, (tm, tm), 0)
    c_i = lax.broadcasted_iota(jnp.int32, (tm, tm), 1)
    ltri = jnp.where(c_i < r_i, 1.0, 0.0).astype(BF16)
    before = jnp.dot(ltri, onehot.astype(BF16), preferred_element_type=F32) + carry_s[...]
    lane4 = lax.broadcasted_iota(jnp.int32, (tm, TOP_K), 1)
    e_out = jnp.zeros((tm, TOP_K), jnp.int32)
    g_out = jnp.zeros((tm, TOP_K), F32)
    r_out = jnp.zeros((tm, TOP_K), jnp.int32)
    for j in range(TOP_K):
        rk = jnp.sum(jnp.where(lane == idxs[j], before, 0.0), axis=-1, keepdims=True)
        e_out = jnp.where(lane4 == j, idxs[j], e_out)
        g_out = jnp.where(lane4 == j, es[j] / den, g_out)
        r_out = jnp.where(lane4 == j, rk.astype(jnp.int32), r_out)
    eidx_ref[0] = e_out
    gate_ref[0] = g_out
    rank_ref[0] = r_out
    carry_s[...] = carry_s[...] + jnp.sum(onehot, axis=0, keepdims=True)
    cnt_ref[...] = carry_s[...]


def _post(a, ys, x, mod3, gluw, glub, onw, wout, n2w, rw, rb, tm):
    B, S, D = x.shape
    c2 = lambda b, i: (0, 0)
    tok = lambda b, i: (b, i, 0)
    return pl.pallas_call(
        _post_kernel,
        grid=(B, S // tm),
        in_specs=[pl.BlockSpec((1, tm, W_ATTN), tok),
                  pl.BlockSpec((1, tm, W_S5), tok),
                  pl.BlockSpec((1, tm, D), tok),
                  pl.BlockSpec((1, 6, D), lambda b, i: (b, 0, 0)),
                  pl.BlockSpec((W_S5, W_S5), c2),
                  pl.BlockSpec((1, W_S5), c2),
                  pl.BlockSpec((1, W_S5), c2),
                  pl.BlockSpec((D, D), c2),
                  pl.BlockSpec((1, D), c2),
                  pl.BlockSpec((D, LANES), c2),
                  pl.BlockSpec((1, LANES), c2)],
        out_specs=[pl.BlockSpec((1, tm, D), tok),
                   pl.BlockSpec((1, tm, D), tok),
                   pl.BlockSpec((1, tm, TOP_K), tok),
                   pl.BlockSpec((1, tm, TOP_K), tok),
                   pl.BlockSpec((1, tm, TOP_K), tok),
                   pl.BlockSpec((1, LANES), c2)],
        out_shape=[jax.ShapeDtypeStruct((B, S, D), F32),
                   jax.ShapeDtypeStruct((B, S, D), F32),
                   jax.ShapeDtypeStruct((B, S, TOP_K), jnp.int32),
                   jax.ShapeDtypeStruct((B, S, TOP_K), F32),
                   jax.ShapeDtypeStruct((B, S, TOP_K), jnp.int32),
                   jax.ShapeDtypeStruct((1, LANES), F32)],
        scratch_shapes=[pltpu.VMEM((1, LANES), F32)],
        compiler_params=_cparams(("arbitrary", "arbitrary")),
        name="post",
    )(a, ys, x, mod3, gluw, glub, onw, wout, n2w, rw, rb)


def _dispatch_kernel(pend_ref, padded_ref, dest_ref, h2_ref, xs_hbm, zero_s, sem, zsem, *, tg):
    @pl.when(pl.program_id(0) == 0)
    def _():
        zero_s[...] = jnp.zeros_like(zero_s)

        def zero_copy(e):
            start = pl.multiple_of(pend_ref[e] - MOE_ROWS, MOE_ROWS)
            return pltpu.make_async_copy(zero_s, xs_hbm.at[pl.ds(start, MOE_ROWS)], zsem)

        for e in range(N_EXPERTS):
            @pl.when(padded_ref[e] > 0)
            def _():
                zero_copy(e).start()
        for e in range(N_EXPERTS):
            @pl.when(padded_ref[e] > 0)
            def _():
                zero_copy(e).wait()

    def row_copy(t, d):
        return pltpu.make_async_copy(h2_ref.at[pl.ds(t, 1)], xs_hbm.at[pl.ds(d, 1)], sem)

    def issue(t, _):
        for j in range(TOP_K):
            row_copy(t, dest_ref[0, 0, t * TOP_K + j]).start()
        return 0

    lax.fori_loop(0, tg, issue, 0, unroll=2)
    for _ in range(TOP_K):
        pltpu.make_async_copy(h2_ref, xs_hbm.at[pl.ds(0, tg)], sem).wait()


def _dispatch(pad_end, padded, dest3, h2, n_rows, tg):
    T, D = h2.shape
    kern = functools.partial(_dispatch_kernel, tg=tg)
    grid_spec = pltpu.PrefetchScalarGridSpec(
        num_scalar_prefetch=2,
        grid=(T // tg,),
        in_specs=[pl.BlockSpec((1, 1, tg * TOP_K), lambda i, pe, pd: (i, 0, 0),
                               memory_space=pltpu.SMEM),
                  pl.BlockSpec((tg, D), lambda i, pe, pd: (i, 0))],
        out_specs=pl.BlockSpec(memory_space=pl.ANY),
        scratch_shapes=[pltpu.VMEM((MOE_ROWS, D), F32), pltpu.SemaphoreType.DMA(()),
                        pltpu.SemaphoreType.DMA(())],
    )
    return pl.pallas_call(
        kern,
        grid_spec=grid_spec,
        out_shape=jax.ShapeDtypeStruct((n_rows, D), F32),
        compiler_params=_cparams(("arbitrary",)),
        name="dispatch",
    )(pad_end, padded, dest3, h2)


PERM_CHUNK = 2 * LANES


def _expert_kernel(be_ref, na_ref, x_ref, w1_ref, bg_ref, bl_ref, w2_ref, b2_ref, perm_ref, y_ref,
                   wg_s, wl_s, w2_s):
    i = pl.program_id(0)
    active = i < na_ref[0]
    prev = be_ref[jnp.maximum(i - 1, 0)]
    changed = active & ((i == 0) | (be_ref[i] != prev))

    @pl.when(changed)
    def _():
        perm = perm_ref[...]
        for j in range(2 * D_FF // PERM_CHUNK):
            chunk = w1_ref[0, :, j * PERM_CHUNK:(j + 1) * PERM_CHUNK].astype(BF16)
            sep = jnp.dot(chunk, perm, preferred_element_type=F32).astype(BF16)
            wg_s[:, j * LANES:(j + 1) * LANES] = sep[:, :LANES]
            wl_s[:, j * LANES:(j + 1) * LANES] = sep[:, LANES:]
        w2_s[...] = w2_ref[0].astype(BF16)

    @pl.when(active)
    def _():
        x = x_ref[...].astype(BF16)
        zg = jnp.dot(x, wg_s[...], preferred_element_type=F32) + bg_ref[0]
        zl = jnp.dot(x, wl_s[...], preferred_element_type=F32) + bl_ref[0]
        xg = jnp.minimum(zg, SWIGLU_LIMIT)
        xl = jnp.clip(zl, -SWIGLU_LIMIT, SWIGLU_LIMIT)
        act = xg * jax.nn.sigmoid(SWIGLU_ALPHA * xg) * (xl + 1.0)
        y_ref[...] = jnp.dot(act.astype(BF16), w2_s[...], preferred_element_type=F32) + b2_ref[0]


def _experts(block_e, n_active, xs, w1, bg, bl, w2, b2, perm):
    n_rows, D = xs.shape
    nblk = n_rows // MOE_ROWS

    def row_map(i, be, na):
        return (jnp.minimum(i, na[0] - 1), 0)

    def w_map(i, be, na):
        return (be[i], 0, 0)

    grid_spec = pltpu.PrefetchScalarGridSpec(
        num_scalar_prefetch=2,
        grid=(nblk,),
        in_specs=[pl.BlockSpec((MOE_ROWS, D), row_map),
                  pl.BlockSpec((1, D, 2 * D_FF), w_map),
                  pl.BlockSpec((1, 1, D_FF), w_map),
                  pl.BlockSpec((1, 1, D_FF), w_map),
                  pl.BlockSpec((1, D_FF, D), w_map),
                  pl.BlockSpec((1, 1, D), w_map),
                  pl.BlockSpec((PERM_CHUNK, PERM_CHUNK), lambda i, be, na: (0, 0))],
        out_specs=pl.BlockSpec((MOE_ROWS, D), row_map),
        scratch_shapes=[pltpu.VMEM((D, D_FF), BF16), pltpu.VMEM((D, D_FF), BF16),
                        pltpu.VMEM((D_FF, D), BF16)],
    )
    return pl.pallas_call(
        _expert_kernel,
        grid_spec=grid_spec,
        out_shape=jax.ShapeDtypeStruct((n_rows, D), F32),
        compiler_params=_cparams(("arbitrary",)),
        name="experts",
    )(block_e, n_active, xs, w1, bg, bl, w2, b2, perm)


def _combine_kernel(dest_ref, gate_ref, x1_ref, mod_ref, ys_hbm, o_ref, buf, sem, *, tc):
    def row_copy(t, j, d):
        return pltpu.make_async_copy(ys_hbm.at[pl.ds(d, 1)], buf.at[j, pl.ds(t, 1)], sem)

    def issue(t, _):
        for j in range(TOP_K):
            row_copy(t, j, dest_ref[0, 0, t * TOP_K + j]).start()
        return 0

    lax.fori_loop(0, tc, issue, 0, unroll=2)
    for j in range(TOP_K):
        pltpu.make_async_copy(ys_hbm.at[pl.ds(0, tc)], buf.at[j], sem).wait()
    gates = gate_ref[...]
    acc = gates[:, 0:1] * buf[0]
    for j in range(1, TOP_K):
        acc = acc + gates[:, j:j + 1] * buf[j]
    o_ref[...] = x1_ref[...] + mod_ref[0, 5:6, :] * acc


def _combine(dest3, gates, x1, mod3, ys, tc, tiles_per_batch):
    T, D = x1.shape
    kern = functools.partial(_combine_kernel, tc=tc)
    return pl.pallas_call(
        kern,
        grid=(T // tc,),
        in_specs=[pl.BlockSpec((1, 1, tc * TOP_K), lambda i: (i, 0, 0), memory_space=pltpu.SMEM),
                  pl.BlockSpec((tc, TOP_K), lambda i: (i, 0)),
                  pl.BlockSpec((tc, D), lambda i: (i, 0)),
                  pl.BlockSpec((1, 6, D), lambda i: (i // tiles_per_batch, 0, 0)),
                  pl.BlockSpec(memory_space=pl.ANY)],
        out_specs=pl.BlockSpec((tc, D), lambda i: (i, 0)),
        out_shape=jax.ShapeDtypeStruct((T, D), F32),
        scratch_shapes=[pltpu.VMEM((TOP_K, tc, D), F32), pltpu.SemaphoreType.DMA(())],
        compiler_params=_cparams(("arbitrary",)),
        name="combine",
    )(dest3, gates, x1, mod3, ys)


def kernel(x, c, positions, ada_w, ada_b, norm1_w, w_in, q_norm_w, k_norm_w, lambda_q1, lambda_k1,
           lambda_q2, lambda_k2, subln_w, s5_lambda_re, s5_lambda_im, s5_log_step, s5_b_re, s5_b_im,
           s5_cmat_re, s5_cmat_im, s5_d, s5_glu_w, s5_glu_b, s5_out_norm_w, w_out, norm2_w,
           router_w, router_b, mlp1_w, mlp1_b, mlp2_w, mlp2_b):
    B, S, D = x.shape
    T = B * S
    l = 0

    c_pad = jnp.pad(c, ((0, SUBLANES - B % SUBLANES if B % SUBLANES else 0), (0, 0)))
    mod = _ada(c_pad, ada_w[l], ada_b[l][None, :])[:B]
    mod3 = mod.reshape(B, 6, D)

    tm = min(512, S)
    inv_freq = ROPE_THETA ** (-jnp.arange(0, ROT_DIM, 2, dtype=F32) / ROT_DIM)
    d_in_head = jnp.arange(LANES) % QK_DIM
    invf = jnp.where(d_in_head < ROT_DIM, inv_freq[d_in_head % (ROT_DIM // 2)], 0.0)[None, :]
    pos3 = positions.astype(F32)[..., None]
    gmat = jnp.kron(jnp.eye(W_QK // QK_DIM, dtype=F32), jnp.ones((QK_DIM, QK_DIM), F32)).astype(BF16)
    qw = jnp.tile(q_norm_w[l], W_QK // QK_DIM)[None, :]
    kw = jnp.tile(k_norm_w[l], W_QK // QK_DIM)[None, :]
    qt, k, vt, u = _inproj(x, mod3, norm1_w[l][None, :], pos3, invf, w_in[l].astype(BF16),
                          qw, kw, gmat, tm)

    lam_p = jnp.stack([lambda_q1[l], lambda_k1[l], lambda_q2[l], lambda_k2[l]]).astype(F32)
    sw2 = jnp.tile(subln_w[l], 2)[None, :]
    a_out = _attn(lam_p, sw2, qt, k, vt, tq=min(512, S), tk=min(256, S // 2))

    s5_rows = min(256, S)
    bmat, cmat, tab = _s5_tables(s5_lambda_re[l], s5_lambda_im[l], s5_log_step[l], s5_b_re[l],
                                 s5_b_im[l], s5_cmat_re[l], s5_cmat_im[l], s5_rows // SUBLANES)
    d3 = s5_d[l].astype(F32).reshape(W_S5 // LANES, 1, LANES)
    y_s5 = _s5(u, d3, bmat, cmat, tab, rows=s5_rows)

    rw = jnp.pad(router_w[l], ((0, 0), (0, LANES - N_EXPERTS)))
    rb = jnp.pad(router_b[l], (0, LANES - N_EXPERTS))[None, :]
    tp = min(256, S)
    x1, h2, eidx, gates, rank, counts = _post(
        a_out, y_s5, x, mod3, s5_glu_w[l].astype(BF16), s5_glu_b[l][None, :],
        s5_out_norm_w[l][None, :], w_out[l].astype(BF16), norm2_w[l][None, :], rw, rb, tp)

    counts = counts[0, :N_EXPERTS].astype(jnp.int32)
    padded = ((counts + MOE_ROWS - 1) // MOE_ROWS) * MOE_ROWS
    pad_end = jnp.cumsum(padded)
    pad_start = pad_end - padded
    eflat = eidx.reshape(T * TOP_K)
    dest = pad_start[eflat] + rank.reshape(T * TOP_K)
    n_rows = T * TOP_K + N_EXPERTS * MOE_ROWS
    nblk = n_rows // MOE_ROWS
    blk_row = jnp.arange(nblk, dtype=jnp.int32)[:, None] * MOE_ROWS
    block_e = jnp.minimum(jnp.sum((blk_row >= pad_end[None, :]).astype(jnp.int32), axis=1),
                          N_EXPERTS - 1)
    n_active = (pad_end[-1] // MOE_ROWS).astype(jnp.int32)[None]

    tg = min(512, T)
    xs = _dispatch(pad_end.astype(jnp.int32), padded, dest.reshape(T // tg, 1, tg * TOP_K),
                   h2.reshape(T, D), n_rows, tg)

    bg = mlp1_b[l][:, None, 0::2]
    bl = mlp1_b[l][:, None, 1::2]
    src = jnp.arange(PERM_CHUNK)
    perm = (jnp.arange(PERM_CHUNK)[None, :] == ((src % 2) * LANES + src // 2)[:, None]).astype(BF16)
    ys = _experts(block_e, n_active, xs, mlp1_w[l], bg, bl, mlp2_w[l], mlp2_b[l][:, None, :], perm)

    tc = min(256, S)
    out = _combine(dest.reshape(T // tc, 1, tc * TOP_K), gates.reshape(T, TOP_K),
                   x1.reshape(T, D), mod3, ys, tc, S // tc)
    return out.reshape(B, S, D)
```

```python
import functools
import math

import jax
import jax.numpy as jnp
from jax import lax
from jax.experimental import pallas as pl
from jax.experimental.pallas import tpu as pltpu

F32 = jnp.float32
BF16 = jnp.bfloat16

D_MODEL = 1024
QK_DIM = 32
V_DIM = 64
N_HEADS = 8
W_QK = N_HEADS * 2 * QK_DIM
W_ATTN = N_HEADS * V_DIM
ROT_DIM = QK_DIM // 4
ROPE_THETA = 500000.0
S5_CH = 16
S5_STATE = 64
W_S5 = 512
S5_GROUPS = W_S5 // S5_CH
D_IN_PROJ = 2 * W_QK + W_ATTN + W_S5
N_EXPERTS = 32
TOP_K = 4
D_FF = D_MODEL
SWIGLU_ALPHA = 1.702
SWIGLU_LIMIT = 7.0
RMS_EPS = 1e-6
LAMBDA_INIT = 0.8 - 0.6 * math.exp(-0.3 * 0)

LANES = 128
SUBLANES = 8
VMEM_LIMIT = 56 * 1024 * 1024

GROUPS_PER_BLOCK = LANES // S5_CH
STATE_LANES = GROUPS_PER_BLOCK * S5_STATE
MOE_ROWS = 512
VT_ROWS = V_DIM + 16
F8 = jnp.float8_e4m3fn
F8_MAX = 448.0
QK_CONTRACT = 4 * 2 * QK_DIM


def _cparams(sem):
    return pltpu.CompilerParams(dimension_semantics=sem, vmem_limit_bytes=VMEM_LIMIT)


def _rms(x, eps=RMS_EPS):
    return x * lax.rsqrt(jnp.mean(x * x, axis=-1, keepdims=True) + eps)


def _ada_kernel(c_ref, w_ref, b_ref, o_ref):
    c = c_ref[...]
    ca = c * jax.nn.sigmoid(c)
    o_ref[...] = jnp.dot(ca, w_ref[...], preferred_element_type=F32,
                         precision=lax.Precision.HIGHEST) + b_ref[...]


def _ada(c_pad, w, b):
    rows, d = c_pad.shape
    n = w.shape[1]
    tn = 1536
    return pl.pallas_call(
        _ada_kernel,
        grid=(n // tn,),
        in_specs=[pl.BlockSpec((rows, d), lambda j: (0, 0)),
                  pl.BlockSpec((d, tn), lambda j: (0, j)),
                  pl.BlockSpec((1, tn), lambda j: (0, j))],
        out_specs=pl.BlockSpec((rows, tn), lambda j: (0, j)),
        out_shape=jax.ShapeDtypeStruct((rows, n), F32),
        compiler_params=_cparams(("arbitrary",)),
        name="ada",
    )(c_pad, w, b)


def _inproj_kernel(x_ref, mod_ref, n1w_ref, pos_ref, invf_ref, win_ref, qw_ref, kw_ref, gm_ref,
                   qt_ref, k_ref, vt_ref, u_ref, *, q_scale):
    x = x_ref[0]
    tm = x.shape[0]
    sh1 = mod_ref[0, 0:1, :]
    sc1 = mod_ref[0, 1:2, :]
    h = (_rms(x) * n1w_ref[...]) * (1.0 + sc1) + sh1
    proj = jnp.dot(h.astype(BF16), win_ref[...], preferred_element_type=F32)

    ang = pos_ref[0] * invf_ref[...]
    cos = jnp.cos(ang)
    sin = jnp.sin(ang)
    d_in_head = lax.broadcasted_iota(jnp.int32, (1, LANES), 1) % QK_DIM
    half = ROT_DIM // 2
    s_lo = jnp.where(d_in_head < half, -sin, 0.0)
    s_hi = jnp.where((d_in_head >= half) & (d_in_head < ROT_DIM), sin, 0.0)
    reps = W_QK // LANES
    cos = jnp.concatenate([cos] * reps, axis=1)
    s_lo = jnp.concatenate([s_lo] * reps, axis=1)
    s_hi = jnp.concatenate([s_hi] * reps, axis=1)

    def qk_norm_rope(t, w_ref, scale):
        ssq = jnp.dot((t * t).astype(BF16), gm_ref[...], preferred_element_type=F32)
        tn = t * lax.rsqrt(ssq * (1.0 / QK_DIM) + RMS_EPS) * w_ref[...]
        r = (tn * cos + pltpu.roll(tn, W_QK - half, 1) * s_lo + pltpu.roll(tn, half, 1) * s_hi)
        return r * scale

    q = qk_norm_rope(proj[:, 0:W_QK], qw_ref, q_scale)
    k = qk_norm_rope(proj[:, W_QK:2 * W_QK], kw_ref, 1.0)
    v = proj[:, 2 * W_QK:2 * W_QK + W_ATTN]
    u_ref[0] = proj[:, 2 * W_QK + W_ATTN:]
    qt = q.T
    vt = v.T
    hw = 2 * QK_DIM

    def split8(x):
        xc = jnp.clip(x, -F8_MAX, F8_MAX)
        hi = xc.astype(F8)
        return hi, (xc - hi.astype(F32)).astype(F8)

    k_hi, k_lo = split8(k)
    map_of_row = (lax.broadcasted_iota(jnp.int32, (W_QK, tm), 0) // QK_DIM) % 2
    q_parts = [split8(jnp.where(map_of_row == c, qt, 0.0)) for c in range(2)]
    tail = jnp.where(lax.broadcasted_iota(jnp.int32, (VT_ROWS - V_DIM, tm), 0) == 0,
                     1.0, 0.0).astype(BF16)
    zk = jnp.zeros((tm, hw), F8)
    zq = jnp.zeros((hw, tm), F8)
    for hd in range(N_HEADS):
        cols = slice(hd * hw, (hd + 1) * hw)
        k_ref[0, hd, :, 0 * hw:1 * hw] = k_hi[:, cols]
        k_ref[0, hd, :, 1 * hw:2 * hw] = k_lo[:, cols]
        k_ref[0, hd, :, 2 * hw:3 * hw] = k_hi[:, cols]
        k_ref[0, hd, :, 3 * hw:4 * hw] = zk
        for c in range(2):
            q_hi, q_lo = q_parts[c]
            qt_ref[0, hd, c, 0 * hw:1 * hw, :] = q_hi[cols, :]
            qt_ref[0, hd, c, 1 * hw:2 * hw, :] = q_hi[cols, :]
            qt_ref[0, hd, c, 2 * hw:3 * hw, :] = q_lo[cols, :]
            qt_ref[0, hd, c, 3 * hw:4 * hw, :] = zq
        vt_ref[0, hd, 0:V_DIM, :] = vt[hd * V_DIM:(hd + 1) * V_DIM, :].astype(BF16)
        vt_ref[0, hd, V_DIM:, :] = tail


def _inproj(x, mod3, n1w, pos3, invf, win_bf, qw, kw, gmat, tm):
    B, S, D = x.shape
    q_scale = math.log2(math.e) / math.sqrt(QK_DIM)
    kern = functools.partial(_inproj_kernel, q_scale=q_scale)
    const2 = lambda b, i: (0, 0)
    return pl.pallas_call(
        kern,
        grid=(B, S // tm),
        in_specs=[pl.BlockSpec((1, tm, D), lambda b, i: (b, i, 0)),
                  pl.BlockSpec((1, 6, D), lambda b, i: (b, 0, 0)),
                  pl.BlockSpec((1, D), const2),
                  pl.BlockSpec((1, tm, 1), lambda b, i: (b, i, 0)),
                  pl.BlockSpec((1, LANES), const2),
                  pl.BlockSpec((D, D_IN_PROJ), const2),
                  pl.BlockSpec((1, W_QK), const2),
                  pl.BlockSpec((1, W_QK), const2),
                  pl.BlockSpec((W_QK, W_QK), const2)],
        out_specs=[pl.BlockSpec((1, N_HEADS, 2, QK_CONTRACT, tm), lambda b, i: (b, 0, 0, 0, i)),
                   pl.BlockSpec((1, N_HEADS, tm, QK_CONTRACT), lambda b, i: (b, 0, i, 0)),
                   pl.BlockSpec((1, N_HEADS, VT_ROWS, tm), lambda b, i: (b, 0, 0, i)),
                   pl.BlockSpec((1, tm, W_S5), lambda b, i: (b, i, 0))],
        out_shape=[jax.ShapeDtypeStruct((B, N_HEADS, 2, QK_CONTRACT, S), F8),
                   jax.ShapeDtypeStruct((B, N_HEADS, S, QK_CONTRACT), F8),
                   jax.ShapeDtypeStruct((B, N_HEADS, VT_ROWS, S), BF16),
                   jax.ShapeDtypeStruct((B, S, W_S5), F32)],
        compiler_params=_cparams(("arbitrary", "arbitrary")),
        name="inproj",
    )(x, mod3, n1w, pos3, invf, win_bf, qw, kw, gmat)


def _attn_kernel(lam_ref, sw_ref, qt_ref, k_ref, vt_ref, o_ref, st_a, st_b, pt_a, pt_b, acc_s, *,
                 tk, heads_per_step):
    S = k_ref.shape[2]
    tq = qt_ref.shape[4]
    lp = lam_ref[...]
    lam = (jnp.exp(jnp.sum(lp[0:1] * lp[1:2], axis=-1, keepdims=True))
           - jnp.exp(jnp.sum(lp[2:3] * lp[3:4], axis=-1, keepdims=True)) + LAMBDA_INIT)
    chains = [(hh, c) for hh in range(heads_per_step) for c in range(2)]
    qts = [qt_ref[0, hh, c] for hh, c in chains]

    def scores(kb, st_buf):
        off = pl.multiple_of(kb * tk, tk)
        for ci, ((hh, c), qt) in enumerate(zip(chains, qts)):
            st_buf[ci] = jnp.dot(k_ref[0, hh, pl.ds(off, tk), :], qt,
                                 preferred_element_type=F32).astype(BF16)

    def softmax(st_buf, pt_buf, ms):
        m_new, alphas = [], []
        for ci, m in enumerate(ms):
            mn = jnp.maximum(m, jnp.max(st_buf[ci], axis=0, keepdims=True).astype(F32))
            alphas.append(jnp.exp2(m - mn))
            pt_buf[ci] = jnp.exp2(st_buf[ci] - mn.astype(BF16))
            m_new.append(mn)
        return tuple(m_new), tuple(alphas)

    def accumulate(kb, pt_buf, alphas):
        off = pl.multiple_of(kb * tk, tk)
        for ci, ((hh, c), alpha) in enumerate(zip(chains, alphas)):
            pv = jnp.dot(vt_ref[0, hh, :, pl.ds(off, tk)], pt_buf[ci],
                         preferred_element_type=F32)
            acc_s[ci] = alpha * acc_s[ci] + pv

    def step(kb, st_cur, pt_cur, st_nxt, pt_prev, ms, alphas_prev):
        scores(kb + 1, st_nxt)
        accumulate(kb - 1, pt_prev, alphas_prev)
        return softmax(st_cur, pt_cur, ms)

    def body(jj, carry):
        ms, alphas = carry
        kb = 2 * jj + 1
        ms, alphas = step(kb, st_b, pt_b, st_a, pt_a, ms, alphas)
        return step(kb + 1, st_a, pt_a, st_b, pt_b, ms, alphas)

    nkb = S // tk
    assert nkb % 2 == 0
    acc_s[...] = jnp.zeros_like(acc_s)
    m0 = tuple(jnp.full((1, tq), -jnp.inf, F32) for _ in chains)
    scores(0, st_a)
    scores(1, st_b)
    ms, alphas = softmax(st_a, pt_a, m0)
    ms, alphas = lax.fori_loop(0, nkb // 2 - 1, body, (ms, alphas))
    accumulate(nkb - 2, pt_a, alphas)
    ms, alphas = softmax(st_b, pt_b, ms)
    accumulate(nkb - 1, pt_b, alphas)
    res = tuple((None, acc_s[ci]) for ci in range(len(chains)))
    outs = []
    for hh in range(heads_per_step):
        a0 = res[2 * hh][1]
        a1 = res[2 * hh + 1][1]
        ot = (a0[:V_DIM, :] / a0[V_DIM:V_DIM + 1, :]
              - lam * (a1[:V_DIM, :] / a1[V_DIM:V_DIM + 1, :]))
        ot = ot * lax.rsqrt(jnp.mean(ot * ot, axis=0, keepdims=True) + RMS_EPS)
        outs.append(ot)
    o = jnp.concatenate(outs, axis=0).T
    o_ref[0] = (o * sw_ref[...] * (1.0 - LAMBDA_INIT)).astype(o_ref.dtype)


def _attn(lam_p, sw2, qt, k, vt, tq, tk):
    B, H, S, _ = k.shape
    hps = 2
    kern = functools.partial(_attn_kernel, tk=tk, heads_per_step=hps)
    return pl.pallas_call(
        kern,
        grid=(B, H // hps, S // tq),
        in_specs=[pl.BlockSpec((4, QK_DIM), lambda b, h, i: (0, 0)),
                  pl.BlockSpec((1, hps * V_DIM), lambda b, h, i: (0, 0)),
                  pl.BlockSpec((1, hps, 2, QK_CONTRACT, tq), lambda b, h, i: (b, h, 0, 0, i)),
                  pl.BlockSpec((1, hps, S, QK_CONTRACT), lambda b, h, i: (b, h, 0, 0)),
                  pl.BlockSpec((1, hps, VT_ROWS, S), lambda b, h, i: (b, h, 0, 0))],
        out_specs=pl.BlockSpec((1, tq, hps * V_DIM), lambda b, h, i: (b, i, h)),
        out_shape=jax.ShapeDtypeStruct((B, S, W_ATTN), BF16),
        scratch_shapes=[pltpu.VMEM((2 * hps, tk, tq), BF16), pltpu.VMEM((2 * hps, tk, tq), BF16),
                        pltpu.VMEM((2 * hps, tk, tq), BF16), pltpu.VMEM((2 * hps, tk, tq), BF16),
                        pltpu.VMEM((2 * hps, VT_ROWS, tq), F32)],
        compiler_params=_cparams(("arbitrary", "arbitrary", "arbitrary")),
        name="attn",
    )(lam_p, sw2, qt, k, vt)


def _s5_kernel(u_ref, d_ref, bm_ref, cm_ref, tab_ref, y_ref, xr_s, xi_s, *, rows):
    S = u_ref.shape[1]
    R = rows
    seg = R // SUBLANES
    nchunk = S // R
    P = STATE_LANES

    def cmul_add(ar, ai, xr, xi, br, bi):
        return ar * xr - ai * xi + br, ar * xi + ai * xr + bi

    for di in range(2):
        bm = bm_ref[di, 0]
        cm = cm_ref[di, 0]
        ar = tab_ref[di, 0, 0:1, :]
        ai = tab_ref[di, 0, 1:2, :]
        asr = tab_ref[di, 0, 2:3, :]
        asi = tab_ref[di, 0, 3:4, :]
        steps = list(range(seg)) if di == 0 else list(range(seg - 1, -1, -1))
        segs = list(range(SUBLANES)) if di == 0 else list(range(SUBLANES - 1, -1, -1))

        def chunk_body(ci, carry, di=di, bm=bm, cm=cm, ar=ar, ai=ai, asr=asr, asi=asi,
                       steps=steps, segs=segs):
            c = ci if di == 0 else nchunk - 1 - ci
            r0 = pl.multiple_of(c * R, R)
            u = jnp.concatenate([u_ref[0, pl.ds(r0 + j, SUBLANES, stride=seg), :]
                                 for j in range(seg)], axis=0)
            bu = jnp.dot(u.astype(BF16), bm, preferred_element_type=F32)
            ar8 = jnp.broadcast_to(ar, (SUBLANES, P))
            ai8 = jnp.broadcast_to(ai, (SUBLANES, P))
            zero8 = jnp.zeros((SUBLANES, P), F32)
            xr, xi = zero8, zero8
            for j in steps:
                xr, xi = cmul_add(ar8, ai8, xr, xi, bu[j * SUBLANES:(j + 1) * SUBLANES, :P],
                                  bu[j * SUBLANES:(j + 1) * SUBLANES, P:])
            cr, cim = carry
            ent_r, ent_i = [None] * SUBLANES, [None] * SUBLANES
            for s_ in segs:
                ent_r[s_], ent_i[s_] = cr, cim
                cr, cim = cmul_add(asr, asi, cr, cim, xr[s_:s_ + 1, :], xi[s_:s_ + 1, :])
            xr = jnp.concatenate(ent_r, axis=0)
            xi = jnp.concatenate(ent_i, axis=0)
            for j in steps:
                xr, xi = cmul_add(ar8, ai8, xr, xi, bu[j * SUBLANES:(j + 1) * SUBLANES, :P],
                                  bu[j * SUBLANES:(j + 1) * SUBLANES, P:])
                xr_s[j * SUBLANES:(j + 1) * SUBLANES, :] = xr
                xi_s[j * SUBLANES:(j + 1) * SUBLANES, :] = xi
            yc = (jnp.dot(xr_s[...].astype(BF16), cm[:P], preferred_element_type=F32)
                  + jnp.dot(xi_s[...].astype(BF16), cm[P:], preferred_element_type=F32))
            if di == 0:
                yc = yc + u * d_ref[0]
            for j in range(seg):
                rows_j = pl.ds(r0 + j, SUBLANES, stride=seg)
                piece = yc[j * SUBLANES:(j + 1) * SUBLANES, :]
                if di == 0:
                    y_ref[0, rows_j, :] = piece
                else:
                    y_ref[0, rows_j, :] = y_ref[0, rows_j, :] + piece
            return cr, cim

        zero = jnp.zeros((1, P), F32)
        lax.fori_loop(0, nchunk, chunk_body, (zero, zero))


def _s5(u, d3, bmat, cmat, tab, rows):
    B, S, W = u.shape
    nb = W // LANES
    P = STATE_LANES
    kern = functools.partial(_s5_kernel, rows=rows)
    return pl.pallas_call(
        kern,
        grid=(B, nb),
        in_specs=[pl.BlockSpec((1, S, LANES), lambda b, g: (b, 0, g)),
                  pl.BlockSpec((1, 1, LANES), lambda b, g: (g, 0, 0)),
                  pl.BlockSpec((2, 1, LANES, 2 * P), lambda b, g: (0, g, 0, 0)),
                  pl.BlockSpec((2, 1, 2 * P, LANES), lambda b, g: (0, g, 0, 0)),
                  pl.BlockSpec((2, 1, 4, P), lambda b, g: (0, g, 0, 0))],
        out_specs=pl.BlockSpec((1, S, LANES), lambda b, g: (b, 0, g)),
        out_shape=jax.ShapeDtypeStruct((B, S, W), F32),
        scratch_shapes=[pltpu.VMEM((rows, P), F32), pltpu.VMEM((rows, P), F32)],
        compiler_params=_cparams(("arbitrary", "arbitrary")),
        name="s5",
    )(u, d3, bmat, cmat, tab)


def _s5_tables(lam_re, lam_im, log_step, b_re, b_im, cm_re, cm_im, seg):
    nb = S5_GROUPS // GROUPS_PER_BLOCK
    eye = jnp.eye(GROUPS_PER_BLOCK, dtype=F32)
    bmats, cmats, tabs = [], [], []
    for di in range(2):
        lr = lam_re[di].astype(F32)
        li = lam_im[di].astype(F32)
        delta = jnp.exp(log_step[di].astype(F32))[:, None]
        mag = jnp.exp(lr * delta)
        a_re = mag * jnp.cos(li * delta)
        a_im = mag * jnp.sin(li * delta)
        den = lr * lr + li * li
        num_re = a_re - 1.0
        f_re = (num_re * lr + a_im * li) / den
        f_im = (a_im * lr - num_re * li) / den
        br = b_re[di].astype(F32)
        bi = b_im[di].astype(F32)
        bbar_re = f_re[..., None] * br - f_im[..., None] * bi
        bbar_im = f_re[..., None] * bi + f_im[..., None] * br

        def blockdiag_in(bb):
            bb = bb.reshape(nb, GROUPS_PER_BLOCK, S5_STATE, S5_CH)
            m = jnp.einsum('bgpc,gh->bgchp', bb, eye)
            return m.reshape(nb, LANES, STATE_LANES)

        def blockdiag_out(cc):
            cc = cc.reshape(nb, GROUPS_PER_BLOCK, S5_CH, S5_STATE)
            m = jnp.einsum('bgcp,gh->bgphc', cc, eye)
            return m.reshape(nb, STATE_LANES, LANES)

        bmats.append(jnp.concatenate([blockdiag_in(bbar_re), blockdiag_in(bbar_im)], axis=2))
        cmats.append(jnp.concatenate([blockdiag_out(cm_re[di].astype(F32)),
                                      -blockdiag_out(cm_im[di].astype(F32))], axis=1))
        sr, si = a_re, a_im
        n = 1
        while n < seg:
            sr, si = sr * sr - si * si, 2.0 * sr * si
            n *= 2
        assert n == seg
        tabs.append(jnp.stack([a_re, a_im, sr, si]).reshape(4, nb, STATE_LANES)
                    .transpose(1, 0, 2))
    return jnp.stack(bmats).astype(BF16), jnp.stack(cmats).astype(BF16), jnp.stack(tabs)


def _post_kernel(a_ref, ys_ref, x_ref, mod_ref, gluw_ref, glub_ref, onw_ref, wout_ref, n2w_ref,
                 rw_ref, rb_ref, x1_ref, h2_ref, eidx_ref, gate_ref, rank_ref, cnt_ref, carry_s):
    first = (pl.program_id(0) == 0) & (pl.program_id(1) == 0)

    @pl.when(first)
    def _():
        carry_s[...] = jnp.zeros_like(carry_s)

    tm = x_ref.shape[1]
    y = ys_ref[0]
    y = 0.5 * y * (1.0 + jnp.tanh(math.sqrt(2.0 / math.pi) * (y + 0.044715 * (y * y * y))))
    g = jnp.dot(y.astype(BF16), gluw_ref[...], preferred_element_type=F32) + glub_ref[...]
    y = y * jax.nn.sigmoid(g)
    s = _rms(y) * onw_ref[...]
    mix = (jnp.dot(a_ref[0], wout_ref[0:W_ATTN, :], preferred_element_type=F32)
           + jnp.dot(s.astype(BF16), wout_ref[W_ATTN:, :], preferred_element_type=F32))
    g1 = mod_ref[0, 2:3, :]
    sh2 = mod_ref[0, 3:4, :]
    sc2 = mod_ref[0, 4:5, :]
    x1 = x_ref[0] + g1 * mix
    x1_ref[0] = x1
    h2 = (_rms(x1) * n2w_ref[...]) * (1.0 + sc2) + sh2
    h2_ref[0] = h2
    logits = jnp.dot(h2, rw_ref[...], preferred_element_type=F32,
                     precision=lax.Precision.HIGHEST) + rb_ref[...]
    lane = lax.broadcasted_iota(jnp.int32, (tm, LANES), 1)
    neg = jnp.float32(-jnp.inf)
    work = jnp.where(lane < N_EXPERTS, logits, neg)
    vals, idxs = [], []
    for _ in range(TOP_K):
        m = jnp.max(work, axis=-1, keepdims=True)
        idx = jnp.min(jnp.where(work == m, lane, LANES), axis=-1, keepdims=True)
        vals.append(m)
        idxs.append(idx)
        work = jnp.where(lane == idx, neg, work)
    es = [jnp.exp(v - vals[0]) for v in vals]
    den = es[0] + es[1] + es[2] + es[3]
    onehot = jnp.zeros((tm, LANES), F32)
    for idx in idxs:
        onehot = onehot + jnp.where(lane == idx, 1.0, 0.0)
    r_i = lax.broadcasted_iota(jnp.int32, (tm, tm), 0)
    c_i = lax.broadcasted_iota(jnp.int32, (tm, tm), 1)
    ltri = jnp.where(c_i < r_i, 1.0, 0.0).astype(BF16)
    before = jnp.dot(ltri, onehot.astype(BF16), preferred_element_type=F32) + carry_s[...]
    lane4 = lax.broadcasted_iota(jnp.int32, (tm, TOP_K), 1)
    e_out = jnp.zeros((tm, TOP_K), jnp.int32)
    g_out = jnp.zeros((tm, TOP_K), F32)
    r_out = jnp.zeros((tm, TOP_K), jnp.int32)
    for j in range(TOP_K):
        rk = jnp.sum(jnp.where(lane == idxs[j], before, 0.0), axis=-1, keepdims=True)
        e_out = jnp.where(lane4 == j, idxs[j], e_out)
        g_out = jnp.where(lane4 == j, es[j] / den, g_out)
        r_out = jnp.where(lane4 == j, rk.astype(jnp.int32), r_out)
    eidx_ref[0] = e_out
    gate_ref[0] = g_out
    rank_ref[0] = r_out
    carry_s[...] = carry_s[...] + jnp.sum(onehot, axis=0, keepdims=True)
    cnt_ref[...] = carry_s[...]


def _post(a, ys, x, mod3, gluw, glub, onw, wout, n2w, rw, rb, tm):
    B, S, D = x.shape
    c2 = lambda b, i: (0, 0)
    tok = lambda b, i: (b, i, 0)
    return pl.pallas_call(
        _post_kernel,
        grid=(B, S // tm),
        in_specs=[pl.BlockSpec((1, tm, W_ATTN), tok),
                  pl.BlockSpec((1, tm, W_S5), tok),
                  pl.BlockSpec((1, tm, D), tok),
                  pl.BlockSpec((1, 6, D), lambda b, i: (b, 0, 0)),
                  pl.BlockSpec((W_S5, W_S5), c2),
                  pl.BlockSpec((1, W_S5), c2),
                  pl.BlockSpec((1, W_S5), c2),
                  pl.BlockSpec((D, D), c2),
                  pl.BlockSpec((1, D), c2),
                  pl.BlockSpec((D, LANES), c2),
                  pl.BlockSpec((1, LANES), c2)],
        out_specs=[pl.BlockSpec((1, tm, D), tok),
                   pl.BlockSpec((1, tm, D), tok),
                   pl.BlockSpec((1, tm, TOP_K), tok),
                   pl.BlockSpec((1, tm, TOP_K), tok),
                   pl.BlockSpec((1, tm, TOP_K), tok),
                   pl.BlockSpec((1, LANES), c2)],
        out_shape=[jax.ShapeDtypeStruct((B, S, D), F32),
                   jax.ShapeDtypeStruct((B, S, D), F32),
                   jax.ShapeDtypeStruct((B, S, TOP_K), jnp.int32),
                   jax.ShapeDtypeStruct((B, S, TOP_K), F32),
                   jax.ShapeDtypeStruct((B, S, TOP_K), jnp.int32),
                   jax.ShapeDtypeStruct((1, LANES), F32)],
        scratch_shapes=[pltpu.VMEM((1, LANES), F32)],
        compiler_params=_cparams(("arbitrary", "arbitrary")),
        name="post",
    )(a, ys, x, mod3, gluw, glub, onw, wout, n2w, rw, rb)


def _dispatch_kernel(pend_ref, padded_ref, dest_ref, h2_ref, xs_hbm, zero_s, sem, zsem, *, tg):
    @pl.when(pl.program_id(0) == 0)
    def _():
        zero_s[...] = jnp.zeros_like(zero_s)

        def zero_copy(e):
            start = pl.multiple_of(pend_ref[e] - MOE_ROWS, MOE_ROWS)
            return pltpu.make_async_copy(zero_s, xs_hbm.at[pl.ds(start, MOE_ROWS)], zsem)

        for e in range(N_EXPERTS):
            @pl.when(padded_ref[e] > 0)
            def _():
                zero_copy(e).start()
        for e in range(N_EXPERTS):
            @pl.when(padded_ref[e] > 0)
            def _():
                zero_copy(e).wait()

    def row_copy(t, d):
        return pltpu.make_async_copy(h2_ref.at[pl.ds(t, 1)], xs_hbm.at[pl.ds(d, 1)], sem)

    def issue(t, _):
        for j in range(TOP_K):
            row_copy(t, dest_ref[0, 0, t * TOP_K + j]).start()
        return 0

    lax.fori_loop(0, tg, issue, 0, unroll=2)
    for _ in range(TOP_K):
        pltpu.make_async_copy(h2_ref, xs_hbm.at[pl.ds(0, tg)], sem).wait()


def _dispatch(pad_end, padded, dest3, h2, n_rows, tg):
    T, D = h2.shape
    kern = functools.partial(_dispatch_kernel, tg=tg)
    grid_spec = pltpu.PrefetchScalarGridSpec(
        num_scalar_prefetch=2,
        grid=(T // tg,),
        in_specs=[pl.BlockSpec((1, 1, tg * TOP_K), lambda i, pe, pd: (i, 0, 0),
                               memory_space=pltpu.SMEM),
                  pl.BlockSpec((tg, D), lambda i, pe, pd: (i, 0))],
        out_specs=pl.BlockSpec(memory_space=pl.ANY),
        scratch_shapes=[pltpu.VMEM((MOE_ROWS, D), F32), pltpu.SemaphoreType.DMA(()),
                        pltpu.SemaphoreType.DMA(())],
    )
    return pl.pallas_call(
        kern,
        grid_spec=grid_spec,
        out_shape=jax.ShapeDtypeStruct((n_rows, D), F32),
        compiler_params=_cparams(("arbitrary",)),
        name="dispatch",
    )(pad_end, padded, dest3, h2)


PERM_CHUNK = 2 * LANES


def _expert_kernel(be_ref, na_ref, x_ref, w1_ref, bg_ref, bl_ref, w2_ref, b2_ref, perm_ref, y_ref,
                   wg_s, wl_s, w2_s):
    i = pl.program_id(0)
    active = i < na_ref[0]
    prev = be_ref[jnp.maximum(i - 1, 0)]
    changed = active & ((i == 0) | (be_ref[i] != prev))

    @pl.when(changed)
    def _():
        perm = perm_ref[...]
        for j in range(2 * D_FF // PERM_CHUNK):
            chunk = w1_ref[0, :, j * PERM_CHUNK:(j + 1) * PERM_CHUNK].astype(BF16)
            sep = jnp.dot(chunk, perm, preferred_element_type=F32).astype(BF16)
            wg_s[:, j * LANES:(j + 1) * LANES] = sep[:, :LANES]
            wl_s[:, j * LANES:(j + 1) * LANES] = sep[:, LANES:]
        w2_s[...] = w2_ref[0].astype(BF16)

    @pl.when(active)
    def _():
        x = x_ref[...].astype(BF16)
        zg = jnp.dot(x, wg_s[...], preferred_element_type=F32) + bg_ref[0]
        zl = jnp.dot(x, wl_s[...], preferred_element_type=F32) + bl_ref[0]
        xg = jnp.minimum(zg, SWIGLU_LIMIT)
        xl = jnp.clip(zl, -SWIGLU_LIMIT, SWIGLU_LIMIT)
        act = xg * jax.nn.sigmoid(SWIGLU_ALPHA * xg) * (xl + 1.0)
        y_ref[...] = jnp.dot(act.astype(BF16), w2_s[...], preferred_element_type=F32) + b2_ref[0]


def _experts(block_e, n_active, xs, w1, bg, bl, w2, b2, perm):
    n_rows, D = xs.shape
    nblk = n_rows // MOE_ROWS

    def row_map(i, be, na):
        return (jnp.minimum(i, na[0] - 1), 0)

    def w_map(i, be, na):
        return (be[i], 0, 0)

    grid_spec = pltpu.PrefetchScalarGridSpec(
        num_scalar_prefetch=2,
        grid=(nblk,),
        in_specs=[pl.BlockSpec((MOE_ROWS, D), row_map),
                  pl.BlockSpec((1, D, 2 * D_FF), w_map),
                  pl.BlockSpec((1, 1, D_FF), w_map),
                  pl.BlockSpec((1, 1, D_FF), w_map),
                  pl.BlockSpec((1, D_FF, D), w_map),
                  pl.BlockSpec((1, 1, D), w_map),
                  pl.BlockSpec((PERM_CHUNK, PERM_CHUNK), lambda i, be, na: (0, 0))],
        out_specs=pl.BlockSpec((MOE_ROWS, D), row_map),
        scratch_shapes=[pltpu.VMEM((D, D_FF), BF16), pltpu.VMEM((D, D_FF), BF16),
                        pltpu.VMEM((D_FF, D), BF16)],
    )
    return pl.pallas_call(
        _expert_kernel,
        grid_spec=grid_spec,
        out_shape=jax.ShapeDtypeStruct((n_rows, D), F32),
        compiler_params=_cparams(("arbitrary",)),
        name="experts",
    )(block_e, n_active, xs, w1, bg, bl, w2, b2, perm)


def _combine_kernel(dest_ref, gate_ref, x1_ref, mod_ref, ys_hbm, o_ref, buf, sem, *, tc):
    def row_copy(t, j, d):
        return pltpu.make_async_copy(ys_hbm.at[pl.ds(d, 1)], buf.at[j, pl.ds(t, 1)], sem)

    def issue(t, _):
        for j in range(TOP_K):
            row_copy(t, j, dest_ref[0, 0, t * TOP_K + j]).start()
        return 0

    lax.fori_loop(0, tc, issue, 0, unroll=2)
    for j in range(TOP_K):
        pltpu.make_async_copy(ys_hbm.at[pl.ds(0, tc)], buf.at[j], sem).wait()
    gates = gate_ref[...]
    acc = gates[:, 0:1] * buf[0]
    for j in range(1, TOP_K):
        acc = acc + gates[:, j:j + 1] * buf[j]
    o_ref[...] = x1_ref[...] + mod_ref[0, 5:6, :] * acc


def _combine(dest3, gates, x1, mod3, ys, tc, tiles_per_batch):
    T, D = x1.shape
    kern = functools.partial(_combine_kernel, tc=tc)
    return pl.pallas_call(
        kern,
        grid=(T // tc,),
        in_specs=[pl.BlockSpec((1, 1, tc * TOP_K), lambda i: (i, 0, 0), memory_space=pltpu.SMEM),
                  pl.BlockSpec((tc, TOP_K), lambda i: (i, 0)),
                  pl.BlockSpec((tc, D), lambda i: (i, 0)),
                  pl.BlockSpec((1, 6, D), lambda i: (i // tiles_per_batch, 0, 0)),
                  pl.BlockSpec(memory_space=pl.ANY)],
        out_specs=pl.BlockSpec((tc, D), lambda i: (i, 0)),
        out_shape=jax.ShapeDtypeStruct((T, D), F32),
        scratch_shapes=[pltpu.VMEM((TOP_K, tc, D), F32), pltpu.SemaphoreType.DMA(())],
        compiler_params=_cparams(("arbitrary",)),
        name="combine",
    )(dest3, gates, x1, mod3, ys)


def kernel(x, c, positions, ada_w, ada_b, norm1_w, w_in, q_norm_w, k_norm_w, lambda_q1, lambda_k1,
           lambda_q2, lambda_k2, subln_w, s5_lambda_re, s5_lambda_im, s5_log_step, s5_b_re, s5_b_im,
           s5_cmat_re, s5_cmat_im, s5_d, s5_glu_w, s5_glu_b, s5_out_norm_w, w_out, norm2_w,
           router_w, router_b, mlp1_w, mlp1_b, mlp2_w, mlp2_b):
    B, S, D = x.shape
    T = B * S
    l = 0

    c_pad = jnp.pad(c, ((0, SUBLANES - B % SUBLANES if B % SUBLANES else 0), (0, 0)))
    mod = _ada(c_pad, ada_w[l], ada_b[l][None, :])[:B]
    mod3 = mod.reshape(B, 6, D)

    tm = min(512, S)
    inv_freq = ROPE_THETA ** (-jnp.arange(0, ROT_DIM, 2, dtype=F32) / ROT_DIM)
    d_in_head = jnp.arange(LANES) % QK_DIM
    invf = jnp.where(d_in_head < ROT_DIM, inv_freq[d_in_head % (ROT_DIM // 2)], 0.0)[None, :]
    pos3 = positions.astype(F32)[..., None]
    gmat = jnp.kron(jnp.eye(W_QK // QK_DIM, dtype=F32), jnp.ones((QK_DIM, QK_DIM), F32)).astype(BF16)
    q_bound = jnp.max(jnp.abs(q_norm_w[l])) * (math.log2(math.e) / math.sqrt(QK_DIM))
    k_bound = jnp.max(jnp.abs(k_norm_w[l]))
    shift = jnp.exp2(jnp.floor(0.5 * jnp.log2(jnp.maximum(q_bound, 1e-30)
                                              / jnp.maximum(k_bound, 1e-30))))
    qw = jnp.tile(q_norm_w[l], W_QK // QK_DIM)[None, :] / shift
    kw = jnp.tile(k_norm_w[l], W_QK // QK_DIM)[None, :] * shift
    qt, k, vt, u = _inproj(x, mod3, norm1_w[l][None, :], pos3, invf, w_in[l].astype(BF16),
                          qw, kw, gmat, tm)

    lam_p = jnp.stack([lambda_q1[l], lambda_k1[l], lambda_q2[l], lambda_k2[l]]).astype(F32)
    sw2 = jnp.tile(subln_w[l], 2)[None, :]
    a_out = _attn(lam_p, sw2, qt, k, vt, tq=min(512, S), tk=min(256, S // 2))

    s5_rows = min(256, S)
    bmat, cmat, tab = _s5_tables(s5_lambda_re[l], s5_lambda_im[l], s5_log_step[l], s5_b_re[l],
                                 s5_b_im[l], s5_cmat_re[l], s5_cmat_im[l], s5_rows // SUBLANES)
    d3 = s5_d[l].astype(F32).reshape(W_S5 // LANES, 1, LANES)
    y_s5 = _s5(u, d3, bmat, cmat, tab, rows=s5_rows)

    rw = jnp.pad(router_w[l], ((0, 0), (0, LANES - N_EXPERTS)))
    rb = jnp.pad(router_b[l], (0, LANES - N_EXPERTS))[None, :]
    tp = min(256, S)
    x1, h2, eidx, gates, rank, counts = _post(
        a_out, y_s5, x, mod3, s5_glu_w[l].astype(BF16), s5_glu_b[l][None, :],
        s5_out_norm_w[l][None, :], w_out[l].astype(BF16), norm2_w[l][None, :], rw, rb, tp)

    counts = counts[0, :N_EXPERTS].astype(jnp.int32)
    padded = ((counts + MOE_ROWS - 1) // MOE_ROWS) * MOE_ROWS
    pad_end = jnp.cumsum(padded)
    pad_start = pad_end - padded
    eflat = eidx.reshape(T * TOP_K)
    dest = pad_start[eflat] + rank.reshape(T * TOP_K)
    n_rows = T * TOP_K + N_EXPERTS * MOE_ROWS
    nblk = n_rows // MOE_ROWS
    blk_row = jnp.arange(nblk, dtype=jnp.int32)[:, None] * MOE_ROWS
    block_e = jnp.minimum(jnp.sum((blk_row >= pad_end[None, :]).astype(jnp.int32), axis=1),
                          N_EXPERTS - 1)
    n_active = (pad_end[-1] // MOE_ROWS).astype(jnp.int32)[None]

    tg = min(512, T)
    xs = _dispatch(pad_end.astype(jnp.int32), padded, dest.reshape(T // tg, 1, tg * TOP_K),
                   h2.reshape(T, D), n_rows, tg)

    bg = mlp1_b[l][:, None, 0::2]
    bl = mlp1_b[l][:, None, 1::2]
    src = jnp.arange(PERM_CHUNK)
    perm = (jnp.arange(PERM_CHUNK)[None, :] == ((src % 2) * LANES + src // 2)[:, None]).astype(BF16)
    ys = _experts(block_e, n_active, xs, mlp1_w[l], bg, bl, mlp2_w[l], mlp2_b[l][:, None, :], perm)

    tc = min(256, S)
    out = _combine(dest.reshape(T // tc, 1, tc * TOP_K), gates.reshape(T, TOP_K),
                   x1.reshape(T, D), mod3, ys, tc, S // tc)
    return out.reshape(B, S, D)
```

```python
import functools
import math

import jax
import jax.numpy as jnp
from jax import lax
from jax.experimental import pallas as pl
from jax.experimental.pallas import tpu as pltpu

F32 = jnp.float32
BF16 = jnp.bfloat16

D_MODEL = 1024
QK_DIM = 32
V_DIM = 64
N_HEADS = 8
W_QK = N_HEADS * 2 * QK_DIM
W_ATTN = N_HEADS * V_DIM
ROT_DIM = QK_DIM // 4
ROPE_THETA = 500000.0
S5_CH = 16
S5_STATE = 64
W_S5 = 512
S5_GROUPS = W_S5 // S5_CH
D_IN_PROJ = 2 * W_QK + W_ATTN + W_S5
N_EXPERTS = 32
TOP_K = 4
D_FF = D_MODEL
SWIGLU_ALPHA = 1.702
SWIGLU_LIMIT = 7.0
RMS_EPS = 1e-6
LAMBDA_INIT = 0.8 - 0.6 * math.exp(-0.3 * 0)

LANES = 128
SUBLANES = 8
VMEM_LIMIT = 56 * 1024 * 1024

GROUPS_PER_BLOCK = LANES // S5_CH
STATE_LANES = GROUPS_PER_BLOCK * S5_STATE
MOE_ROWS = 512
VT_ROWS = V_DIM + 16
ATTN_PAIRS_PER_ITER = 2
F8 = jnp.float8_e4m3fn
F8_MAX = 448.0
QK_CONTRACT = 4 * 2 * QK_DIM


def _cparams(sem):
    return pltpu.CompilerParams(dimension_semantics=sem, vmem_limit_bytes=VMEM_LIMIT)


def _rms(x, eps=RMS_EPS):
    return x * lax.rsqrt(jnp.mean(x * x, axis=-1, keepdims=True) + eps)


def _ada_kernel(c_ref, w_ref, b_ref, o_ref):
    c = c_ref[...]
    ca = c * jax.nn.sigmoid(c)
    o_ref[...] = jnp.dot(ca, w_ref[...], preferred_element_type=F32,
                         precision=lax.Precision.HIGHEST) + b_ref[...]


def _ada(c_pad, w, b):
    rows, d = c_pad.shape
    n = w.shape[1]
    tn = 1536
    return pl.pallas_call(
        _ada_kernel,
        grid=(n // tn,),
        in_specs=[pl.BlockSpec((rows, d), lambda j: (0, 0)),
                  pl.BlockSpec((d, tn), lambda j: (0, j)),
                  pl.BlockSpec((1, tn), lambda j: (0, j))],
        out_specs=pl.BlockSpec((rows, tn), lambda j: (0, j)),
        out_shape=jax.ShapeDtypeStruct((rows, n), F32),
        compiler_params=_cparams(("arbitrary",)),
        name="ada",
    )(c_pad, w, b)


def _inproj_kernel(x_ref, mod_ref, n1w_ref, pos_ref, invf_ref, win_ref, qw_ref, kw_ref, gm_ref,
                   qt_ref, k_ref, vt_ref, u_ref, *, q_scale):
    x = x_ref[0]
    tm = x.shape[0]
    sh1 = mod_ref[0, 0:1, :]
    sc1 = mod_ref[0, 1:2, :]
    h = (_rms(x) * n1w_ref[...]) * (1.0 + sc1) + sh1
    proj = jnp.dot(h.astype(BF16), win_ref[...], preferred_element_type=F32)

    ang = pos_ref[0] * invf_ref[...]
    cos = jnp.cos(ang)
    sin = jnp.sin(ang)
    d_in_head = lax.broadcasted_iota(jnp.int32, (1, LANES), 1) % QK_DIM
    half = ROT_DIM // 2
    s_lo = jnp.where(d_in_head < half, -sin, 0.0)
    s_hi = jnp.where((d_in_head >= half) & (d_in_head < ROT_DIM), sin, 0.0)
    reps = W_QK // LANES
    cos = jnp.concatenate([cos] * reps, axis=1)
    s_lo = jnp.concatenate([s_lo] * reps, axis=1)
    s_hi = jnp.concatenate([s_hi] * reps, axis=1)

    def qk_norm_rope(t, w_ref, scale):
        ssq = jnp.dot((t * t).astype(BF16), gm_ref[...], preferred_element_type=F32)
        tn = t * lax.rsqrt(ssq * (1.0 / QK_DIM) + RMS_EPS) * w_ref[...]
        r = (tn * cos + pltpu.roll(tn, W_QK - half, 1) * s_lo + pltpu.roll(tn, half, 1) * s_hi)
        return r * scale

    q = qk_norm_rope(proj[:, 0:W_QK], qw_ref, q_scale)
    k = qk_norm_rope(proj[:, W_QK:2 * W_QK], kw_ref, 1.0)
    v = proj[:, 2 * W_QK:2 * W_QK + W_ATTN]
    u_ref[0] = proj[:, 2 * W_QK + W_ATTN:]
    qt = q.T
    vt = v.T
    hw = 2 * QK_DIM

    def split8(x):
        xc = jnp.clip(x, -F8_MAX, F8_MAX)
        hi = xc.astype(F8)
        return hi, (xc - hi.astype(F32)).astype(F8)

    k_hi, k_lo = split8(k)
    map_of_row = (lax.broadcasted_iota(jnp.int32, (W_QK, tm), 0) // QK_DIM) % 2
    q_parts = [split8(jnp.where(map_of_row == c, qt, 0.0)) for c in range(2)]
    tail = jnp.where(lax.broadcasted_iota(jnp.int32, (VT_ROWS - V_DIM, tm), 0) == 0,
                     1.0, 0.0).astype(BF16)
    zk = jnp.zeros((tm, hw), F8)
    zq = jnp.zeros((hw, tm), F8)
    for hd in range(N_HEADS):
        cols = slice(hd * hw, (hd + 1) * hw)
        k_ref[0, hd, :, 0 * hw:1 * hw] = k_hi[:, cols]
        k_ref[0, hd, :, 1 * hw:2 * hw] = k_lo[:, cols]
        k_ref[0, hd, :, 2 * hw:3 * hw] = k_hi[:, cols]
        k_ref[0, hd, :, 3 * hw:4 * hw] = zk
        for c in range(2):
            q_hi, q_lo = q_parts[c]
            qt_ref[0, hd, c, 0 * hw:1 * hw, :] = q_hi[cols, :]
            qt_ref[0, hd, c, 1 * hw:2 * hw, :] = q_hi[cols, :]
            qt_ref[0, hd, c, 2 * hw:3 * hw, :] = q_lo[cols, :]
            qt_ref[0, hd, c, 3 * hw:4 * hw, :] = zq
        vt_ref[0, hd, 0:V_DIM, :] = vt[hd * V_DIM:(hd + 1) * V_DIM, :].astype(BF16)
        vt_ref[0, hd, V_DIM:, :] = tail


def _inproj(x, mod3, n1w, pos3, invf, win_bf, qw, kw, gmat, tm):
    B, S, D = x.shape
    q_scale = math.log2(math.e) / math.sqrt(QK_DIM)
    kern = functools.partial(_inproj_kernel, q_scale=q_scale)
    const2 = lambda b, i: (0, 0)
    return pl.pallas_call(
        kern,
        grid=(B, S // tm),
        in_specs=[pl.BlockSpec((1, tm, D), lambda b, i: (b, i, 0)),
                  pl.BlockSpec((1, 6, D), lambda b, i: (b, 0, 0)),
                  pl.BlockSpec((1, D), const2),
                  pl.BlockSpec((1, tm, 1), lambda b, i: (b, i, 0)),
                  pl.BlockSpec((1, LANES), const2),
                  pl.BlockSpec((D, D_IN_PROJ), const2),
                  pl.BlockSpec((1, W_QK), const2),
                  pl.BlockSpec((1, W_QK), const2),
                  pl.BlockSpec((W_QK, W_QK), const2)],
        out_specs=[pl.BlockSpec((1, N_HEADS, 2, QK_CONTRACT, tm), lambda b, i: (b, 0, 0, 0, i)),
                   pl.BlockSpec((1, N_HEADS, tm, QK_CONTRACT), lambda b, i: (b, 0, i, 0)),
                   pl.BlockSpec((1, N_HEADS, VT_ROWS, tm), lambda b, i: (b, 0, 0, i)),
                   pl.BlockSpec((1, tm, W_S5), lambda b, i: (b, i, 0))],
        out_shape=[jax.ShapeDtypeStruct((B, N_HEADS, 2, QK_CONTRACT, S), F8),
                   jax.ShapeDtypeStruct((B, N_HEADS, S, QK_CONTRACT), F8),
                   jax.ShapeDtypeStruct((B, N_HEADS, VT_ROWS, S), BF16),
                   jax.ShapeDtypeStruct((B, S, W_S5), F32)],
        compiler_params=_cparams(("arbitrary", "arbitrary")),
        name="inproj",
    )(x, mod3, n1w, pos3, invf, win_bf, qw, kw, gmat)


def _attn_kernel(lam_ref, sw_ref, qt_ref, k_ref, vt_ref, o_ref, st_a, st_b, pt_a, pt_b, acc_s, *,
                 tk, heads_per_step):
    S = k_ref.shape[2]
    tq = qt_ref.shape[4]
    lp = lam_ref[...]
    lam = (jnp.exp(jnp.sum(lp[0:1] * lp[1:2], axis=-1, keepdims=True))
           - jnp.exp(jnp.sum(lp[2:3] * lp[3:4], axis=-1, keepdims=True)) + LAMBDA_INIT)
    chains = [(hh, c) for hh in range(heads_per_step) for c in range(2)]
    qts = [qt_ref[0, hh, c] for hh, c in chains]

    def scores(kb, st_buf):
        off = pl.multiple_of(kb * tk, tk)
        for ci, ((hh, c), qt) in enumerate(zip(chains, qts)):
            st_buf[ci] = jnp.dot(k_ref[0, hh, pl.ds(off, tk), :], qt,
                                 preferred_element_type=F32).astype(BF16)

    def softmax(st_buf, pt_buf, ms):
        m_new, alphas = [], []
        for ci, m in enumerate(ms):
            mn = jnp.maximum(m, jnp.max(st_buf[ci], axis=0, keepdims=True).astype(F32))
            alphas.append(jnp.exp2(m - mn))
            pt_buf[ci] = jnp.exp2(st_buf[ci] - mn.astype(BF16))
            m_new.append(mn)
        return tuple(m_new), tuple(alphas)

    def accumulate(kb, pt_buf, alphas):
        off = pl.multiple_of(kb * tk, tk)
        for ci, ((hh, c), alpha) in enumerate(zip(chains, alphas)):
            pv = jnp.dot(vt_ref[0, hh, :, pl.ds(off, tk)], pt_buf[ci],
                         preferred_element_type=F32)
            acc_s[ci] = alpha * acc_s[ci] + pv

    def step(kb, st_cur, pt_cur, st_nxt, pt_prev, ms, alphas_prev):
        scores(kb + 1, st_nxt)
        accumulate(kb - 1, pt_prev, alphas_prev)
        return softmax(st_cur, pt_cur, ms)

    def pair(kb, carry):
        ms, alphas = step(kb, st_b, pt_b, st_a, pt_a, *carry)
        return step(kb + 1, st_a, pt_a, st_b, pt_b, ms, alphas)

    def body(jj, carry):
        kb = 2 * ATTN_PAIRS_PER_ITER * jj + 1
        for r in range(ATTN_PAIRS_PER_ITER):
            carry = pair(kb + 2 * r, carry)
        return carry

    nkb = S // tk
    assert nkb % 2 == 0
    n_pairs = nkb // 2 - 1
    acc_s[...] = jnp.zeros_like(acc_s)
    m0 = tuple(jnp.full((1, tq), -jnp.inf, F32) for _ in chains)
    scores(0, st_a)
    scores(1, st_b)
    carry = softmax(st_a, pt_a, m0)
    carry = lax.fori_loop(0, n_pairs // ATTN_PAIRS_PER_ITER, body, carry)
    for r in range(n_pairs - n_pairs % ATTN_PAIRS_PER_ITER, n_pairs):
        carry = pair(2 * r + 1, carry)
    ms, alphas = carry
    accumulate(nkb - 2, pt_a, alphas)
    ms, alphas = softmax(st_b, pt_b, ms)
    accumulate(nkb - 1, pt_b, alphas)
    res = tuple((None, acc_s[ci]) for ci in range(len(chains)))
    outs = []
    for hh in range(heads_per_step):
        a0 = res[2 * hh][1]
        a1 = res[2 * hh + 1][1]
        ot = (a0[:V_DIM, :] / a0[V_DIM:V_DIM + 1, :]
              - lam * (a1[:V_DIM, :] / a1[V_DIM:V_DIM + 1, :]))
        ot = ot * lax.rsqrt(jnp.mean(ot * ot, axis=0, keepdims=True) + RMS_EPS)
        outs.append(ot)
    o = jnp.concatenate(outs, axis=0).T
    o_ref[0] = (o * sw_ref[...] * (1.0 - LAMBDA_INIT)).astype(o_ref.dtype)


def _attn(lam_p, sw2, qt, k, vt, tq, tk):
    B, H, S, _ = k.shape
    hps = 2
    kern = functools.partial(_attn_kernel, tk=tk, heads_per_step=hps)
    return pl.pallas_call(
        kern,
        grid=(B, H // hps, S // tq),
        in_specs=[pl.BlockSpec((4, QK_DIM), lambda b, h, i: (0, 0)),
                  pl.BlockSpec((1, hps * V_DIM), lambda b, h, i: (0, 0)),
                  pl.BlockSpec((1, hps, 2, QK_CONTRACT, tq), lambda b, h, i: (b, h, 0, 0, i)),
                  pl.BlockSpec((1, hps, S, QK_CONTRACT), lambda b, h, i: (b, h, 0, 0)),
                  pl.BlockSpec((1, hps, VT_ROWS, S), lambda b, h, i: (b, h, 0, 0))],
        out_specs=pl.BlockSpec((1, tq, hps * V_DIM), lambda b, h, i: (b, i, h)),
        out_shape=jax.ShapeDtypeStruct((B, S, W_ATTN), BF16),
        scratch_shapes=[pltpu.VMEM((2 * hps, tk, tq), BF16), pltpu.VMEM((2 * hps, tk, tq), BF16),
                        pltpu.VMEM((2 * hps, tk, tq), BF16), pltpu.VMEM((2 * hps, tk, tq), BF16),
                        pltpu.VMEM((2 * hps, VT_ROWS, tq), F32)],
        compiler_params=_cparams(("arbitrary", "arbitrary", "arbitrary")),
        name="attn",
    )(lam_p, sw2, qt, k, vt)


def _s5_kernel(u_ref, d_ref, bm_ref, cm_ref, tab_ref, y_ref, xr_s, xi_s, *, rows):
    S = u_ref.shape[1]
    R = rows
    seg = R // SUBLANES
    nchunk = S // R
    P = STATE_LANES

    def cmul_add(ar, ai, xr, xi, br, bi):
        return ar * xr - ai * xi + br, ar * xi + ai * xr + bi

    for di in range(2):
        bm = bm_ref[di, 0]
        cm = cm_ref[di, 0]
        ar = tab_ref[di, 0, 0:1, :]
        ai = tab_ref[di, 0, 1:2, :]
        asr = tab_ref[di, 0, 2:3, :]
        asi = tab_ref[di, 0, 3:4, :]
        steps = list(range(seg)) if di == 0 else list(range(seg - 1, -1, -1))
        segs = list(range(SUBLANES)) if di == 0 else list(range(SUBLANES - 1, -1, -1))

        def chunk_body(ci, carry, di=di, bm=bm, cm=cm, ar=ar, ai=ai, asr=asr, asi=asi,
                       steps=steps, segs=segs):
            c = ci if di == 0 else nchunk - 1 - ci
            r0 = pl.multiple_of(c * R, R)
            u = jnp.concatenate([u_ref[0, pl.ds(r0 + j, SUBLANES, stride=seg), :]
                                 for j in range(seg)], axis=0)
            bu = jnp.dot(u.astype(BF16), bm, preferred_element_type=F32)
            ar8 = jnp.broadcast_to(ar, (SUBLANES, P))
            ai8 = jnp.broadcast_to(ai, (SUBLANES, P))
            zero8 = jnp.zeros((SUBLANES, P), F32)
            xr, xi = zero8, zero8
            for j in steps:
                xr, xi = cmul_add(ar8, ai8, xr, xi, bu[j * SUBLANES:(j + 1) * SUBLANES, :P],
                                  bu[j * SUBLANES:(j + 1) * SUBLANES, P:])
            cr, cim = carry
            ent_r, ent_i = [None] * SUBLANES, [None] * SUBLANES
            for s_ in segs:
                ent_r[s_], ent_i[s_] = cr, cim
                cr, cim = cmul_add(asr, asi, cr, cim, xr[s_:s_ + 1, :], xi[s_:s_ + 1, :])
            xr = jnp.concatenate(ent_r, axis=0)
            xi = jnp.concatenate(ent_i, axis=0)
            for j in steps:
                xr, xi = cmul_add(ar8, ai8, xr, xi, bu[j * SUBLANES:(j + 1) * SUBLANES, :P],
                                  bu[j * SUBLANES:(j + 1) * SUBLANES, P:])
                xr_s[j * SUBLANES:(j + 1) * SUBLANES, :] = xr
                xi_s[j * SUBLANES:(j + 1) * SUBLANES, :] = xi
            yc = (jnp.dot(xr_s[...].astype(BF16), cm[:P], preferred_element_type=F32)
                  + jnp.dot(xi_s[...].astype(BF16), cm[P:], preferred_element_type=F32))
            if di == 0:
                yc = yc + u * d_ref[0]
            for j in range(seg):
                rows_j = pl.ds(r0 + j, SUBLANES, stride=seg)
                piece = yc[j * SUBLANES:(j + 1) * SUBLANES, :]
                if di == 0:
                    y_ref[0, rows_j, :] = piece
                else:
                    y_ref[0, rows_j, :] = y_ref[0, rows_j, :] + piece
            return cr, cim

        zero = jnp.zeros((1, P), F32)
        lax.fori_loop(0, nchunk, chunk_body, (zero, zero))


def _s5(u, d3, bmat, cmat, tab, rows):
    B, S, W = u.shape
    nb = W // LANES
    P = STATE_LANES
    kern = functools.partial(_s5_kernel, rows=rows)
    return pl.pallas_call(
        kern,
        grid=(B, nb),
        in_specs=[pl.BlockSpec((1, S, LANES), lambda b, g: (b, 0, g)),
                  pl.BlockSpec((1, 1, LANES), lambda b, g: (g, 0, 0)),
                  pl.BlockSpec((2, 1, LANES, 2 * P), lambda b, g: (0, g, 0, 0)),
                  pl.BlockSpec((2, 1, 2 * P, LANES), lambda b, g: (0, g, 0, 0)),
                  pl.BlockSpec((2, 1, 4, P), lambda b, g: (0, g, 0, 0))],
        out_specs=pl.BlockSpec((1, S, LANES), lambda b, g: (b, 0, g)),
        out_shape=jax.ShapeDtypeStruct((B, S, W), F32),
        scratch_shapes=[pltpu.VMEM((rows, P), F32), pltpu.VMEM((rows, P), F32)],
        compiler_params=_cparams(("arbitrary", "arbitrary")),
        name="s5",
    )(u, d3, bmat, cmat, tab)


def _s5_tables(lam_re, lam_im, log_step, b_re, b_im, cm_re, cm_im, seg):
    nb = S5_GROUPS // GROUPS_PER_BLOCK
    eye = jnp.eye(GROUPS_PER_BLOCK, dtype=F32)
    bmats, cmats, tabs = [], [], []
    for di in range(2):
        lr = lam_re[di].astype(F32)
        li = lam_im[di].astype(F32)
        delta = jnp.exp(log_step[di].astype(F32))[:, None]
        mag = jnp.exp(lr * delta)
        a_re = mag * jnp.cos(li * delta)
        a_im = mag * jnp.sin(li * delta)
        den = lr * lr + li * li
        num_re = a_re - 1.0
        f_re = (num_re * lr + a_im * li) / den
        f_im = (a_im * lr - num_re * li) / den
        br = b_re[di].astype(F32)
        bi = b_im[di].astype(F32)
        bbar_re = f_re[..., None] * br - f_im[..., None] * bi
        bbar_im = f_re[..., None] * bi + f_im[..., None] * br

        def blockdiag_in(bb):
            bb = bb.reshape(nb, GROUPS_PER_BLOCK, S5_STATE, S5_CH)
            m = jnp.einsum('bgpc,gh->bgchp', bb, eye)
            return m.reshape(nb, LANES, STATE_LANES)

        def blockdiag_out(cc):
            cc = cc.reshape(nb, GROUPS_PER_BLOCK, S5_CH, S5_STATE)
            m = jnp.einsum('bgcp,gh->bgphc', cc, eye)
            return m.reshape(nb, STATE_LANES, LANES)

        bmats.append(jnp.concatenate([blockdiag_in(bbar_re), blockdiag_in(bbar_im)], axis=2))
        cmats.append(jnp.concatenate([blockdiag_out(cm_re[di].astype(F32)),
                                      -blockdiag_out(cm_im[di].astype(F32))], axis=1))
        sr, si = a_re, a_im
        n = 1
        while n < seg:
            sr, si = sr * sr - si * si, 2.0 * sr * si
            n *= 2
        assert n == seg
        tabs.append(jnp.stack([a_re, a_im, sr, si]).reshape(4, nb, STATE_LANES)
                    .transpose(1, 0, 2))
    return jnp.stack(bmats).astype(BF16), jnp.stack(cmats).astype(BF16), jnp.stack(tabs)


def _post_kernel(a_ref, ys_ref, x_ref, mod_ref, gluw_ref, glub_ref, onw_ref, wout_ref, n2w_ref,
                 rw_ref, rb_ref, x1_ref, h2_ref, eidx_ref, gate_ref, rank_ref, cnt_ref, carry_s):
    first = (pl.program_id(0) == 0) & (pl.program_id(1) == 0)

    @pl.when(first)
    def _():
        carry_s[...] = jnp.zeros_like(carry_s)

    tm = x_ref.shape[1]
    y = ys_ref[0]
    y = 0.5 * y * (1.0 + jnp.tanh(math.sqrt(2.0 / math.pi) * (y + 0.044715 * (y * y * y))))
    g = jnp.dot(y.astype(BF16), gluw_ref[...], preferred_element_type=F32) + glub_ref[...]
    y = y * jax.nn.sigmoid(g)
    s = _rms(y) * onw_ref[...]
    mix = (jnp.dot(a_ref[0], wout_ref[0:W_ATTN, :], preferred_element_type=F32)
           + jnp.dot(s.astype(BF16), wout_ref[W_ATTN:, :], preferred_element_type=F32))
    g1 = mod_ref[0, 2:3, :]
    sh2 = mod_ref[0, 3:4, :]
    sc2 = mod_ref[0, 4:5, :]
    x1 = x_ref[0] + g1 * mix
    x1_ref[0] = x1
    h2 = (_rms(x1) * n2w_ref[...]) * (1.0 + sc2) + sh2
    h2_ref[0] = h2
    logits = jnp.dot(h2, rw_ref[...], preferred_element_type=F32,
                     precision=lax.Precision.HIGHEST) + rb_ref[...]
    lane = lax.broadcasted_iota(jnp.int32, (tm, LANES), 1)
    neg = jnp.float32(-jnp.inf)
    work = jnp.where(lane < N_EXPERTS, logits, neg)
    vals, idxs = [], []
    for _ in range(TOP_K):
        m = jnp.max(work, axis=-1, keepdims=True)
        idx = jnp.min(jnp.where(work == m, lane, LANES), axis=-1, keepdims=True)
        vals.append(m)
        idxs.append(idx)
        work = jnp.where(lane == idx, neg, work)
    es = [jnp.exp(v - vals[0]) for v in vals]
    den = es[0] + es[1] + es[2] + es[3]
    onehot = jnp.zeros((tm, LANES), F32)
    for idx in idxs:
        onehot = onehot + jnp.where(lane == idx, 1.0, 0.0)
    r_i = lax.broadcasted_iota(jnp.int32, (tm, tm), 0)
    c_i = lax.broadcasted_iota(jnp.int32, (tm, tm), 1)
    ltri = jnp.where(c_i < r_i, 1.0, 0.0).astype(BF16)
    before = jnp.dot(ltri, onehot.astype(BF16), preferred_element_type=F32) + carry_s[...]
    lane4 = lax.broadcasted_iota(jnp.int32, (tm, TOP_K), 1)
    e_out = jnp.zeros((tm, TOP_K), jnp.int32)
    g_out = jnp.zeros((tm, TOP_K), F32)
    r_out = jnp.zeros((tm, TOP_K), jnp.int32)
    for j in range(TOP_K):
        rk = jnp.sum(jnp.where(lane == idxs[j], before, 0.0), axis=-1, keepdims=True)
        e_out = jnp.where(lane4 == j, idxs[j], e_out)
        g_out = jnp.where(lane4 == j, es[j] / den, g_out)
        r_out = jnp.where(lane4 == j, rk.astype(jnp.int32), r_out)
    eidx_ref[0] = e_out
    gate_ref[0] = g_out
    rank_ref[0] = r_out
    carry_s[...] = carry_s[...] + jnp.sum(onehot, axis=0, keepdims=True)
    cnt_ref[...] = carry_s[...]


def _post(a, ys, x, mod3, gluw, glub, onw, wout, n2w, rw, rb, tm):
    B, S, D = x.shape
    c2 = lambda b, i: (0, 0)
    tok = lambda b, i: (b, i, 0)
    return pl.pallas_call(
        _post_kernel,
        grid=(B, S // tm),
        in_specs=[pl.BlockSpec((1, tm, W_ATTN), tok),
                  pl.BlockSpec((1, tm, W_S5), tok),
                  pl.BlockSpec((1, tm, D), tok),
                  pl.BlockSpec((1, 6, D), lambda b, i: (b, 0, 0)),
                  pl.BlockSpec((W_S5, W_S5), c2),
                  pl.BlockSpec((1, W_S5), c2),
                  pl.BlockSpec((1, W_S5), c2),
                  pl.BlockSpec((D, D), c2),
                  pl.BlockSpec((1, D), c2),
                  pl.BlockSpec((D, LANES), c2),
                  pl.BlockSpec((1, LANES), c2)],
        out_specs=[pl.BlockSpec((1, tm, D), tok),
                   pl.BlockSpec((1, tm, D), tok),
                   pl.BlockSpec((1, tm, TOP_K), tok),
                   pl.BlockSpec((1, tm, TOP_K), tok),
                   pl.BlockSpec((1, tm, TOP_K), tok),
                   pl.BlockSpec((1, LANES), c2)],
        out_shape=[jax.ShapeDtypeStruct((B, S, D), F32),
                   jax.ShapeDtypeStruct((B, S, D), F32),
                   jax.ShapeDtypeStruct((B, S, TOP_K), jnp.int32),
                   jax.ShapeDtypeStruct((B, S, TOP_K), F32),
                   jax.ShapeDtypeStruct((B, S, TOP_K), jnp.int32),
                   jax.ShapeDtypeStruct((1, LANES), F32)],
        scratch_shapes=[pltpu.VMEM((1, LANES), F32)],
        compiler_params=_cparams(("arbitrary", "arbitrary")),
        name="post",
    )(a, ys, x, mod3, gluw, glub, onw, wout, n2w, rw, rb)


def _dispatch_kernel(pend_ref, padded_ref, dest_ref, h2_ref, xs_hbm, zero_s, sem, zsem, *, tg):
    @pl.when(pl.program_id(0) == 0)
    def _():
        zero_s[...] = jnp.zeros_like(zero_s)

        def zero_copy(e):
            start = pl.multiple_of(pend_ref[e] - MOE_ROWS, MOE_ROWS)
            return pltpu.make_async_copy(zero_s, xs_hbm.at[pl.ds(start, MOE_ROWS)], zsem)

        for e in range(N_EXPERTS):
            @pl.when(padded_ref[e] > 0)
            def _():
                zero_copy(e).start()
        for e in range(N_EXPERTS):
            @pl.when(padded_ref[e] > 0)
            def _():
                zero_copy(e).wait()

    def row_copy(t, d):
        return pltpu.make_async_copy(h2_ref.at[pl.ds(t, 1)], xs_hbm.at[pl.ds(d, 1)], sem)

    def issue(t, _):
        for j in range(TOP_K):
            row_copy(t, dest_ref[0, 0, t * TOP_K + j]).start()
        return 0

    lax.fori_loop(0, tg, issue, 0, unroll=2)
    for _ in range(TOP_K):
        pltpu.make_async_copy(h2_ref, xs_hbm.at[pl.ds(0, tg)], sem).wait()


def _dispatch(pad_end, padded, dest3, h2, n_rows, tg):
    T, D = h2.shape
    kern = functools.partial(_dispatch_kernel, tg=tg)
    grid_spec = pltpu.PrefetchScalarGridSpec(
        num_scalar_prefetch=2,
        grid=(T // tg,),
        in_specs=[pl.BlockSpec((1, 1, tg * TOP_K), lambda i, pe, pd: (i, 0, 0),
                               memory_space=pltpu.SMEM),
                  pl.BlockSpec((tg, D), lambda i, pe, pd: (i, 0))],
        out_specs=pl.BlockSpec(memory_space=pl.ANY),
        scratch_shapes=[pltpu.VMEM((MOE_ROWS, D), F32), pltpu.SemaphoreType.DMA(()),
                        pltpu.SemaphoreType.DMA(())],
    )
    return pl.pallas_call(
        kern,
        grid_spec=grid_spec,
        out_shape=jax.ShapeDtypeStruct((n_rows, D), F32),
        compiler_params=_cparams(("arbitrary",)),
        name="dispatch",
    )(pad_end, padded, dest3, h2)


PERM_CHUNK = 2 * LANES


def _expert_kernel(be_ref, na_ref, x_ref, w1_ref, bg_ref, bl_ref, w2_ref, b2_ref, perm_ref, y_ref,
                   wg_s, wl_s, w2_s):
    i = pl.program_id(0)
    active = i < na_ref[0]
    prev = be_ref[jnp.maximum(i - 1, 0)]
    changed = active & ((i == 0) | (be_ref[i] != prev))

    @pl.when(changed)
    def _():
        perm = perm_ref[...]
        for j in range(2 * D_FF // PERM_CHUNK):
            chunk = w1_ref[0, :, j * PERM_CHUNK:(j + 1) * PERM_CHUNK].astype(BF16)
            sep = jnp.dot(chunk, perm, preferred_element_type=F32).astype(BF16)
            wg_s[:, j * LANES:(j + 1) * LANES] = sep[:, :LANES]
            wl_s[:, j * LANES:(j + 1) * LANES] = sep[:, LANES:]
        w2_s[...] = w2_ref[0].astype(BF16)

    @pl.when(active)
    def _():
        x = x_ref[...].astype(BF16)
        zg = jnp.dot(x, wg_s[...], preferred_element_type=F32) + bg_ref[0]
        zl = jnp.dot(x, wl_s[...], preferred_element_type=F32) + bl_ref[0]
        xg = jnp.minimum(zg, SWIGLU_LIMIT)
        xl = jnp.clip(zl, -SWIGLU_LIMIT, SWIGLU_LIMIT)
        act = xg * jax.nn.sigmoid(SWIGLU_ALPHA * xg) * (xl + 1.0)
        y_ref[...] = jnp.dot(act.astype(BF16), w2_s[...], preferred_element_type=F32) + b2_ref[0]


def _experts(block_e, n_active, xs, w1, bg, bl, w2, b2, perm):
    n_rows, D = xs.shape
    nblk = n_rows // MOE_ROWS

    def row_map(i, be, na):
        return (jnp.minimum(i, na[0] - 1), 0)

    def w_map(i, be, na):
        return (be[i], 0, 0)

    grid_spec = pltpu.PrefetchScalarGridSpec(
        num_scalar_prefetch=2,
        grid=(nblk,),
        in_specs=[pl.BlockSpec((MOE_ROWS, D), row_map),
                  pl.BlockSpec((1, D, 2 * D_FF), w_map),
                  pl.BlockSpec((1, 1, D_FF), w_map),
                  pl.BlockSpec((1, 1, D_FF), w_map),
                  pl.BlockSpec((1, D_FF, D), w_map),
                  pl.BlockSpec((1, 1, D), w_map),
                  pl.BlockSpec((PERM_CHUNK, PERM_CHUNK), lambda i, be, na: (0, 0))],
        out_specs=pl.BlockSpec((MOE_ROWS, D), row_map),
        scratch_shapes=[pltpu.VMEM((D, D_FF), BF16), pltpu.VMEM((D, D_FF), BF16),
                        pltpu.VMEM((D_FF, D), BF16)],
    )
    return pl.pallas_call(
        _expert_kernel,
        grid_spec=grid_spec,
        out_shape=jax.ShapeDtypeStruct((n_rows, D), F32),
        compiler_params=_cparams(("arbitrary",)),
        name="experts",
    )(block_e, n_active, xs, w1, bg, bl, w2, b2, perm)


def _combine_kernel(dest_ref, destn_ref, gate_ref, x1_ref, mod_ref, ys_hbm, o_ref, buf, sem, *, tc):
    i = pl.program_id(0)
    slot = i % 2

    def issue_all(idx_ref, sl):
        def issue(t, _):
            for j in range(TOP_K):
                d = idx_ref[0, 0, t * TOP_K + j]
                pltpu.make_async_copy(ys_hbm.at[pl.ds(d, 1)], buf.at[sl, j, pl.ds(t, 1)],
                                      sem.at[sl]).start()
            return 0

        lax.fori_loop(0, tc, issue, 0, unroll=2)

    @pl.when(i == 0)
    def _():
        issue_all(dest_ref, 0)

    @pl.when(i + 1 < pl.num_programs(0))
    def _():
        issue_all(destn_ref, 1 - slot)

    for j in range(TOP_K):
        pltpu.make_async_copy(ys_hbm.at[pl.ds(0, tc)], buf.at[slot, j], sem.at[slot]).wait()
    gates = gate_ref[...]
    acc = gates[:, 0:1] * buf[slot, 0]
    for j in range(1, TOP_K):
        acc = acc + gates[:, j:j + 1] * buf[slot, j]
    o_ref[...] = x1_ref[...] + mod_ref[0, 5:6, :] * acc


def _combine(dest3, gates, x1, mod3, ys, tc, tiles_per_batch):
    T, D = x1.shape
    n_tiles = T // tc
    kern = functools.partial(_combine_kernel, tc=tc)
    return pl.pallas_call(
        kern,
        grid=(n_tiles,),
        in_specs=[pl.BlockSpec((1, 1, tc * TOP_K), lambda i: (i, 0, 0), memory_space=pltpu.SMEM),
                  pl.BlockSpec((1, 1, tc * TOP_K), lambda i: (jnp.minimum(i + 1, n_tiles - 1), 0, 0),
                               memory_space=pltpu.SMEM),
                  pl.BlockSpec((tc, TOP_K), lambda i: (i, 0)),
                  pl.BlockSpec((tc, D), lambda i: (i, 0)),
                  pl.BlockSpec((1, 6, D), lambda i: (i // tiles_per_batch, 0, 0)),
                  pl.BlockSpec(memory_space=pl.ANY)],
        out_specs=pl.BlockSpec((tc, D), lambda i: (i, 0)),
        out_shape=jax.ShapeDtypeStruct((T, D), F32),
        scratch_shapes=[pltpu.VMEM((2, TOP_K, tc, D), F32), pltpu.SemaphoreType.DMA((2,))],
        compiler_params=_cparams(("arbitrary",)),
        name="combine",
    )(dest3, dest3, gates, x1, mod3, ys)


def kernel(x, c, positions, ada_w, ada_b, norm1_w, w_in, q_norm_w, k_norm_w, lambda_q1, lambda_k1,
           lambda_q2, lambda_k2, subln_w, s5_lambda_re, s5_lambda_im, s5_log_step, s5_b_re, s5_b_im,
           s5_cmat_re, s5_cmat_im, s5_d, s5_glu_w, s5_glu_b, s5_out_norm_w, w_out, norm2_w,
           router_w, router_b, mlp1_w, mlp1_b, mlp2_w, mlp2_b):
    B, S, D = x.shape
    T = B * S
    l = 0

    c_pad = jnp.pad(c, ((0, SUBLANES - B % SUBLANES if B % SUBLANES else 0), (0, 0)))
    mod = _ada(c_pad, ada_w[l], ada_b[l][None, :])[:B]
    mod3 = mod.reshape(B, 6, D)

    tm = min(512, S)
    inv_freq = ROPE_THETA ** (-jnp.arange(0, ROT_DIM, 2, dtype=F32) / ROT_DIM)
    d_in_head = jnp.arange(LANES) % QK_DIM
    invf = jnp.where(d_in_head < ROT_DIM, inv_freq[d_in_head % (ROT_DIM // 2)], 0.0)[None, :]
    pos3 = positions.astype(F32)[..., None]
    gmat = jnp.kron(jnp.eye(W_QK // QK_DIM, dtype=F32), jnp.ones((QK_DIM, QK_DIM), F32)).astype(BF16)
    q_bound = jnp.max(jnp.abs(q_norm_w[l])) * (math.log2(math.e) / math.sqrt(QK_DIM))
    k_bound = jnp.max(jnp.abs(k_norm_w[l]))
    shift = jnp.exp2(jnp.floor(0.5 * jnp.log2(jnp.maximum(q_bound, 1e-30)
                                              / jnp.maximum(k_bound, 1e-30))))
    qw = jnp.tile(q_norm_w[l], W_QK // QK_DIM)[None, :] / shift
    kw = jnp.tile(k_norm_w[l], W_QK // QK_DIM)[None, :] * shift
    qt, k, vt, u = _inproj(x, mod3, norm1_w[l][None, :], pos3, invf, w_in[l].astype(BF16),
                          qw, kw, gmat, tm)

    lam_p = jnp.stack([lambda_q1[l], lambda_k1[l], lambda_q2[l], lambda_k2[l]]).astype(F32)
    sw2 = jnp.tile(subln_w[l], 2)[None, :]
    a_out = _attn(lam_p, sw2, qt, k, vt, tq=min(512, S), tk=min(256, S // 2))

    s5_rows = min(256, S)
    bmat, cmat, tab = _s5_tables(s5_lambda_re[l], s5_lambda_im[l], s5_log_step[l], s5_b_re[l],
                                 s5_b_im[l], s5_cmat_re[l], s5_cmat_im[l], s5_rows // SUBLANES)
    d3 = s5_d[l].astype(F32).reshape(W_S5 // LANES, 1, LANES)
    y_s5 = _s5(u, d3, bmat, cmat, tab, rows=s5_rows)

    rw = jnp.pad(router_w[l], ((0, 0), (0, LANES - N_EXPERTS)))
    rb = jnp.pad(router_b[l], (0, LANES - N_EXPERTS))[None, :]
    tp = min(256, S)
    x1, h2, eidx, gates, rank, counts = _post(
        a_out, y_s5, x, mod3, s5_glu_w[l].astype(BF16), s5_glu_b[l][None, :],
        s5_out_norm_w[l][None, :], w_out[l].astype(BF16), norm2_w[l][None, :], rw, rb, tp)

    counts = counts[0, :N_EXPERTS].astype(jnp.int32)
    padded = ((counts + MOE_ROWS - 1) // MOE_ROWS) * MOE_ROWS
    pad_end = jnp.cumsum(padded)
    pad_start = pad_end - padded
    eflat = eidx.reshape(T * TOP_K)
    dest = pad_start[eflat] + rank.reshape(T * TOP_K)
    n_rows = T * TOP_K + N_EXPERTS * MOE_ROWS
    nblk = n_rows // MOE_ROWS
    blk_row = jnp.arange(nblk, dtype=jnp.int32)[:, None] * MOE_ROWS
    block_e = jnp.minimum(jnp.sum((blk_row >= pad_end[None, :]).astype(jnp.int32), axis=1),
                          N_EXPERTS - 1)
    n_active = (pad_end[-1] // MOE_ROWS).astype(jnp.int32)[None]

    tg = min(512, T)
    xs = _dispatch(pad_end.astype(jnp.int32), padded, dest.reshape(T // tg, 1, tg * TOP_K),
                   h2.reshape(T, D), n_rows, tg)

    bg = mlp1_b[l][:, None, 0::2]
    bl = mlp1_b[l][:, None, 1::2]
    src = jnp.arange(PERM_CHUNK)
    perm = (jnp.arange(PERM_CHUNK)[None, :] == ((src % 2) * LANES + src // 2)[:, None]).astype(BF16)
    ys = _experts(block_e, n_active, xs, mlp1_w[l], bg, bl, mlp2_w[l], mlp2_b[l][:, None, :], perm)

    tc = min(256, S)
    out = _combine(dest.reshape(T // tc, 1, tc * TOP_K), gates.reshape(T, TOP_K),
                   x1.reshape(T, D), mod3, ys, tc, S // tc)
    return out.reshape(B, S, D)
```

```python
import functools
import math

import jax
import jax.numpy as jnp
from jax import lax
from jax.experimental import pallas as pl
from jax.experimental.pallas import tpu as pltpu

F32 = jnp.float32
BF16 = jnp.bfloat16

D_MODEL = 1024
QK_DIM = 32
V_DIM = 64
N_HEADS = 8
W_QK = N_HEADS * 2 * QK_DIM
W_ATTN = N_HEADS * V_DIM
ROT_DIM = QK_DIM // 4
ROPE_THETA = 500000.0
S5_CH = 16
S5_STATE = 64
W_S5 = 512
S5_GROUPS = W_S5 // S5_CH
D_IN_PROJ = 2 * W_QK + W_ATTN + W_S5
N_EXPERTS = 32
TOP_K = 4
D_FF = D_MODEL
SWIGLU_ALPHA = 1.702
SWIGLU_LIMIT = 7.0
RMS_EPS = 1e-6
LAMBDA_INIT = 0.8 - 0.6 * math.exp(-0.3 * 0)

LANES = 128
SUBLANES = 8
VMEM_LIMIT = 56 * 1024 * 1024

GROUPS_PER_BLOCK = LANES // S5_CH
STATE_LANES = GROUPS_PER_BLOCK * S5_STATE
MOE_ROWS = 512
VT_ROWS = V_DIM + 16
ATTN_PAIRS_PER_ITER = 2
EXP_SLICES = 8
F8 = jnp.float8_e4m3fn
F8_MAX = 448.0
QK_CONTRACT = 4 * 2 * QK_DIM


def _cparams(sem):
    return pltpu.CompilerParams(dimension_semantics=sem, vmem_limit_bytes=VMEM_LIMIT)


def _rms(x, eps=RMS_EPS):
    return x * lax.rsqrt(jnp.mean(x * x, axis=-1, keepdims=True) + eps)


def _ada_kernel(c_ref, w_ref, b_ref, o_ref):
    c = c_ref[...]
    ca = c * jax.nn.sigmoid(c)
    o_ref[...] = jnp.dot(ca, w_ref[...], preferred_element_type=F32,
                         precision=lax.Precision.HIGHEST) + b_ref[...]


def _ada(c_pad, w, b):
    rows, d = c_pad.shape
    n = w.shape[1]
    tn = 1536
    return pl.pallas_call(
        _ada_kernel,
        grid=(n // tn,),
        in_specs=[pl.BlockSpec((rows, d), lambda j: (0, 0)),
                  pl.BlockSpec((d, tn), lambda j: (0, j)),
                  pl.BlockSpec((1, tn), lambda j: (0, j))],
        out_specs=pl.BlockSpec((rows, tn), lambda j: (0, j)),
        out_shape=jax.ShapeDtypeStruct((rows, n), F32),
        compiler_params=_cparams(("arbitrary",)),
        name="ada",
    )(c_pad, w, b)


def _inproj_kernel(x_ref, mod_ref, n1w_ref, pos_ref, invf_ref, win_ref, qw_ref, kw_ref, gm_ref,
                   qt_ref, k_ref, vt_ref, u_ref, *, q_scale):
    x = x_ref[0]
    tm = x.shape[0]
    sh1 = mod_ref[0, 0:1, :]
    sc1 = mod_ref[0, 1:2, :]
    h = (_rms(x) * n1w_ref[...]) * (1.0 + sc1) + sh1
    proj = jnp.dot(h.astype(BF16), win_ref[...], preferred_element_type=F32)

    ang = pos_ref[0] * invf_ref[...]
    cos = jnp.cos(ang)
    sin = jnp.sin(ang)
    d_in_head = lax.broadcasted_iota(jnp.int32, (1, LANES), 1) % QK_DIM
    half = ROT_DIM // 2
    s_lo = jnp.where(d_in_head < half, -sin, 0.0)
    s_hi = jnp.where((d_in_head >= half) & (d_in_head < ROT_DIM), sin, 0.0)
    reps = W_QK // LANES
    cos = jnp.concatenate([cos] * reps, axis=1)
    s_lo = jnp.concatenate([s_lo] * reps, axis=1)
    s_hi = jnp.concatenate([s_hi] * reps, axis=1)

    def qk_norm_rope(t, w_ref, scale):
        ssq = jnp.dot((t * t).astype(BF16), gm_ref[...], preferred_element_type=F32)
        tn = t * lax.rsqrt(ssq * (1.0 / QK_DIM) + RMS_EPS) * w_ref[...]
        r = (tn * cos + pltpu.roll(tn, W_QK - half, 1) * s_lo + pltpu.roll(tn, half, 1) * s_hi)
        return r * scale

    q = qk_norm_rope(proj[:, 0:W_QK], qw_ref, q_scale)
    k = qk_norm_rope(proj[:, W_QK:2 * W_QK], kw_ref, 1.0)
    v = proj[:, 2 * W_QK:2 * W_QK + W_ATTN]
    u_ref[0] = proj[:, 2 * W_QK + W_ATTN:]
    qt = q.T
    vt = v.T
    hw = 2 * QK_DIM

    def split8(x):
        xc = jnp.clip(x, -F8_MAX, F8_MAX)
        hi = xc.astype(F8)
        return hi, (xc - hi.astype(F32)).astype(F8)

    k_hi, k_lo = split8(k)
    map_of_row = (lax.broadcasted_iota(jnp.int32, (W_QK, tm), 0) // QK_DIM) % 2
    q_parts = [split8(jnp.where(map_of_row == c, qt, 0.0)) for c in range(2)]
    tail = jnp.where(lax.broadcasted_iota(jnp.int32, (VT_ROWS - V_DIM, tm), 0) == 0,
                     1.0, 0.0).astype(BF16)
    zk = jnp.zeros((tm, hw), F8)
    zq = jnp.zeros((hw, tm), F8)
    for hd in range(N_HEADS):
        cols = slice(hd * hw, (hd + 1) * hw)
        k_ref[0, hd, :, 0 * hw:1 * hw] = k_hi[:, cols]
        k_ref[0, hd, :, 1 * hw:2 * hw] = k_lo[:, cols]
        k_ref[0, hd, :, 2 * hw:3 * hw] = k_hi[:, cols]
        k_ref[0, hd, :, 3 * hw:4 * hw] = zk
        for c in range(2):
            q_hi, q_lo = q_parts[c]
            qt_ref[0, hd, c, 0 * hw:1 * hw, :] = q_hi[cols, :]
            qt_ref[0, hd, c, 1 * hw:2 * hw, :] = q_hi[cols, :]
            qt_ref[0, hd, c, 2 * hw:3 * hw, :] = q_lo[cols, :]
            qt_ref[0, hd, c, 3 * hw:4 * hw, :] = zq
        vt_ref[0, hd, 0:V_DIM, :] = vt[hd * V_DIM:(hd + 1) * V_DIM, :].astype(BF16)
        vt_ref[0, hd, V_DIM:, :] = tail


def _inproj(x, mod3, n1w, pos3, invf, win_bf, qw, kw, gmat, tm):
    B, S, D = x.shape
    q_scale = math.log2(math.e) / math.sqrt(QK_DIM)
    kern = functools.partial(_inproj_kernel, q_scale=q_scale)
    const2 = lambda b, i: (0, 0)
    return pl.pallas_call(
        kern,
        grid=(B, S // tm),
        in_specs=[pl.BlockSpec((1, tm, D), lambda b, i: (b, i, 0)),
                  pl.BlockSpec((1, 6, D), lambda b, i: (b, 0, 0)),
                  pl.BlockSpec((1, D), const2),
                  pl.BlockSpec((1, tm, 1), lambda b, i: (b, i, 0)),
                  pl.BlockSpec((1, LANES), const2),
                  pl.BlockSpec((D, D_IN_PROJ), const2),
                  pl.BlockSpec((1, W_QK), const2),
                  pl.BlockSpec((1, W_QK), const2),
                  pl.BlockSpec((W_QK, W_QK), const2)],
        out_specs=[pl.BlockSpec((1, N_HEADS, 2, QK_CONTRACT, tm), lambda b, i: (b, 0, 0, 0, i)),
                   pl.BlockSpec((1, N_HEADS, tm, QK_CONTRACT), lambda b, i: (b, 0, i, 0)),
                   pl.BlockSpec((1, N_HEADS, VT_ROWS, tm), lambda b, i: (b, 0, 0, i)),
                   pl.BlockSpec((1, tm, W_S5), lambda b, i: (b, i, 0))],
        out_shape=[jax.ShapeDtypeStruct((B, N_HEADS, 2, QK_CONTRACT, S), F8),
                   jax.ShapeDtypeStruct((B, N_HEADS, S, QK_CONTRACT), F8),
                   jax.ShapeDtypeStruct((B, N_HEADS, VT_ROWS, S), BF16),
                   jax.ShapeDtypeStruct((B, S, W_S5), F32)],
        compiler_params=_cparams(("arbitrary", "arbitrary")),
        name="inproj",
    )(x, mod3, n1w, pos3, invf, win_bf, qw, kw, gmat)


def _attn_kernel(lam_ref, sw_ref, qt_ref, k_ref, vt_ref, o_ref, st_a, st_b, pt_a, pt_b, acc_s, *,
                 tk, heads_per_step):
    S = k_ref.shape[2]
    tq = qt_ref.shape[4]
    lp = lam_ref[...]
    lam = (jnp.exp(jnp.sum(lp[0:1] * lp[1:2], axis=-1, keepdims=True))
           - jnp.exp(jnp.sum(lp[2:3] * lp[3:4], axis=-1, keepdims=True)) + LAMBDA_INIT)
    chains = [(hh, c) for hh in range(heads_per_step) for c in range(2)]
    qts = [qt_ref[0, hh, c] for hh, c in chains]

    def scores(kb, st_buf):
        off = pl.multiple_of(kb * tk, tk)
        for ci, ((hh, c), qt) in enumerate(zip(chains, qts)):
            st_buf[ci] = jnp.dot(k_ref[0, hh, pl.ds(off, tk), :], qt,
                                 preferred_element_type=F32).astype(BF16)

    def softmax(st_buf, pt_buf, ms):
        m_new, alphas = [], []
        for ci, m in enumerate(ms):
            mn = jnp.maximum(m, jnp.max(st_buf[ci], axis=0, keepdims=True).astype(F32))
            alphas.append(jnp.exp2(m - mn))
            pt_buf[ci] = jnp.exp2(st_buf[ci] - mn.astype(BF16))
            m_new.append(mn)
        return tuple(m_new), tuple(alphas)

    def accumulate(kb, pt_buf, alphas):
        off = pl.multiple_of(kb * tk, tk)
        for ci, ((hh, c), alpha) in enumerate(zip(chains, alphas)):
            pv = jnp.dot(vt_ref[0, hh, :, pl.ds(off, tk)], pt_buf[ci],
                         preferred_element_type=F32)
            acc_s[ci] = alpha * acc_s[ci] + pv

    def tied_zero(src):
        u = pltpu.bitcast(src, jnp.uint32)
        u = lax.shift_right_logical(lax.shift_right_logical(u, jnp.uint32(16)), jnp.uint32(16))
        return pltpu.bitcast(u, F32)

    n_ch = len(chains)
    rs = tk // EXP_SLICES
    qk_rows = tk // EXP_SLICES
    pv_groups = VT_ROWS // 16
    qk_events = [(ci, r) for ci in range(n_ch) for r in range(EXP_SLICES)]
    pv_events = [(ci, g) for ci in range(n_ch) for g in range(pv_groups)]
    t_qk, t_pv = 8.0 * len(qk_events), 17.0 * len(pv_events)

    def event_for(i, n):
        t = (i + 0.5) * (t_qk + t_pv) / n
        if t < t_qk:
            return ("qk",) + qk_events[int(t / 8.0)]
        return ("pv",) + pv_events[min(int((t - t_qk) / 17.0), len(pv_events) - 1)]

    def step(kb, st_cur, pt_cur, st_nxt, pt_prev, ms, alphas_prev):
        off_n = pl.multiple_of((kb + 1) * tk, tk)
        off_p = pl.multiple_of((kb - 1) * tk, tk)
        res, pvs = [], []
        for ci, ((hh, c), qt) in enumerate(zip(chains, qts)):
            r = jnp.dot(k_ref[0, hh, pl.ds(off_n, tk), :], qt, preferred_element_type=F32)
            st_nxt[ci] = r.astype(BF16)
            res.append(r)
        for ci, ((hh, c), alpha) in enumerate(zip(chains, alphas_prev)):
            pv = jnp.dot(vt_ref[0, hh, :, pl.ds(off_p, tk)], pt_prev[ci],
                         preferred_element_type=F32)
            acc_s[ci] = alpha * acc_s[ci] + pv
            pvs.append(pv)
        m_new, alphas = [], []
        for ci, m in enumerate(ms):
            mn = jnp.maximum(m, jnp.max(st_cur[ci], axis=0, keepdims=True).astype(F32))
            alphas.append(jnp.exp2(m - mn))
            m_new.append(mn)
        slices = [(ci, r) for r in range(EXP_SLICES) for ci in range(n_ch)]
        for i, (ci, r) in enumerate(slices):
            kind, cj, e = event_for(i, len(slices))
            src = (res[cj][e * qk_rows:e * qk_rows + 1, :] if kind == "qk"
                   else pvs[cj][e * 16:e * 16 + 1, :])
            mt = (m_new[ci] + tied_zero(src)).astype(BF16)
            rows = slice(r * rs, (r + 1) * rs)
            pt_cur[ci, rows, :] = jnp.exp2(st_cur[ci, rows, :] - mt)
        return tuple(m_new), tuple(alphas)

    def pair(kb, carry):
        ms, alphas = step(kb, st_b, pt_b, st_a, pt_a, *carry)
        return step(kb + 1, st_a, pt_a, st_b, pt_b, ms, alphas)

    def body(jj, carry):
        kb = 2 * ATTN_PAIRS_PER_ITER * jj + 1
        for r in range(ATTN_PAIRS_PER_ITER):
            carry = pair(kb + 2 * r, carry)
        return carry

    nkb = S // tk
    assert nkb % 2 == 0
    n_pairs = nkb // 2 - 1
    acc_s[...] = jnp.zeros_like(acc_s)
    m0 = tuple(jnp.full((1, tq), -jnp.inf, F32) for _ in chains)
    scores(0, st_a)
    scores(1, st_b)
    carry = softmax(st_a, pt_a, m0)
    carry = lax.fori_loop(0, n_pairs // ATTN_PAIRS_PER_ITER, body, carry)
    for r in range(n_pairs - n_pairs % ATTN_PAIRS_PER_ITER, n_pairs):
        carry = pair(2 * r + 1, carry)
    ms, alphas = carry
    accumulate(nkb - 2, pt_a, alphas)
    ms, alphas = softmax(st_b, pt_b, ms)
    accumulate(nkb - 1, pt_b, alphas)
    res = tuple((None, acc_s[ci]) for ci in range(len(chains)))
    outs = []
    for hh in range(heads_per_step):
        a0 = res[2 * hh][1]
        a1 = res[2 * hh + 1][1]
        ot = (a0[:V_DIM, :] / a0[V_DIM:V_DIM + 1, :]
              - lam * (a1[:V_DIM, :] / a1[V_DIM:V_DIM + 1, :]))
        ot = ot * lax.rsqrt(jnp.mean(ot * ot, axis=0, keepdims=True) + RMS_EPS)
        outs.append(ot)
    o = jnp.concatenate(outs, axis=0).T
    o_ref[0] = (o * sw_ref[...] * (1.0 - LAMBDA_INIT)).astype(o_ref.dtype)


def _attn(lam_p, sw2, qt, k, vt, tq, tk):
    B, H, S, _ = k.shape
    hps = 2
    kern = functools.partial(_attn_kernel, tk=tk, heads_per_step=hps)
    return pl.pallas_call(
        kern,
        grid=(B, H // hps, S // tq),
        in_specs=[pl.BlockSpec((4, QK_DIM), lambda b, h, i: (0, 0)),
                  pl.BlockSpec((1, hps * V_DIM), lambda b, h, i: (0, 0)),
                  pl.BlockSpec((1, hps, 2, QK_CONTRACT, tq), lambda b, h, i: (b, h, 0, 0, i)),
                  pl.BlockSpec((1, hps, S, QK_CONTRACT), lambda b, h, i: (b, h, 0, 0)),
                  pl.BlockSpec((1, hps, VT_ROWS, S), lambda b, h, i: (b, h, 0, 0))],
        out_specs=pl.BlockSpec((1, tq, hps * V_DIM), lambda b, h, i: (b, i, h)),
        out_shape=jax.ShapeDtypeStruct((B, S, W_ATTN), BF16),
        scratch_shapes=[pltpu.VMEM((2 * hps, tk, tq), BF16), pltpu.VMEM((2 * hps, tk, tq), BF16),
                        pltpu.VMEM((2 * hps, tk, tq), BF16), pltpu.VMEM((2 * hps, tk, tq), BF16),
                        pltpu.VMEM((2 * hps, VT_ROWS, tq), F32)],
        compiler_params=_cparams(("arbitrary", "arbitrary", "arbitrary")),
        name="attn",
    )(lam_p, sw2, qt, k, vt)


def _s5_kernel(u_ref, d_ref, bm_ref, cm_ref, tab_ref, y_ref, xr_s, xi_s, *, rows):
    S = u_ref.shape[1]
    R = rows
    seg = R // SUBLANES
    nchunk = S // R
    P = STATE_LANES

    def cmul_add(ar, ai, xr, xi, br, bi):
        return ar * xr - ai * xi + br, ar * xi + ai * xr + bi

    for di in range(2):
        bm = bm_ref[di, 0]
        cm = cm_ref[di, 0]
        ar = tab_ref[di, 0, 0:1, :]
        ai = tab_ref[di, 0, 1:2, :]
        asr = tab_ref[di, 0, 2:3, :]
        asi = tab_ref[di, 0, 3:4, :]
        steps = list(range(seg)) if di == 0 else list(range(seg - 1, -1, -1))
        segs = list(range(SUBLANES)) if di == 0 else list(range(SUBLANES - 1, -1, -1))

        def chunk_body(ci, carry, di=di, bm=bm, cm=cm, ar=ar, ai=ai, asr=asr, asi=asi,
                       steps=steps, segs=segs):
            c = ci if di == 0 else nchunk - 1 - ci
            r0 = pl.multiple_of(c * R, R)
            u = jnp.concatenate([u_ref[0, pl.ds(r0 + j, SUBLANES, stride=seg), :]
                                 for j in range(seg)], axis=0)
            bu = jnp.dot(u.astype(BF16), bm, preferred_element_type=F32)
            ar8 = jnp.broadcast_to(ar, (SUBLANES, P))
            ai8 = jnp.broadcast_to(ai, (SUBLANES, P))
            zero8 = jnp.zeros((SUBLANES, P), F32)
            xr, xi = zero8, zero8
            for j in steps:
                xr, xi = cmul_add(ar8, ai8, xr, xi, bu[j * SUBLANES:(j + 1) * SUBLANES, :P],
                                  bu[j * SUBLANES:(j + 1) * SUBLANES, P:])
            cr, cim = carry
            ent_r, ent_i = [None] * SUBLANES, [None] * SUBLANES
            for s_ in segs:
                ent_r[s_], ent_i[s_] = cr, cim
                cr, cim = cmul_add(asr, asi, cr, cim, xr[s_:s_ + 1, :], xi[s_:s_ + 1, :])
            xr = jnp.concatenate(ent_r, axis=0)
            xi = jnp.concatenate(ent_i, axis=0)
            for j in steps:
                xr, xi = cmul_add(ar8, ai8, xr, xi, bu[j * SUBLANES:(j + 1) * SUBLANES, :P],
                                  bu[j * SUBLANES:(j + 1) * SUBLANES, P:])
                xr_s[j * SUBLANES:(j + 1) * SUBLANES, :] = xr
                xi_s[j * SUBLANES:(j + 1) * SUBLANES, :] = xi
            yc = (jnp.dot(xr_s[...].astype(BF16), cm[:P], preferred_element_type=F32)
                  + jnp.dot(xi_s[...].astype(BF16), cm[P:], preferred_element_type=F32))
            if di == 0:
                yc = yc + u * d_ref[0]
            for j in range(seg):
                rows_j = pl.ds(r0 + j, SUBLANES, stride=seg)
                piece = yc[j * SUBLANES:(j + 1) * SUBLANES, :]
                if di == 0:
                    y_ref[0, rows_j, :] = piece
                else:
                    y_ref[0, rows_j, :] = y_ref[0, rows_j, :] + piece
            return cr, cim

        zero = jnp.zeros((1, P), F32)
        lax.fori_loop(0, nchunk, chunk_body, (zero, zero))


def _s5(u, d3, bmat, cmat, tab, rows):
    B, S, W = u.shape
    nb = W // LANES
    P = STATE_LANES
    kern = functools.partial(_s5_kernel, rows=rows)
    return pl.pallas_call(
        kern,
        grid=(B, nb),
        in_specs=[pl.BlockSpec((1, S, LANES), lambda b, g: (b, 0, g)),
                  pl.BlockSpec((1, 1, LANES), lambda b, g: (g, 0, 0)),
                  pl.BlockSpec((2, 1, LANES, 2 * P), lambda b, g: (0, g, 0, 0)),
                  pl.BlockSpec((2, 1, 2 * P, LANES), lambda b, g: (0, g, 0, 0)),
                  pl.BlockSpec((2, 1, 4, P), lambda b, g: (0, g, 0, 0))],
        out_specs=pl.BlockSpec((1, S, LANES), lambda b, g: (b, 0, g)),
        out_shape=jax.ShapeDtypeStruct((B, S, W), F32),
        scratch_shapes=[pltpu.VMEM((rows, P), F32), pltpu.VMEM((rows, P), F32)],
        compiler_params=_cparams(("arbitrary", "arbitrary")),
        name="s5",
    )(u, d3, bmat, cmat, tab)


def _s5_tables(lam_re, lam_im, log_step, b_re, b_im, cm_re, cm_im, seg):
    nb = S5_GROUPS // GROUPS_PER_BLOCK
    eye = jnp.eye(GROUPS_PER_BLOCK, dtype=F32)
    bmats, cmats, tabs = [], [], []
    for di in range(2):
        lr = lam_re[di].astype(F32)
        li = lam_im[di].astype(F32)
        delta = jnp.exp(log_step[di].astype(F32))[:, None]
        mag = jnp.exp(lr * delta)
        a_re = mag * jnp.cos(li * delta)
        a_im = mag * jnp.sin(li * delta)
        den = lr * lr + li * li
        num_re = a_re - 1.0
        f_re = (num_re * lr + a_im * li) / den
        f_im = (a_im * lr - num_re * li) / den
        br = b_re[di].astype(F32)
        bi = b_im[di].astype(F32)
        bbar_re = f_re[..., None] * br - f_im[..., None] * bi
        bbar_im = f_re[..., None] * bi + f_im[..., None] * br

        def blockdiag_in(bb):
            bb = bb.reshape(nb, GROUPS_PER_BLOCK, S5_STATE, S5_CH)
            m = jnp.einsum('bgpc,gh->bgchp', bb, eye)
            return m.reshape(nb, LANES, STATE_LANES)

        def blockdiag_out(cc):
            cc = cc.reshape(nb, GROUPS_PER_BLOCK, S5_CH, S5_STATE)
            m = jnp.einsum('bgcp,gh->bgphc', cc, eye)
            return m.reshape(nb, STATE_LANES, LANES)

        bmats.append(jnp.concatenate([blockdiag_in(bbar_re), blockdiag_in(bbar_im)], axis=2))
        cmats.append(jnp.concatenate([blockdiag_out(cm_re[di].astype(F32)),
                                      -blockdiag_out(cm_im[di].astype(F32))], axis=1))
        sr, si = a_re, a_im
        n = 1
        while n < seg:
            sr, si = sr * sr - si * si, 2.0 * sr * si
            n *= 2
        assert n == seg
        tabs.append(jnp.stack([a_re, a_im, sr, si]).reshape(4, nb, STATE_LANES)
                    .transpose(1, 0, 2))
    return jnp.stack(bmats).astype(BF16), jnp.stack(cmats).astype(BF16), jnp.stack(tabs)


def _post_kernel(a_ref, ys_ref, x_ref, mod_ref, gluw_ref, glub_ref, onw_ref, wout_ref, n2w_ref,
                 rw_ref, rb_ref, x1_ref, h2_ref, eidx_ref, gate_ref, rank_ref, cnt_ref, carry_s):
    first = (pl.program_id(0) == 0) & (pl.program_id(1) == 0)

    @pl.when(first)
    def _():
        carry_s[...] = jnp.zeros_like(carry_s)

    tm = x_ref.shape[1]
    y = ys_ref[0]
    y = 0.5 * y * (1.0 + jnp.tanh(math.sqrt(2.0 / math.pi) * (y + 0.044715 * (y * y * y))))
    g = jnp.dot(y.astype(BF16), gluw_ref[...], preferred_element_type=F32) + glub_ref[...]
    y = y * jax.nn.sigmoid(g)
    s = _rms(y) * onw_ref[...]
    mix = (jnp.dot(a_ref[0], wout_ref[0:W_ATTN, :], preferred_element_type=F32)
           + jnp.dot(s.astype(BF16), wout_ref[W_ATTN:, :], preferred_element_type=F32))
    g1 = mod_ref[0, 2:3, :]
    sh2 = mod_ref[0, 3:4, :]
    sc2 = mod_ref[0, 4:5, :]
    x1 = x_ref[0] + g1 * mix
    x1_ref[0] = x1
    h2 = (_rms(x1) * n2w_ref[...]) * (1.0 + sc2) + sh2
    h2_ref[0] = h2
    logits = jnp.dot(h2, rw_ref[...], preferred_element_type=F32,
                     precision=lax.Precision.HIGHEST) + rb_ref[...]
    lane = lax.broadcasted_iota(jnp.int32, (tm, LANES), 1)
    neg = jnp.float32(-jnp.inf)
    work = jnp.where(lane < N_EXPERTS, logits, neg)
    vals, idxs = [], []
    for _ in range(TOP_K):
        m = jnp.max(work, axis=-1, keepdims=True)
        idx = jnp.min(jnp.where(work == m, lane, LANES), axis=-1, keepdims=True)
        vals.append(m)
        idxs.append(idx)
        work = jnp.where(lane == idx, neg, work)
    es = [jnp.exp(v - vals[0]) for v in vals]
    den = es[0] + es[1] + es[2] + es[3]
    onehot = jnp.zeros((tm, LANES), F32)
    for idx in idxs:
        onehot = onehot + jnp.where(lane == idx, 1.0, 0.0)
    r_i = lax.broadcasted_iota(jnp.int32, (tm, tm), 0)
    c_i = lax.broadcasted_iota(jnp.int32, (tm, tm), 1)
    ltri = jnp.where(c_i < r_i, 1.0, 0.0).astype(BF16)
    before = jnp.dot(ltri, onehot.astype(BF16), preferred_element_type=F32) + carry_s[...]
    lane4 = lax.broadcasted_iota(jnp.int32, (tm, TOP_K), 1)
    e_out = jnp.zeros((tm, TOP_K), jnp.int32)
    g_out = jnp.zeros((tm, TOP_K), F32)
    r_out = jnp.zeros((tm, TOP_K), jnp.int32)
    for j in range(TOP_K):
        rk = jnp.sum(jnp.where(lane == idxs[j], before, 0.0), axis=-1, keepdims=True)
        e_out = jnp.where(lane4 == j, idxs[j], e_out)
        g_out = jnp.where(lane4 == j, es[j] / den, g_out)
        r_out = jnp.where(lane4 == j, rk.astype(jnp.int32), r_out)
    eidx_ref[0] = e_out
    gate_ref[0] = g_out
    rank_ref[0] = r_out
    carry_s[...] = carry_s[...] + jnp.sum(onehot, axis=0, keepdims=True)
    cnt_ref[...] = carry_s[...]


def _post(a, ys, x, mod3, gluw, glub, onw, wout, n2w, rw, rb, tm):
    B, S, D = x.shape
    c2 = lambda b, i: (0, 0)
    tok = lambda b, i: (b, i, 0)
    return pl.pallas_call(
        _post_kernel,
        grid=(B, S // tm),
        in_specs=[pl.BlockSpec((1, tm, W_ATTN), tok),
                  pl.BlockSpec((1, tm, W_S5), tok),
                  pl.BlockSpec((1, tm, D), tok),
                  pl.BlockSpec((1, 6, D), lambda b, i: (b, 0, 0)),
                  pl.BlockSpec((W_S5, W_S5), c2),
                  pl.BlockSpec((1, W_S5), c2),
                  pl.BlockSpec((1, W_S5), c2),
                  pl.BlockSpec((D, D), c2),
                  pl.BlockSpec((1, D), c2),
                  pl.BlockSpec((D, LANES), c2),
                  pl.BlockSpec((1, LANES), c2)],
        out_specs=[pl.BlockSpec((1, tm, D), tok),
                   pl.BlockSpec((1, tm, D), tok),
                   pl.BlockSpec((1, tm, TOP_K), tok),
                   pl.BlockSpec((1, tm, TOP_K), tok),
                   pl.BlockSpec((1, tm, TOP_K), tok),
                   pl.BlockSpec((1, LANES), c2)],
        out_shape=[jax.ShapeDtypeStruct((B, S, D), F32),
                   jax.ShapeDtypeStruct((B, S, D), F32),
                   jax.ShapeDtypeStruct((B, S, TOP_K), jnp.int32),
                   jax.ShapeDtypeStruct((B, S, TOP_K), F32),
                   jax.ShapeDtypeStruct((B, S, TOP_K), jnp.int32),
                   jax.ShapeDtypeStruct((1, LANES), F32)],
        scratch_shapes=[pltpu.VMEM((1, LANES), F32)],
        compiler_params=_cparams(("arbitrary", "arbitrary")),
        name="post",
    )(a, ys, x, mod3, gluw, glub, onw, wout, n2w, rw, rb)


def _dispatch_kernel(pend_ref, padded_ref, dest_ref, h2_ref, xs_hbm, zero_s, sem, zsem, *, tg):
    @pl.when(pl.program_id(0) == 0)
    def _():
        zero_s[...] = jnp.zeros_like(zero_s)

        def zero_copy(e):
            start = pl.multiple_of(pend_ref[e] - MOE_ROWS, MOE_ROWS)
            return pltpu.make_async_copy(zero_s, xs_hbm.at[pl.ds(start, MOE_ROWS)], zsem)

        for e in range(N_EXPERTS):
            @pl.when(padded_ref[e] > 0)
            def _():
                zero_copy(e).start()
        for e in range(N_EXPERTS):
            @pl.when(padded_ref[e] > 0)
            def _():
                zero_copy(e).wait()

    def row_copy(t, d):
        return pltpu.make_async_copy(h2_ref.at[pl.ds(t, 1)], xs_hbm.at[pl.ds(d, 1)], sem)

    def issue(t, _):
        for j in range(TOP_K):
            row_copy(t, dest_ref[0, 0, t * TOP_K + j]).start()
        return 0

    lax.fori_loop(0, tg, issue, 0, unroll=2)
    for _ in range(TOP_K):
        pltpu.make_async_copy(h2_ref, xs_hbm.at[pl.ds(0, tg)], sem).wait()


def _dispatch(pad_end, padded, dest3, h2, n_rows, tg):
    T, D = h2.shape
    kern = functools.partial(_dispatch_kernel, tg=tg)
    grid_spec = pltpu.PrefetchScalarGridSpec(
        num_scalar_prefetch=2,
        grid=(T // tg,),
        in_specs=[pl.BlockSpec((1, 1, tg * TOP_K), lambda i, pe, pd: (i, 0, 0),
                               memory_space=pltpu.SMEM),
                  pl.BlockSpec((tg, D), lambda i, pe, pd: (i, 0))],
        out_specs=pl.BlockSpec(memory_space=pl.ANY),
        scratch_shapes=[pltpu.VMEM((MOE_ROWS, D), F32), pltpu.SemaphoreType.DMA(()),
                        pltpu.SemaphoreType.DMA(())],
    )
    return pl.pallas_call(
        kern,
        grid_spec=grid_spec,
        out_shape=jax.ShapeDtypeStruct((n_rows, D), F32),
        compiler_params=_cparams(("arbitrary",)),
        name="dispatch",
    )(pad_end, padded, dest3, h2)


PERM_CHUNK = 2 * LANES


def _expert_kernel(be_ref, na_ref, x_ref, w1_ref, bg_ref, bl_ref, w2_ref, b2_ref, perm_ref, y_ref,
                   wg_s, wl_s, w2_s):
    i = pl.program_id(0)
    active = i < na_ref[0]
    prev = be_ref[jnp.maximum(i - 1, 0)]
    changed = active & ((i == 0) | (be_ref[i] != prev))

    @pl.when(changed)
    def _():
        perm = perm_ref[...]
        for j in range(2 * D_FF // PERM_CHUNK):
            chunk = w1_ref[0, :, j * PERM_CHUNK:(j + 1) * PERM_CHUNK].astype(BF16)
            sep = jnp.dot(chunk, perm, preferred_element_type=F32).astype(BF16)
            wg_s[:, j * LANES:(j + 1) * LANES] = sep[:, :LANES]
            wl_s[:, j * LANES:(j + 1) * LANES] = sep[:, LANES:]
        w2_s[...] = w2_ref[0].astype(BF16)

    @pl.when(active)
    def _():
        x = x_ref[...].astype(BF16)
        zg = jnp.dot(x, wg_s[...], preferred_element_type=F32) + bg_ref[0]
        zl = jnp.dot(x, wl_s[...], preferred_element_type=F32) + bl_ref[0]
        xg = jnp.minimum(zg, SWIGLU_LIMIT)
        xl = jnp.clip(zl, -SWIGLU_LIMIT, SWIGLU_LIMIT)
        act = xg * jax.nn.sigmoid(SWIGLU_ALPHA * xg) * (xl + 1.0)
        y_ref[...] = jnp.dot(act.astype(BF16), w2_s[...], preferred_element_type=F32) + b2_ref[0]


def _experts(block_e, n_active, xs, w1, bg, bl, w2, b2, perm):
    n_rows, D = xs.shape
    nblk = n_rows // MOE_ROWS

    def row_map(i, be, na):
        return (jnp.minimum(i, na[0] - 1), 0)

    def w_map(i, be, na):
        return (be[i], 0, 0)

    grid_spec = pltpu.PrefetchScalarGridSpec(
        num_scalar_prefetch=2,
        grid=(nblk,),
        in_specs=[pl.BlockSpec((MOE_ROWS, D), row_map),
                  pl.BlockSpec((1, D, 2 * D_FF), w_map),
                  pl.BlockSpec((1, 1, D_FF), w_map),
                  pl.BlockSpec((1, 1, D_FF), w_map),
                  pl.BlockSpec((1, D_FF, D), w_map),
                  pl.BlockSpec((1, 1, D), w_map),
                  pl.BlockSpec((PERM_CHUNK, PERM_CHUNK), lambda i, be, na: (0, 0))],
        out_specs=pl.BlockSpec((MOE_ROWS, D), row_map),
        scratch_shapes=[pltpu.VMEM((D, D_FF), BF16), pltpu.VMEM((D, D_FF), BF16),
                        pltpu.VMEM((D_FF, D), BF16)],
    )
    return pl.pallas_call(
        _expert_kernel,
        grid_spec=grid_spec,
        out_shape=jax.ShapeDtypeStruct((n_rows, D), F32),
        compiler_params=_cparams(("arbitrary",)),
        name="experts",
    )(block_e, n_active, xs, w1, bg, bl, w2, b2, perm)


def _combine_kernel(dest_ref, destn_ref, gate_ref, x1_ref, mod_ref, ys_hbm, o_ref, buf, sem, *, tc):
    i = pl.program_id(0)
    slot = i % 2

    def issue_all(idx_ref, sl):
        def issue(t, _):
            for j in range(TOP_K):
                d = idx_ref[0, 0, t * TOP_K + j]
                pltpu.make_async_copy(ys_hbm.at[pl.ds(d, 1)], buf.at[sl, j, pl.ds(t, 1)],
                                      sem.at[sl]).start()
            return 0

        lax.fori_loop(0, tc, issue, 0, unroll=2)

    @pl.when(i == 0)
    def _():
        issue_all(dest_ref, 0)

    @pl.when(i + 1 < pl.num_programs(0))
    def _():
        issue_all(destn_ref, 1 - slot)

    for j in range(TOP_K):
        pltpu.make_async_copy(ys_hbm.at[pl.ds(0, tc)], buf.at[slot, j], sem.at[slot]).wait()
    gates = gate_ref[...]
    acc = gates[:, 0:1] * buf[slot, 0]
    for j in range(1, TOP_K):
        acc = acc + gates[:, j:j + 1] * buf[slot, j]
    o_ref[...] = x1_ref[...] + mod_ref[0, 5:6, :] * acc


def _combine(dest3, gates, x1, mod3, ys, tc, tiles_per_batch):
    T, D = x1.shape
    n_tiles = T // tc
    kern = functools.partial(_combine_kernel, tc=tc)
    return pl.pallas_call(
        kern,
        grid=(n_tiles,),
        in_specs=[pl.BlockSpec((1, 1, tc * TOP_K), lambda i: (i, 0, 0), memory_space=pltpu.SMEM),
                  pl.BlockSpec((1, 1, tc * TOP_K), lambda i: (jnp.minimum(i + 1, n_tiles - 1), 0, 0),
                               memory_space=pltpu.SMEM),
                  pl.BlockSpec((tc, TOP_K), lambda i: (i, 0)),
                  pl.BlockSpec((tc, D), lambda i: (i, 0)),
                  pl.BlockSpec((1, 6, D), lambda i: (i // tiles_per_batch, 0, 0)),
                  pl.BlockSpec(memory_space=pl.ANY)],
        out_specs=pl.BlockSpec((tc, D), lambda i: (i, 0)),
        out_shape=jax.ShapeDtypeStruct((T, D), F32),
        scratch_shapes=[pltpu.VMEM((2, TOP_K, tc, D), F32), pltpu.SemaphoreType.DMA((2,))],
        compiler_params=_cparams(("arbitrary",)),
        name="combine",
    )(dest3, dest3, gates, x1, mod3, ys)


def kernel(x, c, positions, ada_w, ada_b, norm1_w, w_in, q_norm_w, k_norm_w, lambda_q1, lambda_k1,
           lambda_q2, lambda_k2, subln_w, s5_lambda_re, s5_lambda_im, s5_log_step, s5_b_re, s5_b_im,
           s5_cmat_re, s5_cmat_im, s5_d, s5_glu_w, s5_glu_b, s5_out_norm_w, w_out, norm2_w,
           router_w, router_b, mlp1_w, mlp1_b, mlp2_w, mlp2_b):
    B, S, D = x.shape
    T = B * S
    l = 0

    c_pad = jnp.pad(c, ((0, SUBLANES - B % SUBLANES if B % SUBLANES else 0), (0, 0)))
    mod = _ada(c_pad, ada_w[l], ada_b[l][None, :])[:B]
    mod3 = mod.reshape(B, 6, D)

    tm = min(512, S)
    inv_freq = ROPE_THETA ** (-jnp.arange(0, ROT_DIM, 2, dtype=F32) / ROT_DIM)
    d_in_head = jnp.arange(LANES) % QK_DIM
    invf = jnp.where(d_in_head < ROT_DIM, inv_freq[d_in_head % (ROT_DIM // 2)], 0.0)[None, :]
    pos3 = positions.astype(F32)[..., None]
    gmat = jnp.kron(jnp.eye(W_QK // QK_DIM, dtype=F32), jnp.ones((QK_DIM, QK_DIM), F32)).astype(BF16)
    q_bound = jnp.max(jnp.abs(q_norm_w[l])) * (math.log2(math.e) / math.sqrt(QK_DIM))
    k_bound = jnp.max(jnp.abs(k_norm_w[l]))
    shift = jnp.exp2(jnp.floor(0.5 * jnp.log2(jnp.maximum(q_bound, 1e-30)
                                              / jnp.maximum(k_bound, 1e-30))))
    qw = jnp.tile(q_norm_w[l], W_QK // QK_DIM)[None, :] / shift
    kw = jnp.tile(k_norm_w[l], W_QK // QK_DIM)[None, :] * shift
    qt, k, vt, u = _inproj(x, mod3, norm1_w[l][None, :], pos3, invf, w_in[l].astype(BF16),
                          qw, kw, gmat, tm)

    lam_p = jnp.stack([lambda_q1[l], lambda_k1[l], lambda_q2[l], lambda_k2[l]]).astype(F32)
    sw2 = jnp.tile(subln_w[l], 2)[None, :]
    a_out = _attn(lam_p, sw2, qt, k, vt, tq=min(512, S), tk=min(256, S // 2))

    s5_rows = min(256, S)
    bmat, cmat, tab = _s5_tables(s5_lambda_re[l], s5_lambda_im[l], s5_log_step[l], s5_b_re[l],
                                 s5_b_im[l], s5_cmat_re[l], s5_cmat_im[l], s5_rows // SUBLANES)
    d3 = s5_d[l].astype(F32).reshape(W_S5 // LANES, 1, LANES)
    y_s5 = _s5(u, d3, bmat, cmat, tab, rows=s5_rows)

    rw = jnp.pad(router_w[l], ((0, 0), (0, LANES - N_EXPERTS)))
    rb = jnp.pad(router_b[l], (0, LANES - N_EXPERTS))[None, :]
    tp = min(256, S)
    x1, h2, eidx, gates, rank, counts = _post(
        a_out, y_s5, x, mod3, s5_glu_w[l].astype(BF16), s5_glu_b[l][None, :],
        s5_out_norm_w[l][None, :], w_out[l].astype(BF16), norm2_w[l][None, :], rw, rb, tp)

    counts = counts[0, :N_EXPERTS].astype(jnp.int32)
    padded = ((counts + MOE_ROWS - 1) // MOE_ROWS) * MOE_ROWS
    pad_end = jnp.cumsum(padded)
    pad_start = pad_end - padded
    eflat = eidx.reshape(T * TOP_K)
    dest = pad_start[eflat] + rank.reshape(T * TOP_K)
    n_rows = T * TOP_K + N_EXPERTS * MOE_ROWS
    nblk = n_rows // MOE_ROWS
    blk_row = jnp.arange(nblk, dtype=jnp.int32)[:, None] * MOE_ROWS
    block_e = jnp.minimum(jnp.sum((blk_row >= pad_end[None, :]).astype(jnp.int32), axis=1),
                          N_EXPERTS - 1)
    n_active = (pad_end[-1] // MOE_ROWS).astype(jnp.int32)[None]

    tg = min(512, T)
    xs = _dispatch(pad_end.astype(jnp.int32), padded, dest.reshape(T // tg, 1, tg * TOP_K),
                   h2.reshape(T, D), n_rows, tg)

    bg = mlp1_b[l][:, None, 0::2]
    bl = mlp1_b[l][:, None, 1::2]
    src = jnp.arange(PERM_CHUNK)
    perm = (jnp.arange(PERM_CHUNK)[None, :] == ((src % 2) * LANES + src // 2)[:, None]).astype(BF16)
    ys = _experts(block_e, n_active, xs, mlp1_w[l], bg, bl, mlp2_w[l], mlp2_b[l][:, None, :], perm)

    tc = min(256, S)
    out = _combine(dest.reshape(T // tc, 1, tc * TOP_K), gates.reshape(T, TOP_K),
                   x1.reshape(T, D), mod3, ys, tc, S // tc)
    return out.reshape(B, S, D)
```

```python
import functools
import math

import jax
import jax.numpy as jnp
from jax import lax
from jax.experimental import pallas as pl
from jax.experimental.pallas import tpu as pltpu

F32 = jnp.float32
BF16 = jnp.bfloat16

D_MODEL = 1024
QK_DIM = 32
V_DIM = 64
N_HEADS = 8
W_QK = N_HEADS * 2 * QK_DIM
W_ATTN = N_HEADS * V_DIM
ROT_DIM = QK_DIM // 4
ROPE_THETA = 500000.0
S5_CH = 16
S5_STATE = 64
W_S5 = 512
S5_GROUPS = W_S5 // S5_CH
D_IN_PROJ = 2 * W_QK + W_ATTN + W_S5
N_EXPERTS = 32
TOP_K = 4
D_FF = D_MODEL
SWIGLU_ALPHA = 1.702
SWIGLU_LIMIT = 7.0
RMS_EPS = 1e-6
LAMBDA_INIT = 0.8 - 0.6 * math.exp(-0.3 * 0)

LANES = 128
SUBLANES = 8
VMEM_LIMIT = 56 * 1024 * 1024

GROUPS_PER_BLOCK = LANES // S5_CH
STATE_LANES = GROUPS_PER_BLOCK * S5_STATE
MOE_ROWS = 512
VT_ROWS = V_DIM + 16
ATTN_PAIRS_PER_ITER = 2
EXP_SLICES = 8
F8 = jnp.float8_e4m3fn
F8_MAX = 448.0
QK_CONTRACT = 4 * 2 * QK_DIM


def _cparams(sem):
    return pltpu.CompilerParams(dimension_semantics=sem, vmem_limit_bytes=VMEM_LIMIT)


def _rms(x, eps=RMS_EPS):
    return x * lax.rsqrt(jnp.mean(x * x, axis=-1, keepdims=True) + eps)


def _ada_kernel(c_ref, w_ref, b_ref, o_ref):
    c = c_ref[...]
    ca = c * jax.nn.sigmoid(c)
    o_ref[...] = jnp.dot(ca, w_ref[...], preferred_element_type=F32,
                         precision=lax.Precision.HIGHEST) + b_ref[...]


def _ada(c_pad, w, b):
    rows, d = c_pad.shape
    n = w.shape[1]
    tn = 1536
    return pl.pallas_call(
        _ada_kernel,
        grid=(n // tn,),
        in_specs=[pl.BlockSpec((rows, d), lambda j: (0, 0)),
                  pl.BlockSpec((d, tn), lambda j: (0, j)),
                  pl.BlockSpec((1, tn), lambda j: (0, j))],
        out_specs=pl.BlockSpec((rows, tn), lambda j: (0, j)),
        out_shape=jax.ShapeDtypeStruct((rows, n), F32),
        compiler_params=_cparams(("arbitrary",)),
        name="ada",
    )(c_pad, w, b)


def _inproj_kernel(x_ref, mod_ref, n1w_ref, pos_ref, invf_ref, win_ref, qw_ref, kw_ref, gm_ref,
                   qt_ref, k_ref, vt_ref, u_ref, *, q_scale):
    x = x_ref[0]
    tm = x.shape[0]
    sh1 = mod_ref[0, 0:1, :]
    sc1 = mod_ref[0, 1:2, :]
    h = (_rms(x) * n1w_ref[...]) * (1.0 + sc1) + sh1
    proj = jnp.dot(h.astype(BF16), win_ref[...], preferred_element_type=F32)

    ang = pos_ref[0] * invf_ref[...]
    cos = jnp.cos(ang)
    sin = jnp.sin(ang)
    d_in_head = lax.broadcasted_iota(jnp.int32, (1, LANES), 1) % QK_DIM
    half = ROT_DIM // 2
    s_lo = jnp.where(d_in_head < half, -sin, 0.0)
    s_hi = jnp.where((d_in_head >= half) & (d_in_head < ROT_DIM), sin, 0.0)
    reps = W_QK // LANES
    cos = jnp.concatenate([cos] * reps, axis=1)
    s_lo = jnp.concatenate([s_lo] * reps, axis=1)
    s_hi = jnp.concatenate([s_hi] * reps, axis=1)

    def qk_norm_rope(t, w_ref, scale):
        ssq = jnp.dot((t * t).astype(BF16), gm_ref[...], preferred_element_type=F32)
        tn = t * lax.rsqrt(ssq * (1.0 / QK_DIM) + RMS_EPS) * w_ref[...]
        r = (tn * cos + pltpu.roll(tn, W_QK - half, 1) * s_lo + pltpu.roll(tn, half, 1) * s_hi)
        return r * scale

    q = qk_norm_rope(proj[:, 0:W_QK], qw_ref, q_scale)
    k = qk_norm_rope(proj[:, W_QK:2 * W_QK], kw_ref, 1.0)
    v = proj[:, 2 * W_QK:2 * W_QK + W_ATTN]
    u_ref[0] = proj[:, 2 * W_QK + W_ATTN:]
    qt = q.T
    vt = v.T
    hw = 2 * QK_DIM

    def split8(x):
        xc = jnp.clip(x, -F8_MAX, F8_MAX)
        hi = xc.astype(F8)
        return hi, (xc - hi.astype(F32)).astype(F8)

    k_hi, k_lo = split8(k)
    map_of_row = (lax.broadcasted_iota(jnp.int32, (W_QK, tm), 0) // QK_DIM) % 2
    q_parts = [split8(jnp.where(map_of_row == c, qt, 0.0)) for c in range(2)]
    tail = jnp.where(lax.broadcasted_iota(jnp.int32, (VT_ROWS - V_DIM, tm), 0) == 0,
                     1.0, 0.0).astype(BF16)
    zk = jnp.zeros((tm, hw), F8)
    zq = jnp.zeros((hw, tm), F8)
    for hd in range(N_HEADS):
        cols = slice(hd * hw, (hd + 1) * hw)
        k_ref[0, hd, :, 0 * hw:1 * hw] = k_hi[:, cols]
        k_ref[0, hd, :, 1 * hw:2 * hw] = k_lo[:, cols]
        k_ref[0, hd, :, 2 * hw:3 * hw] = k_hi[:, cols]
        k_ref[0, hd, :, 3 * hw:4 * hw] = zk
        for c in range(2):
            q_hi, q_lo = q_parts[c]
            qt_ref[0, hd, c, 0 * hw:1 * hw, :] = q_hi[cols, :]
            qt_ref[0, hd, c, 1 * hw:2 * hw, :] = q_hi[cols, :]
            qt_ref[0, hd, c, 2 * hw:3 * hw, :] = q_lo[cols, :]
            qt_ref[0, hd, c, 3 * hw:4 * hw, :] = zq
        vt_ref[0, hd, 0:V_DIM, :] = vt[hd * V_DIM:(hd + 1) * V_DIM, :].astype(BF16)
        vt_ref[0, hd, V_DIM:, :] = tail


def _inproj(x, mod3, n1w, pos3, invf, win_bf, qw, kw, gmat, tm):
    B, S, D = x.shape
    q_scale = math.log2(math.e) / math.sqrt(QK_DIM)
    kern = functools.partial(_inproj_kernel, q_scale=q_scale)
    const2 = lambda b, i: (0, 0)
    return pl.pallas_call(
        kern,
        grid=(B, S // tm),
        in_specs=[pl.BlockSpec((1, tm, D), lambda b, i: (b, i, 0)),
                  pl.BlockSpec((1, 6, D), lambda b, i: (b, 0, 0)),
                  pl.BlockSpec((1, D), const2),
                  pl.BlockSpec((1, tm, 1), lambda b, i: (b, i, 0)),
                  pl.BlockSpec((1, LANES), const2),
                  pl.BlockSpec((D, D_IN_PROJ), const2),
                  pl.BlockSpec((1, W_QK), const2),
                  pl.BlockSpec((1, W_QK), const2),
                  pl.BlockSpec((W_QK, W_QK), const2)],
        out_specs=[pl.BlockSpec((1, N_HEADS, 2, QK_CONTRACT, tm), lambda b, i: (b, 0, 0, 0, i)),
                   pl.BlockSpec((1, N_HEADS, tm, QK_CONTRACT), lambda b, i: (b, 0, i, 0)),
                   pl.BlockSpec((1, N_HEADS, VT_ROWS, tm), lambda b, i: (b, 0, 0, i)),
                   pl.BlockSpec((1, tm, W_S5), lambda b, i: (b, i, 0))],
        out_shape=[jax.ShapeDtypeStruct((B, N_HEADS, 2, QK_CONTRACT, S), F8),
                   jax.ShapeDtypeStruct((B, N_HEADS, S, QK_CONTRACT), F8),
                   jax.ShapeDtypeStruct((B, N_HEADS, VT_ROWS, S), BF16),
                   jax.ShapeDtypeStruct((B, S, W_S5), F32)],
        compiler_params=_cparams(("arbitrary", "arbitrary")),
        name="inproj",
    )(x, mod3, n1w, pos3, invf, win_bf, qw, kw, gmat)


def _attn_kernel(lam_ref, sw_ref, qt_ref, k_ref, vt_ref, o_ref, st_a, st_b, pt_a, pt_b, acc_s, *,
                 tk, heads_per_step):
    S = k_ref.shape[2]
    tq = qt_ref.shape[4]
    lp = lam_ref[...]
    lam = (jnp.exp(jnp.sum(lp[0:1] * lp[1:2], axis=-1, keepdims=True))
           - jnp.exp(jnp.sum(lp[2:3] * lp[3:4], axis=-1, keepdims=True)) + LAMBDA_INIT)
    chains = [(hh, c) for hh in range(heads_per_step) for c in range(2)]
    qts = [qt_ref[0, hh, c] for hh, c in chains]

    def scores(kb, st_buf):
        off = pl.multiple_of(kb * tk, tk)
        for ci, ((hh, c), qt) in enumerate(zip(chains, qts)):
            st_buf[ci] = jnp.dot(k_ref[0, hh, pl.ds(off, tk), :], qt,
                                 preferred_element_type=F32).astype(BF16)

    def softmax(st_buf, pt_buf, ms):
        m_new, alphas = [], []
        for ci, m in enumerate(ms):
            mn = jnp.maximum(m, jnp.max(st_buf[ci], axis=0, keepdims=True).astype(F32))
            alphas.append(jnp.exp2(m - mn))
            pt_buf[ci] = jnp.exp2(st_buf[ci] - mn.astype(BF16))
            m_new.append(mn)
        return tuple(m_new), tuple(alphas)

    def accumulate(kb, pt_buf, alphas):
        off = pl.multiple_of(kb * tk, tk)
        for ci, ((hh, c), alpha) in enumerate(zip(chains, alphas)):
            pv = jnp.dot(vt_ref[0, hh, :, pl.ds(off, tk)], pt_buf[ci],
                         preferred_element_type=F32)
            acc_s[ci] = alpha * acc_s[ci] + pv

    def tied_zero(src):
        u = pltpu.bitcast(src, jnp.uint32)
        u = lax.shift_right_logical(lax.shift_right_logical(u, jnp.uint32(16)), jnp.uint32(16))
        return pltpu.bitcast(u, F32)

    n_ch = len(chains)
    rs = tk // EXP_SLICES
    qk_rows = tk // EXP_SLICES
    pv_groups = VT_ROWS // 16
    qk_events = [(ci, r) for ci in range(n_ch) for r in range(EXP_SLICES)]
    pv_events = [(ci, g) for ci in range(n_ch) for g in range(pv_groups)]
    dt_qk, dt_pv = 8.0 * qk_rows / 32.0, 10.0
    t_qk, t_pv = dt_qk * len(qk_events), dt_pv * len(pv_events)

    def event_for(i, n):
        t = (i + 0.5) * (t_qk + t_pv) / n
        if t < t_qk:
            return ("qk",) + qk_events[int(t / dt_qk)]
        return ("pv",) + pv_events[min(int((t - t_qk) / dt_pv), len(pv_events) - 1)]

    def step(kb, st_cur, pt_cur, st_nxt, pt_prev, ms, alphas_prev):
        off_n = pl.multiple_of((kb + 1) * tk, tk)
        off_p = pl.multiple_of((kb - 1) * tk, tk)
        res, pvs = [], []
        for ci, ((hh, c), qt) in enumerate(zip(chains, qts)):
            r = jnp.dot(k_ref[0, hh, pl.ds(off_n, tk), :], qt, preferred_element_type=F32)
            st_nxt[ci] = r.astype(BF16)
            res.append(r)
        for ci, ((hh, c), alpha) in enumerate(zip(chains, alphas_prev)):
            pv = jnp.dot(vt_ref[0, hh, :, pl.ds(off_p, tk)], pt_prev[ci],
                         preferred_element_type=F32)
            acc_s[ci] = alpha * acc_s[ci] + pv
            pvs.append(pv)
        m_new, alphas = [], []
        for ci, m in enumerate(ms):
            mn = jnp.maximum(m, jnp.max(st_cur[ci], axis=0, keepdims=True).astype(F32))
            alphas.append(jnp.exp2(m - mn))
            m_new.append(mn)
        slices = [(ci, r) for r in range(EXP_SLICES) for ci in range(n_ch)]
        for i, (ci, r) in enumerate(slices):
            kind, cj, e = event_for(i, len(slices))
            src = (res[cj][e * qk_rows:e * qk_rows + 1, :] if kind == "qk"
                   else pvs[cj][e * 16:e * 16 + 1, :])
            mt = (m_new[ci] + tied_zero(src)).astype(BF16)
            rows = slice(r * rs, (r + 1) * rs)
            pt_cur[ci, rows, :] = jnp.exp2(st_cur[ci, rows, :] - mt)
        return tuple(m_new), tuple(alphas)

    def pair(kb, carry):
        ms, alphas = step(kb, st_b, pt_b, st_a, pt_a, *carry)
        return step(kb + 1, st_a, pt_a, st_b, pt_b, ms, alphas)

    def body(jj, carry):
        kb = 2 * ATTN_PAIRS_PER_ITER * jj + 1
        for r in range(ATTN_PAIRS_PER_ITER):
            carry = pair(kb + 2 * r, carry)
        return carry

    nkb = S // tk
    assert nkb % 2 == 0
    n_pairs = nkb // 2 - 1
    acc_s[...] = jnp.zeros_like(acc_s)
    m0 = tuple(jnp.full((1, tq), -jnp.inf, F32) for _ in chains)
    scores(0, st_a)
    scores(1, st_b)
    carry = softmax(st_a, pt_a, m0)
    carry = lax.fori_loop(0, n_pairs // ATTN_PAIRS_PER_ITER, body, carry)
    for r in range(n_pairs - n_pairs % ATTN_PAIRS_PER_ITER, n_pairs):
        carry = pair(2 * r + 1, carry)
    ms, alphas = carry
    accumulate(nkb - 2, pt_a, alphas)
    ms, alphas = softmax(st_b, pt_b, ms)
    accumulate(nkb - 1, pt_b, alphas)
    res = tuple((None, acc_s[ci]) for ci in range(len(chains)))
    outs = []
    for hh in range(heads_per_step):
        a0 = res[2 * hh][1]
        a1 = res[2 * hh + 1][1]
        ot = (a0[:V_DIM, :] / a0[V_DIM:V_DIM + 1, :]
              - lam * (a1[:V_DIM, :] / a1[V_DIM:V_DIM + 1, :]))
        ot = ot * lax.rsqrt(jnp.mean(ot * ot, axis=0, keepdims=True) + RMS_EPS)
        outs.append(ot)
    o = jnp.concatenate(outs, axis=0).T
    o_ref[0] = (o * sw_ref[...] * (1.0 - LAMBDA_INIT)).astype(o_ref.dtype)


def _attn(lam_p, sw2, qt, k, vt, tq, tk):
    B, H, S, _ = k.shape
    hps = 2
    kern = functools.partial(_attn_kernel, tk=tk, heads_per_step=hps)
    return pl.pallas_call(
        kern,
        grid=(B, H // hps, S // tq),
        in_specs=[pl.BlockSpec((4, QK_DIM), lambda b, h, i: (0, 0)),
                  pl.BlockSpec((1, hps * V_DIM), lambda b, h, i: (0, 0)),
                  pl.BlockSpec((1, hps, 2, QK_CONTRACT, tq), lambda b, h, i: (b, h, 0, 0, i)),
                  pl.BlockSpec((1, hps, S, QK_CONTRACT), lambda b, h, i: (b, h, 0, 0)),
                  pl.BlockSpec((1, hps, VT_ROWS, S), lambda b, h, i: (b, h, 0, 0))],
        out_specs=pl.BlockSpec((1, tq, hps * V_DIM), lambda b, h, i: (b, i, h)),
        out_shape=jax.ShapeDtypeStruct((B, S, W_ATTN), BF16),
        scratch_shapes=[pltpu.VMEM((2 * hps, tk, tq), BF16), pltpu.VMEM((2 * hps, tk, tq), BF16),
                        pltpu.VMEM((2 * hps, tk, tq), BF16), pltpu.VMEM((2 * hps, tk, tq), BF16),
                        pltpu.VMEM((2 * hps, VT_ROWS, tq), F32)],
        compiler_params=_cparams(("arbitrary", "arbitrary", "arbitrary")),
        name="attn",
    )(lam_p, sw2, qt, k, vt)


def _s5_kernel(u_ref, d_ref, bm_ref, cm_ref, tab_ref, y_ref, xr_s, xi_s, xr_b, xi_b, *, rows):
    S = u_ref.shape[1]
    R = rows
    seg = R // SUBLANES
    nchunk = S // R
    P = STATE_LANES

    def cmul_add(ar, ai, xr, xi, br, bi):
        return ar * xr - ai * xi + br, ar * xi + ai * xr + bi

    def scan_chunk(di, c, carry, xr_s, xi_s):
        bm = bm_ref[di, 0]
        cm = cm_ref[di, 0]
        ar = tab_ref[di, 0, 0:1, :]
        ai = tab_ref[di, 0, 1:2, :]
        asr = tab_ref[di, 0, 2:3, :]
        asi = tab_ref[di, 0, 3:4, :]
        steps = list(range(seg)) if di == 0 else list(range(seg - 1, -1, -1))
        segs = list(range(SUBLANES)) if di == 0 else list(range(SUBLANES - 1, -1, -1))
        r0 = pl.multiple_of(c * R, R)
        u = jnp.concatenate([u_ref[0, pl.ds(r0 + j, SUBLANES, stride=seg), :]
                             for j in range(seg)], axis=0)
        bu = jnp.dot(u.astype(BF16), bm, preferred_element_type=F32)
        ar8 = jnp.broadcast_to(ar, (SUBLANES, P))
        ai8 = jnp.broadcast_to(ai, (SUBLANES, P))
        zero8 = jnp.zeros((SUBLANES, P), F32)
        xr, xi = zero8, zero8
        for j in steps:
            xr, xi = cmul_add(ar8, ai8, xr, xi, bu[j * SUBLANES:(j + 1) * SUBLANES, :P],
                              bu[j * SUBLANES:(j + 1) * SUBLANES, P:])
        cr, cim = carry
        ent_r, ent_i = [None] * SUBLANES, [None] * SUBLANES
        for s_ in segs:
            ent_r[s_], ent_i[s_] = cr, cim
            cr, cim = cmul_add(asr, asi, cr, cim, xr[s_:s_ + 1, :], xi[s_:s_ + 1, :])
        xr = jnp.concatenate(ent_r, axis=0)
        xi = jnp.concatenate(ent_i, axis=0)
        for j in steps:
            xr, xi = cmul_add(ar8, ai8, xr, xi, bu[j * SUBLANES:(j + 1) * SUBLANES, :P],
                              bu[j * SUBLANES:(j + 1) * SUBLANES, P:])
            xr_s[j * SUBLANES:(j + 1) * SUBLANES, :] = xr
            xi_s[j * SUBLANES:(j + 1) * SUBLANES, :] = xi
        yc = (jnp.dot(xr_s[...].astype(BF16), cm[:P], preferred_element_type=F32)
              + jnp.dot(xi_s[...].astype(BF16), cm[P:], preferred_element_type=F32))
        return yc, r0, (cr, cim)

    def add_rows(r0, yc):
        for j in range(seg):
            rows_j = pl.ds(r0 + j, SUBLANES, stride=seg)
            y_ref[0, rows_j, :] = y_ref[0, rows_j, :] + yc[j * SUBLANES:(j + 1) * SUBLANES, :]

    assert nchunk % 2 == 0
    y_ref[0] = u_ref[0] * d_ref[0]

    def body(ci, carry):
        cf, cb = carry
        yf, r0f, cf = scan_chunk(0, ci, cf, xr_s, xi_s)
        yb, r0b, cb = scan_chunk(1, nchunk - 1 - ci, cb, xr_b, xi_b)
        add_rows(r0f, yf)
        add_rows(r0b, yb)
        return cf, cb

    zero = (jnp.zeros((1, P), F32), jnp.zeros((1, P), F32))
    lax.fori_loop(0, nchunk, body, (zero, zero))


def _s5(u, d3, bmat, cmat, tab, rows):
    B, S, W = u.shape
    nb = W // LANES
    P = STATE_LANES
    kern = functools.partial(_s5_kernel, rows=rows)
    return pl.pallas_call(
        kern,
        grid=(B, nb),
        in_specs=[pl.BlockSpec((1, S, LANES), lambda b, g: (b, 0, g)),
                  pl.BlockSpec((1, 1, LANES), lambda b, g: (g, 0, 0)),
                  pl.BlockSpec((2, 1, LANES, 2 * P), lambda b, g: (0, g, 0, 0)),
                  pl.BlockSpec((2, 1, 2 * P, LANES), lambda b, g: (0, g, 0, 0)),
                  pl.BlockSpec((2, 1, 4, P), lambda b, g: (0, g, 0, 0))],
        out_specs=pl.BlockSpec((1, S, LANES), lambda b, g: (b, 0, g)),
        out_shape=jax.ShapeDtypeStruct((B, S, W), F32),
        scratch_shapes=[pltpu.VMEM((rows, P), F32) for _ in range(4)],
        compiler_params=_cparams(("arbitrary", "arbitrary")),
        name="s5",
    )(u, d3, bmat, cmat, tab)


def _s5_tables(lam_re, lam_im, log_step, b_re, b_im, cm_re, cm_im, seg):
    nb = S5_GROUPS // GROUPS_PER_BLOCK
    eye = jnp.eye(GROUPS_PER_BLOCK, dtype=F32)
    bmats, cmats, tabs = [], [], []
    for di in range(2):
        lr = lam_re[di].astype(F32)
        li = lam_im[di].astype(F32)
        delta = jnp.exp(log_step[di].astype(F32))[:, None]
        mag = jnp.exp(lr * delta)
        a_re = mag * jnp.cos(li * delta)
        a_im = mag * jnp.sin(li * delta)
        den = lr * lr + li * li
        num_re = a_re - 1.0
        f_re = (num_re * lr + a_im * li) / den
        f_im = (a_im * lr - num_re * li) / den
        br = b_re[di].astype(F32)
        bi = b_im[di].astype(F32)
        bbar_re = f_re[..., None] * br - f_im[..., None] * bi
        bbar_im = f_re[..., None] * bi + f_im[..., None] * br

        def blockdiag_in(bb):
            bb = bb.reshape(nb, GROUPS_PER_BLOCK, S5_STATE, S5_CH)
            m = jnp.einsum('bgpc,gh->bgchp', bb, eye)
            return m.reshape(nb, LANES, STATE_LANES)

        def blockdiag_out(cc):
            cc = cc.reshape(nb, GROUPS_PER_BLOCK, S5_CH, S5_STATE)
            m = jnp.einsum('bgcp,gh->bgphc', cc, eye)
            return m.reshape(nb, STATE_LANES, LANES)

        bmats.append(jnp.concatenate([blockdiag_in(bbar_re), blockdiag_in(bbar_im)], axis=2))
        cmats.append(jnp.concatenate([blockdiag_out(cm_re[di].astype(F32)),
                                      -blockdiag_out(cm_im[di].astype(F32))], axis=1))
        sr, si = a_re, a_im
        n = 1
        while n < seg:
            sr, si = sr * sr - si * si, 2.0 * sr * si
            n *= 2
        assert n == seg
        tabs.append(jnp.stack([a_re, a_im, sr, si]).reshape(4, nb, STATE_LANES)
                    .transpose(1, 0, 2))
    return jnp.stack(bmats).astype(BF16), jnp.stack(cmats).astype(BF16), jnp.stack(tabs)


def _post_kernel(a_ref, ys_ref, x_ref, mod_ref, gluw_ref, glub_ref, onw_ref, wout_ref, n2w_ref,
                 rw_ref, rb_ref, x1_ref, h2_ref, eidx_ref, gate_ref, rank_ref, cnt_ref, carry_s):
    first = (pl.program_id(0) == 0) & (pl.program_id(1) == 0)

    @pl.when(first)
    def _():
        carry_s[...] = jnp.zeros_like(carry_s)

    tm = x_ref.shape[1]
    y = ys_ref[0]
    y = 0.5 * y * (1.0 + jnp.tanh(math.sqrt(2.0 / math.pi) * (y + 0.044715 * (y * y * y))))
    g = jnp.dot(y.astype(BF16), gluw_ref[...], preferred_element_type=F32) + glub_ref[...]
    y = y * jax.nn.sigmoid(g)
    s = _rms(y) * onw_ref[...]
    mix = (jnp.dot(a_ref[0], wout_ref[0:W_ATTN, :], preferred_element_type=F32)
           + jnp.dot(s.astype(BF16), wout_ref[W_ATTN:, :], preferred_element_type=F32))
    g1 = mod_ref[0, 2:3, :]
    sh2 = mod_ref[0, 3:4, :]
    sc2 = mod_ref[0, 4:5, :]
    x1 = x_ref[0] + g1 * mix
    x1_ref[0] = x1
    h2 = (_rms(x1) * n2w_ref[...]) * (1.0 + sc2) + sh2
    h2_ref[0] = h2
    logits = jnp.dot(h2, rw_ref[...], preferred_element_type=F32,
                     precision=lax.Precision.HIGHEST) + rb_ref[...]
    lane = lax.broadcasted_iota(jnp.int32, (tm, LANES), 1)
    neg = jnp.float32(-jnp.inf)
    work = jnp.where(lane < N_EXPERTS, logits, neg)
    vals, idxs = [], []
    for _ in range(TOP_K):
        m = jnp.max(work, axis=-1, keepdims=True)
        idx = jnp.min(jnp.where(work == m, lane, LANES), axis=-1, keepdims=True)
        vals.append(m)
        idxs.append(idx)
        work = jnp.where(lane == idx, neg, work)
    es = [jnp.exp(v - vals[0]) for v in vals]
    den = es[0] + es[1] + es[2] + es[3]
    onehot = jnp.zeros((tm, LANES), F32)
    for idx in idxs:
        onehot = onehot + jnp.where(lane == idx, 1.0, 0.0)
    r_i = lax.broadcasted_iota(jnp.int32, (tm, tm), 0)
    c_i = lax.broadcasted_iota(jnp.int32, (tm, tm), 1)
    ltri = jnp.where(c_i < r_i, 1.0, 0.0).astype(BF16)
    before = jnp.dot(ltri, onehot.astype(BF16), preferred_element_type=F32) + carry_s[...]
    lane4 = lax.broadcasted_iota(jnp.int32, (tm, TOP_K), 1)
    e_out = jnp.zeros((tm, TOP_K), jnp.int32)
    g_out = jnp.zeros((tm, TOP_K), F32)
    r_out = jnp.zeros((tm, TOP_K), jnp.int32)
    for j in range(TOP_K):
        rk = jnp.sum(jnp.where(lane == idxs[j], before, 0.0), axis=-1, keepdims=True)
        e_out = jnp.where(lane4 == j, idxs[j], e_out)
        g_out = jnp.where(lane4 == j, es[j] / den, g_out)
        r_out = jnp.where(lane4 == j, rk.astype(jnp.int32), r_out)
    eidx_ref[0] = e_out
    gate_ref[0] = g_out
    rank_ref[0] = r_out
    carry_s[...] = carry_s[...] + jnp.sum(onehot, axis=0, keepdims=True)
    cnt_ref[...] = carry_s[...]


def _post(a, ys, x, mod3, gluw, glub, onw, wout, n2w, rw, rb, tm):
    B, S, D = x.shape
    c2 = lambda b, i: (0, 0)
    tok = lambda b, i: (b, i, 0)
    return pl.pallas_call(
        _post_kernel,
        grid=(B, S // tm),
        in_specs=[pl.BlockSpec((1, tm, W_ATTN), tok),
                  pl.BlockSpec((1, tm, W_S5), tok),
                  pl.BlockSpec((1, tm, D), tok),
                  pl.BlockSpec((1, 6, D), lambda b, i: (b, 0, 0)),
                  pl.BlockSpec((W_S5, W_S5), c2),
                  pl.BlockSpec((1, W_S5), c2),
                  pl.BlockSpec((1, W_S5), c2),
                  pl.BlockSpec((D, D), c2),
                  pl.BlockSpec((1, D), c2),
                  pl.BlockSpec((D, LANES), c2),
                  pl.BlockSpec((1, LANES), c2)],
        out_specs=[pl.BlockSpec((1, tm, D), tok),
                   pl.BlockSpec((1, tm, D), tok),
                   pl.BlockSpec((1, tm, TOP_K), tok),
                   pl.BlockSpec((1, tm, TOP_K), tok),
                   pl.BlockSpec((1, tm, TOP_K), tok),
                   pl.BlockSpec((1, LANES), c2)],
        out_shape=[jax.ShapeDtypeStruct((B, S, D), F32),
                   jax.ShapeDtypeStruct((B, S, D), F32),
                   jax.ShapeDtypeStruct((B, S, TOP_K), jnp.int32),
                   jax.ShapeDtypeStruct((B, S, TOP_K), F32),
                   jax.ShapeDtypeStruct((B, S, TOP_K), jnp.int32),
                   jax.ShapeDtypeStruct((1, LANES), F32)],
        scratch_shapes=[pltpu.VMEM((1, LANES), F32)],
        compiler_params=_cparams(("arbitrary", "arbitrary")),
        name="post",
    )(a, ys, x, mod3, gluw, glub, onw, wout, n2w, rw, rb)


def _dispatch_kernel(pend_ref, padded_ref, dest_ref, h2_ref, xs_hbm, zero_s, sem, zsem, *, tg):
    @pl.when(pl.program_id(0) == 0)
    def _():
        zero_s[...] = jnp.zeros_like(zero_s)

        def zero_copy(e):
            start = pl.multiple_of(pend_ref[e] - MOE_ROWS, MOE_ROWS)
            return pltpu.make_async_copy(zero_s, xs_hbm.at[pl.ds(start, MOE_ROWS)], zsem)

        for e in range(N_EXPERTS):
            @pl.when(padded_ref[e] > 0)
            def _():
                zero_copy(e).start()
        for e in range(N_EXPERTS):
            @pl.when(padded_ref[e] > 0)
            def _():
                zero_copy(e).wait()

    def row_copy(t, d):
        return pltpu.make_async_copy(h2_ref.at[pl.ds(t, 1)], xs_hbm.at[pl.ds(d, 1)], sem)

    def issue(t, _):
        for j in range(TOP_K):
            row_copy(t, dest_ref[0, 0, t * TOP_K + j]).start()
        return 0

    lax.fori_loop(0, tg, issue, 0, unroll=2)
    for _ in range(TOP_K):
        pltpu.make_async_copy(h2_ref, xs_hbm.at[pl.ds(0, tg)], sem).wait()


def _dispatch(pad_end, padded, dest3, h2, n_rows, tg):
    T, D = h2.shape
    kern = functools.partial(_dispatch_kernel, tg=tg)
    grid_spec = pltpu.PrefetchScalarGridSpec(
        num_scalar_prefetch=2,
        grid=(T // tg,),
        in_specs=[pl.BlockSpec((1, 1, tg * TOP_K), lambda i, pe, pd: (i, 0, 0),
                               memory_space=pltpu.SMEM),
                  pl.BlockSpec((tg, D), lambda i, pe, pd: (i, 0))],
        out_specs=pl.BlockSpec(memory_space=pl.ANY),
        scratch_shapes=[pltpu.VMEM((MOE_ROWS, D), F32), pltpu.SemaphoreType.DMA(()),
                        pltpu.SemaphoreType.DMA(())],
    )
    return pl.pallas_call(
        kern,
        grid_spec=grid_spec,
        out_shape=jax.ShapeDtypeStruct((n_rows, D), F32),
        compiler_params=_cparams(("arbitrary",)),
        name="dispatch",
    )(pad_end, padded, dest3, h2)


PERM_CHUNK = 2 * LANES


def _expert_kernel(be_ref, na_ref, x_ref, w1_ref, bg_ref, bl_ref, w2_ref, b2_ref, perm_ref, y_ref,
                   wg_s, wl_s, w2_s):
    i = pl.program_id(0)
    active = i < na_ref[0]
    prev = be_ref[jnp.maximum(i - 1, 0)]
    changed = active & ((i == 0) | (be_ref[i] != prev))

    @pl.when(changed)
    def _():
        perm = perm_ref[...]
        for j in range(2 * D_FF // PERM_CHUNK):
            chunk = w1_ref[0, :, j * PERM_CHUNK:(j + 1) * PERM_CHUNK].astype(BF16)
            sep = jnp.dot(chunk, perm, preferred_element_type=F32).astype(BF16)
            wg_s[:, j * LANES:(j + 1) * LANES] = sep[:, :LANES]
            wl_s[:, j * LANES:(j + 1) * LANES] = sep[:, LANES:]
        w2_s[...] = w2_ref[0].astype(BF16)

    @pl.when(active)
    def _():
        x = x_ref[...].astype(BF16)
        zg = jnp.dot(x, wg_s[...], preferred_element_type=F32) + bg_ref[0]
        zl = jnp.dot(x, wl_s[...], preferred_element_type=F32) + bl_ref[0]
        xg = jnp.minimum(zg, SWIGLU_LIMIT)
        xl = jnp.clip(zl, -SWIGLU_LIMIT, SWIGLU_LIMIT)
        act = xg * jax.nn.sigmoid(SWIGLU_ALPHA * xg) * (xl + 1.0)
        y_ref[...] = jnp.dot(act.astype(BF16), w2_s[...], preferred_element_type=F32) + b2_ref[0]


def _experts(block_e, n_active, xs, w1, bg, bl, w2, b2, perm):
    n_rows, D = xs.shape
    nblk = n_rows // MOE_ROWS

    def row_map(i, be, na):
        return (jnp.minimum(i, na[0] - 1), 0)

    def w_map(i, be, na):
        return (be[i], 0, 0)

    grid_spec = pltpu.PrefetchScalarGridSpec(
        num_scalar_prefetch=2,
        grid=(nblk,),
        in_specs=[pl.BlockSpec((MOE_ROWS, D), row_map),
                  pl.BlockSpec((1, D, 2 * D_FF), w_map),
                  pl.BlockSpec((1, 1, D_FF), w_map),
                  pl.BlockSpec((1, 1, D_FF), w_map),
                  pl.BlockSpec((1, D_FF, D), w_map),
                  pl.BlockSpec((1, 1, D), w_map),
                  pl.BlockSpec((PERM_CHUNK, PERM_CHUNK), lambda i, be, na: (0, 0))],
        out_specs=pl.BlockSpec((MOE_ROWS, D), row_map),
        scratch_shapes=[pltpu.VMEM((D, D_FF), BF16), pltpu.VMEM((D, D_FF), BF16),
                        pltpu.VMEM((D_FF, D), BF16)],
    )
    return pl.pallas_call(
        _expert_kernel,
        grid_spec=grid_spec,
        out_shape=jax.ShapeDtypeStruct((n_rows, D), F32),
        compiler_params=_cparams(("arbitrary",)),
        name="experts",
    )(block_e, n_active, xs, w1, bg, bl, w2, b2, perm)


def _combine_kernel(dest_ref, destn_ref, gate_ref, x1_ref, mod_ref, ys_hbm, o_ref, buf, sem, *, tc):
    i = pl.program_id(0)
    slot = i % 2

    def issue_all(idx_ref, sl):
        def issue(t, _):
            for j in range(TOP_K):
                d = idx_ref[0, 0, t * TOP_K + j]
                pltpu.make_async_copy(ys_hbm.at[pl.ds(d, 1)], buf.at[sl, j, pl.ds(t, 1)],
                                      sem.at[sl]).start()
            return 0

        lax.fori_loop(0, tc, issue, 0, unroll=2)

    @pl.when(i == 0)
    def _():
        issue_all(dest_ref, 0)

    @pl.when(i + 1 < pl.num_programs(0))
    def _():
        issue_all(destn_ref, 1 - slot)

    for j in range(TOP_K):
        pltpu.make_async_copy(ys_hbm.at[pl.ds(0, tc)], buf.at[slot, j], sem.at[slot]).wait()
    gates = gate_ref[...]
    acc = gates[:, 0:1] * buf[slot, 0]
    for j in range(1, TOP_K):
        acc = acc + gates[:, j:j + 1] * buf[slot, j]
    o_ref[...] = x1_ref[...] + mod_ref[0, 5:6, :] * acc


def _combine(dest3, gates, x1, mod3, ys, tc, tiles_per_batch):
    T, D = x1.shape
    n_tiles = T // tc
    kern = functools.partial(_combine_kernel, tc=tc)
    return pl.pallas_call(
        kern,
        grid=(n_tiles,),
        in_specs=[pl.BlockSpec((1, 1, tc * TOP_K), lambda i: (i, 0, 0), memory_space=pltpu.SMEM),
                  pl.BlockSpec((1, 1, tc * TOP_K), lambda i: (jnp.minimum(i + 1, n_tiles - 1), 0, 0),
                               memory_space=pltpu.SMEM),
                  pl.BlockSpec((tc, TOP_K), lambda i: (i, 0)),
                  pl.BlockSpec((tc, D), lambda i: (i, 0)),
                  pl.BlockSpec((1, 6, D), lambda i: (i // tiles_per_batch, 0, 0)),
                  pl.BlockSpec(memory_space=pl.ANY)],
        out_specs=pl.BlockSpec((tc, D), lambda i: (i, 0)),
        out_shape=jax.ShapeDtypeStruct((T, D), F32),
        scratch_shapes=[pltpu.VMEM((2, TOP_K, tc, D), F32), pltpu.SemaphoreType.DMA((2,))],
        compiler_params=_cparams(("arbitrary",)),
        name="combine",
    )(dest3, dest3, gates, x1, mod3, ys)


def kernel(x, c, positions, ada_w, ada_b, norm1_w, w_in, q_norm_w, k_norm_w, lambda_q1, lambda_k1,
           lambda_q2, lambda_k2, subln_w, s5_lambda_re, s5_lambda_im, s5_log_step, s5_b_re, s5_b_im,
           s5_cmat_re, s5_cmat_im, s5_d, s5_glu_w, s5_glu_b, s5_out_norm_w, w_out, norm2_w,
           router_w, router_b, mlp1_w, mlp1_b, mlp2_w, mlp2_b):
    B, S, D = x.shape
    T = B * S
    l = 0

    c_pad = jnp.pad(c, ((0, SUBLANES - B % SUBLANES if B % SUBLANES else 0), (0, 0)))
    mod = _ada(c_pad, ada_w[l], ada_b[l][None, :])[:B]
    mod3 = mod.reshape(B, 6, D)

    tm = min(512, S)
    inv_freq = ROPE_THETA ** (-jnp.arange(0, ROT_DIM, 2, dtype=F32) / ROT_DIM)
    d_in_head = jnp.arange(LANES) % QK_DIM
    invf = jnp.where(d_in_head < ROT_DIM, inv_freq[d_in_head % (ROT_DIM // 2)], 0.0)[None, :]
    pos3 = positions.astype(F32)[..., None]
    gmat = jnp.kron(jnp.eye(W_QK // QK_DIM, dtype=F32), jnp.ones((QK_DIM, QK_DIM), F32)).astype(BF16)
    q_bound = jnp.max(jnp.abs(q_norm_w[l])) * (math.log2(math.e) / math.sqrt(QK_DIM))
    k_bound = jnp.max(jnp.abs(k_norm_w[l]))
    shift = jnp.exp2(jnp.floor(0.5 * jnp.log2(jnp.maximum(q_bound, 1e-30)
                                              / jnp.maximum(k_bound, 1e-30))))
    qw = jnp.tile(q_norm_w[l], W_QK // QK_DIM)[None, :] / shift
    kw = jnp.tile(k_norm_w[l], W_QK // QK_DIM)[None, :] * shift
    qt, k, vt, u = _inproj(x, mod3, norm1_w[l][None, :], pos3, invf, w_in[l].astype(BF16),
                          qw, kw, gmat, tm)

    lam_p = jnp.stack([lambda_q1[l], lambda_k1[l], lambda_q2[l], lambda_k2[l]]).astype(F32)
    sw2 = jnp.tile(subln_w[l], 2)[None, :]
    a_out = _attn(lam_p, sw2, qt, k, vt, tq=min(512, S), tk=min(256, S // 2))

    s5_rows = min(256, S)
    bmat, cmat, tab = _s5_tables(s5_lambda_re[l], s5_lambda_im[l], s5_log_step[l], s5_b_re[l],
                                 s5_b_im[l], s5_cmat_re[l], s5_cmat_im[l], s5_rows // SUBLANES)
    d3 = s5_d[l].astype(F32).reshape(W_S5 // LANES, 1, LANES)
    y_s5 = _s5(u, d3, bmat, cmat, tab, rows=s5_rows)

    rw = jnp.pad(router_w[l], ((0, 0), (0, LANES - N_EXPERTS)))
    rb = jnp.pad(router_b[l], (0, LANES - N_EXPERTS))[None, :]
    tp = min(256, S)
    x1, h2, eidx, gates, rank, counts = _post(
        a_out, y_s5, x, mod3, s5_glu_w[l].astype(BF16), s5_glu_b[l][None, :],
        s5_out_norm_w[l][None, :], w_out[l].astype(BF16), norm2_w[l][None, :], rw, rb, tp)

    counts = counts[0, :N_EXPERTS].astype(jnp.int32)
    padded = ((counts + MOE_ROWS - 1) // MOE_ROWS) * MOE_ROWS
    pad_end = jnp.cumsum(padded)
    pad_start = pad_end - padded
    eflat = eidx.reshape(T * TOP_K)
    dest = pad_start[eflat] + rank.reshape(T * TOP_K)
    n_rows = T * TOP_K + N_EXPERTS * MOE_ROWS
    nblk = n_rows // MOE_ROWS
    blk_row = jnp.arange(nblk, dtype=jnp.int32)[:, None] * MOE_ROWS
    block_e = jnp.minimum(jnp.sum((blk_row >= pad_end[None, :]).astype(jnp.int32), axis=1),
                          N_EXPERTS - 1)
    n_active = (pad_end[-1] // MOE_ROWS).astype(jnp.int32)[None]

    tg = min(1024, T)
    xs = _dispatch(pad_end.astype(jnp.int32), padded, dest.reshape(T // tg, 1, tg * TOP_K),
                   h2.reshape(T, D), n_rows, tg)

    bg = mlp1_b[l][:, None, 0::2]
    bl = mlp1_b[l][:, None, 1::2]
    src = jnp.arange(PERM_CHUNK)
    perm = (jnp.arange(PERM_CHUNK)[None, :] == ((src % 2) * LANES + src // 2)[:, None]).astype(BF16)
    ys = _experts(block_e, n_active, xs, mlp1_w[l], bg, bl, mlp2_w[l], mlp2_b[l][:, None, :], perm)

    tc = min(256, S)
    out = _combine(dest.reshape(T // tc, 1, tc * TOP_K), gates.reshape(T, TOP_K),
                   x1.reshape(T, D), mod3, ys, tc, S // tc)
    return out.reshape(B, S, D)
```

```python
import functools
import math

import jax
import jax.numpy as jnp
from jax import lax
from jax.experimental import pallas as pl
from jax.experimental.pallas import tpu as pltpu

F32 = jnp.float32
BF16 = jnp.bfloat16

D_MODEL = 1024
QK_DIM = 32
V_DIM = 64
N_HEADS = 8
W_QK = N_HEADS * 2 * QK_DIM
W_ATTN = N_HEADS * V_DIM
ROT_DIM = QK_DIM // 4
ROPE_THETA = 500000.0
S5_CH = 16
S5_STATE = 64
W_S5 = 512
S5_GROUPS = W_S5 // S5_CH
D_IN_PROJ = 2 * W_QK + W_ATTN + W_S5
N_EXPERTS = 32
TOP_K = 4
D_FF = D_MODEL
SWIGLU_ALPHA = 1.702
SWIGLU_LIMIT = 7.0
RMS_EPS = 1e-6
LAMBDA_INIT = 0.8 - 0.6 * math.exp(-0.3 * 0)

LANES = 128
SUBLANES = 8
VMEM_LIMIT = 56 * 1024 * 1024

GROUPS_PER_BLOCK = LANES // S5_CH
STATE_LANES = GROUPS_PER_BLOCK * S5_STATE
MOE_ROWS = 512
VT_ROWS = V_DIM + 16
ATTN_PAIRS_PER_ITER = 2
EXP_SLICES = 8
F8 = jnp.float8_e4m3fn
F8_MAX = 448.0
QK_CONTRACT = 4 * 2 * QK_DIM


def _cparams(sem):
    return pltpu.CompilerParams(dimension_semantics=sem, vmem_limit_bytes=VMEM_LIMIT)


def _rms(x, eps=RMS_EPS):
    return x * lax.rsqrt(jnp.mean(x * x, axis=-1, keepdims=True) + eps)


def _ada_kernel(c_ref, w_ref, b_ref, o_ref):
    c = c_ref[...]
    ca = c * jax.nn.sigmoid(c)
    o_ref[...] = jnp.dot(ca, w_ref[...], preferred_element_type=F32,
                         precision=lax.Precision.HIGHEST) + b_ref[...]


def _ada(c_pad, w, b):
    rows, d = c_pad.shape
    n = w.shape[1]
    tn = 1536
    return pl.pallas_call(
        _ada_kernel,
        grid=(n // tn,),
        in_specs=[pl.BlockSpec((rows, d), lambda j: (0, 0)),
                  pl.BlockSpec((d, tn), lambda j: (0, j)),
                  pl.BlockSpec((1, tn), lambda j: (0, j))],
        out_specs=pl.BlockSpec((rows, tn), lambda j: (0, j)),
        out_shape=jax.ShapeDtypeStruct((rows, n), F32),
        compiler_params=_cparams(("arbitrary",)),
        name="ada",
    )(c_pad, w, b)


def _inproj_kernel(x_ref, mod_ref, n1w_ref, pos_ref, invf_ref, win_ref, qw_ref, kw_ref, gm_ref,
                   qt_ref, k_ref, vt_ref, u_ref, *, q_scale):
    x = x_ref[0]
    tm = x.shape[0]
    sh1 = mod_ref[0, 0:1, :]
    sc1 = mod_ref[0, 1:2, :]
    h = (_rms(x) * n1w_ref[...]) * (1.0 + sc1) + sh1
    proj = jnp.dot(h.astype(BF16), win_ref[...], preferred_element_type=F32)

    ang = pos_ref[0] * invf_ref[...]
    cos = jnp.cos(ang)
    sin = jnp.sin(ang)
    d_in_head = lax.broadcasted_iota(jnp.int32, (1, LANES), 1) % QK_DIM
    half = ROT_DIM // 2
    s_lo = jnp.where(d_in_head < half, -sin, 0.0)
    s_hi = jnp.where((d_in_head >= half) & (d_in_head < ROT_DIM), sin, 0.0)
    reps = W_QK // LANES
    cos = jnp.concatenate([cos] * reps, axis=1)
    s_lo = jnp.concatenate([s_lo] * reps, axis=1)
    s_hi = jnp.concatenate([s_hi] * reps, axis=1)

    def qk_norm_rope(t, w_ref, scale):
        ssq = jnp.dot((t * t).astype(BF16), gm_ref[...], preferred_element_type=F32)
        tn = t * lax.rsqrt(ssq * (1.0 / QK_DIM) + RMS_EPS) * w_ref[...]
        r = (tn * cos + pltpu.roll(tn, W_QK - half, 1) * s_lo + pltpu.roll(tn, half, 1) * s_hi)
        return r * scale

    q = qk_norm_rope(proj[:, 0:W_QK], qw_ref, q_scale)
    k = qk_norm_rope(proj[:, W_QK:2 * W_QK], kw_ref, 1.0)
    v = proj[:, 2 * W_QK:2 * W_QK + W_ATTN]
    u_ref[0] = proj[:, 2 * W_QK + W_ATTN:]
    qt = q.T
    vt = v.T
    hw = 2 * QK_DIM

    def split8(x):
        xc = jnp.clip(x, -F8_MAX, F8_MAX)
        hi = xc.astype(F8)
        return hi, (xc - hi.astype(F32)).astype(F8)

    k_hi, k_lo = split8(k)
    map_of_row = (lax.broadcasted_iota(jnp.int32, (W_QK, tm), 0) // QK_DIM) % 2
    q_parts = [split8(jnp.where(map_of_row == c, qt, 0.0)) for c in range(2)]
    tail = jnp.where(lax.broadcasted_iota(jnp.int32, (VT_ROWS - V_DIM, tm), 0) == 0,
                     1.0, 0.0).astype(BF16)
    zk = jnp.zeros((tm, hw), F8)
    zq = jnp.zeros((hw, tm), F8)
    for hd in range(N_HEADS):
        cols = slice(hd * hw, (hd + 1) * hw)
        k_ref[0, hd, :, 0 * hw:1 * hw] = k_hi[:, cols]
        k_ref[0, hd, :, 1 * hw:2 * hw] = k_lo[:, cols]
        k_ref[0, hd, :, 2 * hw:3 * hw] = k_hi[:, cols]
        k_ref[0, hd, :, 3 * hw:4 * hw] = zk
        for c in range(2):
            q_hi, q_lo = q_parts[c]
            qt_ref[0, hd, c, 0 * hw:1 * hw, :] = q_hi[cols, :]
            qt_ref[0, hd, c, 1 * hw:2 * hw, :] = q_hi[cols, :]
            qt_ref[0, hd, c, 2 * hw:3 * hw, :] = q_lo[cols, :]
            qt_ref[0, hd, c, 3 * hw:4 * hw, :] = zq
        vt_ref[0, hd, 0:V_DIM, :] = vt[hd * V_DIM:(hd + 1) * V_DIM, :].astype(BF16)
        vt_ref[0, hd, V_DIM:, :] = tail


def _inproj(x, mod3, n1w, pos3, invf, win_bf, qw, kw, gmat, tm):
    B, S, D = x.shape
    q_scale = math.log2(math.e) / math.sqrt(QK_DIM)
    kern = functools.partial(_inproj_kernel, q_scale=q_scale)
    const2 = lambda b, i: (0, 0)
    return pl.pallas_call(
        kern,
        grid=(B, S // tm),
        in_specs=[pl.BlockSpec((1, tm, D), lambda b, i: (b, i, 0)),
                  pl.BlockSpec((1, 6, D), lambda b, i: (b, 0, 0)),
                  pl.BlockSpec((1, D), const2),
                  pl.BlockSpec((1, tm, 1), lambda b, i: (b, i, 0)),
                  pl.BlockSpec((1, LANES), const2),
                  pl.BlockSpec((D, D_IN_PROJ), const2),
                  pl.BlockSpec((1, W_QK), const2),
                  pl.BlockSpec((1, W_QK), const2),
                  pl.BlockSpec((W_QK, W_QK), const2)],
        out_specs=[pl.BlockSpec((1, N_HEADS, 2, QK_CONTRACT, tm), lambda b, i: (b, 0, 0, 0, i)),
                   pl.BlockSpec((1, N_HEADS, tm, QK_CONTRACT), lambda b, i: (b, 0, i, 0)),
                   pl.BlockSpec((1, N_HEADS, VT_ROWS, tm), lambda b, i: (b, 0, 0, i)),
                   pl.BlockSpec((1, tm, W_S5), lambda b, i: (b, i, 0))],
        out_shape=[jax.ShapeDtypeStruct((B, N_HEADS, 2, QK_CONTRACT, S), F8),
                   jax.ShapeDtypeStruct((B, N_HEADS, S, QK_CONTRACT), F8),
                   jax.ShapeDtypeStruct((B, N_HEADS, VT_ROWS, S), BF16),
                   jax.ShapeDtypeStruct((B, S, W_S5), F32)],
        compiler_params=_cparams(("arbitrary", "arbitrary")),
        name="inproj",
    )(x, mod3, n1w, pos3, invf, win_bf, qw, kw, gmat)


def _attn_kernel(lam_ref, sw_ref, qt_ref, k_ref, vt_ref, o_ref, st_a, st_b, pt_a, pt_b, acc_s, *,
                 tk, heads_per_step):
    S = k_ref.shape[2]
    tq = qt_ref.shape[4]
    lp = lam_ref[...]
    lam = (jnp.exp(jnp.sum(lp[0:1] * lp[1:2], axis=-1, keepdims=True))
           - jnp.exp(jnp.sum(lp[2:3] * lp[3:4], axis=-1, keepdims=True)) + LAMBDA_INIT)
    chains = [(hh, c) for hh in range(heads_per_step) for c in range(2)]
    qts = [qt_ref[0, hh, c] for hh, c in chains]

    def scores(kb, st_buf):
        off = pl.multiple_of(kb * tk, tk)
        for ci, ((hh, c), qt) in enumerate(zip(chains, qts)):
            st_buf[ci] = jnp.dot(k_ref[0, hh, pl.ds(off, tk), :], qt,
                                 preferred_element_type=F32).astype(BF16)

    def softmax(st_buf, pt_buf, ms):
        m_new, alphas = [], []
        for ci, m in enumerate(ms):
            mn = jnp.maximum(m, jnp.max(st_buf[ci], axis=0, keepdims=True).astype(F32))
            alphas.append(jnp.exp2(m - mn))
            pt_buf[ci] = jnp.exp2(st_buf[ci] - mn.astype(BF16))
            m_new.append(mn)
        return tuple(m_new), tuple(alphas)

    def accumulate(kb, pt_buf, alphas):
        off = pl.multiple_of(kb * tk, tk)
        for ci, ((hh, c), alpha) in enumerate(zip(chains, alphas)):
            pv = jnp.dot(vt_ref[0, hh, :, pl.ds(off, tk)], pt_buf[ci],
                         preferred_element_type=F32)
            acc_s[ci] = alpha * acc_s[ci] + pv

    def tied_zero(src):
        u = pltpu.bitcast(src, jnp.uint32)
        u = lax.shift_right_logical(lax.shift_right_logical(u, jnp.uint32(16)), jnp.uint32(16))
        return pltpu.bitcast(u, F32)

    n_ch = len(chains)
    rs = tk // EXP_SLICES
    qk_rows = tk // EXP_SLICES
    pv_groups = VT_ROWS // 16
    qk_events = [(ci, r) for ci in range(n_ch) for r in range(EXP_SLICES)]
    pv_events = [(ci, g) for ci in range(n_ch) for g in range(pv_groups)]
    dt_qk, dt_pv = 8.0 * qk_rows / 32.0, 10.0
    t_qk, t_pv = dt_qk * len(qk_events), dt_pv * len(pv_events)

    def event_for(i, n):
        t = (i + 0.5) * (t_qk + t_pv) / n
        if t < t_qk:
            return ("qk",) + qk_events[int(t / dt_qk)]
        return ("pv",) + pv_events[min(int((t - t_qk) / dt_pv), len(pv_events) - 1)]

    def step(kb, st_cur, pt_cur, st_nxt, pt_prev, ms, alphas_prev):
        off_n = pl.multiple_of((kb + 1) * tk, tk)
        off_p = pl.multiple_of((kb - 1) * tk, tk)
        res, pvs = [], []
        for ci, ((hh, c), qt) in enumerate(zip(chains, qts)):
            r = jnp.dot(k_ref[0, hh, pl.ds(off_n, tk), :], qt, preferred_element_type=F32)
            st_nxt[ci] = r.astype(BF16)
            res.append(r)
        for ci, ((hh, c), alpha) in enumerate(zip(chains, alphas_prev)):
            pv = jnp.dot(vt_ref[0, hh, :, pl.ds(off_p, tk)], pt_prev[ci],
                         preferred_element_type=F32)
            acc_s[ci] = alpha * acc_s[ci] + pv
            pvs.append(pv)
        m_new, alphas = [], []
        for ci, m in enumerate(ms):
            mn = jnp.maximum(m, jnp.max(st_cur[ci], axis=0, keepdims=True).astype(F32))
            alphas.append(jnp.exp2(m - mn))
            m_new.append(mn)
        slices = [(ci, r) for r in range(EXP_SLICES) for ci in range(n_ch)]
        for i, (ci, r) in enumerate(slices):
            kind, cj, e = event_for(i, len(slices))
            src = (res[cj][e * qk_rows:e * qk_rows + 1, :] if kind == "qk"
                   else pvs[cj][e * 16:e * 16 + 1, :])
            mt = (m_new[ci] + tied_zero(src)).astype(BF16)
            rows = slice(r * rs, (r + 1) * rs)
            pt_cur[ci, rows, :] = jnp.exp2(st_cur[ci, rows, :] - mt)
        return tuple(m_new), tuple(alphas)

    def pair(kb, carry):
        ms, alphas = step(kb, st_b, pt_b, st_a, pt_a, *carry)
        return step(kb + 1, st_a, pt_a, st_b, pt_b, ms, alphas)

    def body(jj, carry):
        kb = 2 * ATTN_PAIRS_PER_ITER * jj + 1
        for r in range(ATTN_PAIRS_PER_ITER):
            carry = pair(kb + 2 * r, carry)
        return carry

    nkb = S // tk
    assert nkb % 2 == 0
    n_pairs = nkb // 2 - 1
    acc_s[...] = jnp.zeros_like(acc_s)
    m0 = tuple(jnp.full((1, tq), -jnp.inf, F32) for _ in chains)
    scores(0, st_a)
    scores(1, st_b)
    carry = softmax(st_a, pt_a, m0)
    carry = lax.fori_loop(0, n_pairs // ATTN_PAIRS_PER_ITER, body, carry)
    for r in range(n_pairs - n_pairs % ATTN_PAIRS_PER_ITER, n_pairs):
        carry = pair(2 * r + 1, carry)
    ms, alphas = carry
    accumulate(nkb - 2, pt_a, alphas)
    ms, alphas = softmax(st_b, pt_b, ms)
    accumulate(nkb - 1, pt_b, alphas)
    res = tuple((None, acc_s[ci]) for ci in range(len(chains)))
    outs = []
    for hh in range(heads_per_step):
        a0 = res[2 * hh][1]
        a1 = res[2 * hh + 1][1]
        ot = (a0[:V_DIM, :] / a0[V_DIM:V_DIM + 1, :]
              - lam * (a1[:V_DIM, :] / a1[V_DIM:V_DIM + 1, :]))
        ot = ot * lax.rsqrt(jnp.mean(ot * ot, axis=0, keepdims=True) + RMS_EPS)
        outs.append(ot)
    o = jnp.concatenate(outs, axis=0).T
    o_ref[0] = (o * sw_ref[...] * (1.0 - LAMBDA_INIT)).astype(o_ref.dtype)


def _attn(lam_p, sw2, qt, k, vt, tq, tk):
    B, H, S, _ = k.shape
    hps = 2
    kern = functools.partial(_attn_kernel, tk=tk, heads_per_step=hps)
    return pl.pallas_call(
        kern,
        grid=(B, H // hps, S // tq),
        in_specs=[pl.BlockSpec((4, QK_DIM), lambda b, h, i: (0, 0)),
                  pl.BlockSpec((1, hps * V_DIM), lambda b, h, i: (0, 0)),
                  pl.BlockSpec((1, hps, 2, QK_CONTRACT, tq), lambda b, h, i: (b, h, 0, 0, i)),
                  pl.BlockSpec((1, hps, S, QK_CONTRACT), lambda b, h, i: (b, h, 0, 0)),
                  pl.BlockSpec((1, hps, VT_ROWS, S), lambda b, h, i: (b, h, 0, 0))],
        out_specs=pl.BlockSpec((1, tq, hps * V_DIM), lambda b, h, i: (b, i, h)),
        out_shape=jax.ShapeDtypeStruct((B, S, W_ATTN), BF16),
        scratch_shapes=[pltpu.VMEM((2 * hps, tk, tq), BF16), pltpu.VMEM((2 * hps, tk, tq), BF16),
                        pltpu.VMEM((2 * hps, tk, tq), BF16), pltpu.VMEM((2 * hps, tk, tq), BF16),
                        pltpu.VMEM((2 * hps, VT_ROWS, tq), F32)],
        compiler_params=_cparams(("arbitrary", "arbitrary", "arbitrary")),
        name="attn",
    )(lam_p, sw2, qt, k, vt)


def _s5_kernel(u_ref, d_ref, bm_ref, cm_ref, tab_ref, y_ref, xr_s, xi_s, xr_b, xi_b, *, rows):
    S = u_ref.shape[1]
    R = rows
    seg = R // SUBLANES
    nchunk = S // R
    P = STATE_LANES

    def cmul_add(ar, ai, xr, xi, br, bi):
        return ar * xr - ai * xi + br, ar * xi + ai * xr + bi

    def scan_chunk(di, c, carry, xr_s, xi_s):
        bm = bm_ref[di, 0]
        cm = cm_ref[di, 0]
        ar = tab_ref[di, 0, 0:1, :]
        ai = tab_ref[di, 0, 1:2, :]
        asr = tab_ref[di, 0, 2:3, :]
        asi = tab_ref[di, 0, 3:4, :]
        steps = list(range(seg)) if di == 0 else list(range(seg - 1, -1, -1))
        segs = list(range(SUBLANES)) if di == 0 else list(range(SUBLANES - 1, -1, -1))
        r0 = pl.multiple_of(c * R, R)
        u = jnp.concatenate([u_ref[0, pl.ds(r0 + j, SUBLANES, stride=seg), :]
                             for j in range(seg)], axis=0)
        bu = jnp.dot(u.astype(BF16), bm, preferred_element_type=F32)
        ar8 = jnp.broadcast_to(ar, (SUBLANES, P))
        ai8 = jnp.broadcast_to(ai, (SUBLANES, P))
        zero8 = jnp.zeros((SUBLANES, P), F32)
        xr, xi = zero8, zero8
        for j in steps:
            xr, xi = cmul_add(ar8, ai8, xr, xi, bu[j * SUBLANES:(j + 1) * SUBLANES, :P],
                              bu[j * SUBLANES:(j + 1) * SUBLANES, P:])
        cr, cim = carry
        ent_r, ent_i = [None] * SUBLANES, [None] * SUBLANES
        for s_ in segs:
            ent_r[s_], ent_i[s_] = cr, cim
            cr, cim = cmul_add(asr, asi, cr, cim, xr[s_:s_ + 1, :], xi[s_:s_ + 1, :])
        xr = jnp.concatenate(ent_r, axis=0)
        xi = jnp.concatenate(ent_i, axis=0)
        for j in steps:
            xr, xi = cmul_add(ar8, ai8, xr, xi, bu[j * SUBLANES:(j + 1) * SUBLANES, :P],
                              bu[j * SUBLANES:(j + 1) * SUBLANES, P:])
            xr_s[j * SUBLANES:(j + 1) * SUBLANES, :] = xr
            xi_s[j * SUBLANES:(j + 1) * SUBLANES, :] = xi
        yc = (jnp.dot(xr_s[...].astype(BF16), cm[:P], preferred_element_type=F32)
              + jnp.dot(xi_s[...].astype(BF16), cm[P:], preferred_element_type=F32))
        return yc, r0, (cr, cim)

    def add_rows(r0, yc):
        for j in range(seg):
            rows_j = pl.ds(r0 + j, SUBLANES, stride=seg)
            y_ref[0, rows_j, :] = y_ref[0, rows_j, :] + yc[j * SUBLANES:(j + 1) * SUBLANES, :]

    assert nchunk % 2 == 0
    y_ref[0] = u_ref[0] * d_ref[0]

    def body(ci, carry):
        cf, cb = carry
        yf, r0f, cf = scan_chunk(0, ci, cf, xr_s, xi_s)
        yb, r0b, cb = scan_chunk(1, nchunk - 1 - ci, cb, xr_b, xi_b)
        add_rows(r0f, yf)
        add_rows(r0b, yb)
        return cf, cb

    zero = (jnp.zeros((1, P), F32), jnp.zeros((1, P), F32))
    lax.fori_loop(0, nchunk, body, (zero, zero))


def _s5(u, d3, bmat, cmat, tab, rows):
    B, S, W = u.shape
    nb = W // LANES
    P = STATE_LANES
    kern = functools.partial(_s5_kernel, rows=rows)
    return pl.pallas_call(
        kern,
        grid=(B, nb),
        in_specs=[pl.BlockSpec((1, S, LANES), lambda b, g: (b, 0, g)),
                  pl.BlockSpec((1, 1, LANES), lambda b, g: (g, 0, 0)),
                  pl.BlockSpec((2, 1, LANES, 2 * P), lambda b, g: (0, g, 0, 0)),
                  pl.BlockSpec((2, 1, 2 * P, LANES), lambda b, g: (0, g, 0, 0)),
                  pl.BlockSpec((2, 1, 4, P), lambda b, g: (0, g, 0, 0))],
        out_specs=pl.BlockSpec((1, S, LANES), lambda b, g: (b, 0, g)),
        out_shape=jax.ShapeDtypeStruct((B, S, W), F32),
        scratch_shapes=[pltpu.VMEM((rows, P), F32) for _ in range(4)],
        compiler_params=_cparams(("arbitrary", "arbitrary")),
        name="s5",
    )(u, d3, bmat, cmat, tab)


def _s5_tables(lam_re, lam_im, log_step, b_re, b_im, cm_re, cm_im, seg):
    nb = S5_GROUPS // GROUPS_PER_BLOCK
    eye = jnp.eye(GROUPS_PER_BLOCK, dtype=F32)
    bmats, cmats, tabs = [], [], []
    for di in range(2):
        lr = lam_re[di].astype(F32)
        li = lam_im[di].astype(F32)
        delta = jnp.exp(log_step[di].astype(F32))[:, None]
        mag = jnp.exp(lr * delta)
        a_re = mag * jnp.cos(li * delta)
        a_im = mag * jnp.sin(li * delta)
        den = lr * lr + li * li
        num_re = a_re - 1.0
        f_re = (num_re * lr + a_im * li) / den
        f_im = (a_im * lr - num_re * li) / den
        br = b_re[di].astype(F32)
        bi = b_im[di].astype(F32)
        bbar_re = f_re[..., None] * br - f_im[..., None] * bi
        bbar_im = f_re[..., None] * bi + f_im[..., None] * br

        def blockdiag_in(bb):
            bb = bb.reshape(nb, GROUPS_PER_BLOCK, S5_STATE, S5_CH)
            m = jnp.einsum('bgpc,gh->bgchp', bb, eye)
            return m.reshape(nb, LANES, STATE_LANES)

        def blockdiag_out(cc):
            cc = cc.reshape(nb, GROUPS_PER_BLOCK, S5_CH, S5_STATE)
            m = jnp.einsum('bgcp,gh->bgphc', cc, eye)
            return m.reshape(nb, STATE_LANES, LANES)

        bmats.append(jnp.concatenate([blockdiag_in(bbar_re), blockdiag_in(bbar_im)], axis=2))
        cmats.append(jnp.concatenate([blockdiag_out(cm_re[di].astype(F32)),
                                      -blockdiag_out(cm_im[di].astype(F32))], axis=1))
        sr, si = a_re, a_im
        n = 1
        while n < seg:
            sr, si = sr * sr - si * si, 2.0 * sr * si
            n *= 2
        assert n == seg
        tabs.append(jnp.stack([a_re, a_im, sr, si]).reshape(4, nb, STATE_LANES)
                    .transpose(1, 0, 2))
    return jnp.stack(bmats).astype(BF16), jnp.stack(cmats).astype(BF16), jnp.stack(tabs)


def _post_kernel(a_ref, ys_ref, x_ref, mod_ref, gluw_ref, glub_ref, onw_ref, wout_ref, n2w_ref,
                 rw_ref, rb_ref, x1_ref, h2_ref, eidx_ref, gate_ref, rank_ref, cnt_ref, carry_s):
    first = (pl.program_id(0) == 0) & (pl.program_id(1) == 0)

    @pl.when(first)
    def _():
        carry_s[...] = jnp.zeros_like(carry_s)

    tm = x_ref.shape[1]
    y = ys_ref[0]
    y = 0.5 * y * (1.0 + jnp.tanh(math.sqrt(2.0 / math.pi) * (y + 0.044715 * (y * y * y))))
    g = jnp.dot(y.astype(BF16), gluw_ref[...], preferred_element_type=F32) + glub_ref[...]
    y = y * jax.nn.sigmoid(g)
    s = _rms(y) * onw_ref[...]
    mix = (jnp.dot(a_ref[0], wout_ref[0:W_ATTN, :], preferred_element_type=F32)
           + jnp.dot(s.astype(BF16), wout_ref[W_ATTN:, :], preferred_element_type=F32))
    g1 = mod_ref[0, 2:3, :]
    sh2 = mod_ref[0, 3:4, :]
    sc2 = mod_ref[0, 4:5, :]
    x1 = x_ref[0] + g1 * mix
    x1_ref[0] = x1
    h2 = (_rms(x1) * n2w_ref[...]) * (1.0 + sc2) + sh2
    h2_ref[0] = h2
    logits = jnp.dot(h2, rw_ref[...], preferred_element_type=F32,
                     precision=lax.Precision.HIGHEST) + rb_ref[...]
    lane = lax.broadcasted_iota(jnp.int32, (tm, LANES), 1)
    neg = jnp.float32(-jnp.inf)
    work = jnp.where(lane < N_EXPERTS, logits, neg)
    vals, idxs = [], []
    for _ in range(TOP_K):
        m = jnp.max(work, axis=-1, keepdims=True)
        idx = jnp.min(jnp.where(work == m, lane, LANES), axis=-1, keepdims=True)
        vals.append(m)
        idxs.append(idx)
        work = jnp.where(lane == idx, neg, work)
    es = [jnp.exp(v - vals[0]) for v in vals]
    den = es[0] + es[1] + es[2] + es[3]
    onehot = jnp.zeros((tm, LANES), F32)
    for idx in idxs:
        onehot = onehot + jnp.where(lane == idx, 1.0, 0.0)
    r_i = lax.broadcasted_iota(jnp.int32, (tm, tm), 0)
    c_i = lax.broadcasted_iota(jnp.int32, (tm, tm), 1)
    ltri = jnp.where(c_i < r_i, 1.0, 0.0).astype(BF16)
    before = jnp.dot(ltri, onehot.astype(BF16), preferred_element_type=F32) + carry_s[...]
    lane4 = lax.broadcasted_iota(jnp.int32, (tm, TOP_K), 1)
    e_out = jnp.zeros((tm, TOP_K), jnp.int32)
    g_out = jnp.zeros((tm, TOP_K), F32)
    r_out = jnp.zeros((tm, TOP_K), jnp.int32)
    for j in range(TOP_K):
        rk = jnp.sum(jnp.where(lane == idxs[j], before, 0.0), axis=-1, keepdims=True)
        e_out = jnp.where(lane4 == j, idxs[j], e_out)
        g_out = jnp.where(lane4 == j, es[j] / den, g_out)
        r_out = jnp.where(lane4 == j, rk.astype(jnp.int32), r_out)
    eidx_ref[0] = e_out
    gate_ref[0] = g_out
    rank_ref[0] = r_out
    carry_s[...] = carry_s[...] + jnp.sum(onehot, axis=0, keepdims=True)
    cnt_ref[...] = carry_s[...]


def _post(a, ys, x, mod3, gluw, glub, onw, wout, n2w, rw, rb, tm):
    B, S, D = x.shape
    c2 = lambda b, i: (0, 0)
    tok = lambda b, i: (b, i, 0)
    return pl.pallas_call(
        _post_kernel,
        grid=(B, S // tm),
        in_specs=[pl.BlockSpec((1, tm, W_ATTN), tok),
                  pl.BlockSpec((1, tm, W_S5), tok),
                  pl.BlockSpec((1, tm, D), tok),
                  pl.BlockSpec((1, 6, D), lambda b, i: (b, 0, 0)),
                  pl.BlockSpec((W_S5, W_S5), c2),
                  pl.BlockSpec((1, W_S5), c2),
                  pl.BlockSpec((1, W_S5), c2),
                  pl.BlockSpec((D, D), c2),
                  pl.BlockSpec((1, D), c2),
                  pl.BlockSpec((D, LANES), c2),
                  pl.BlockSpec((1, LANES), c2)],
        out_specs=[pl.BlockSpec((1, tm, D), tok),
                   pl.BlockSpec((1, tm, D), tok),
                   pl.BlockSpec((1, tm, TOP_K), tok),
                   pl.BlockSpec((1, tm, TOP_K), tok),
                   pl.BlockSpec((1, tm, TOP_K), tok),
                   pl.BlockSpec((1, LANES), c2)],
        out_shape=[jax.ShapeDtypeStruct((B, S, D), F32),
                   jax.ShapeDtypeStruct((B, S, D), F32),
                   jax.ShapeDtypeStruct((B, S, TOP_K), jnp.int32),
                   jax.ShapeDtypeStruct((B, S, TOP_K), F32),
                   jax.ShapeDtypeStruct((B, S, TOP_K), jnp.int32),
                   jax.ShapeDtypeStruct((1, LANES), F32)],
        scratch_shapes=[pltpu.VMEM((1, LANES), F32)],
        compiler_params=_cparams(("arbitrary", "arbitrary")),
        name="post",
    )(a, ys, x, mod3, gluw, glub, onw, wout, n2w, rw, rb)


def _dispatch_kernel(pend_ref, padded_ref, dest_ref, h2_ref, xs_hbm, zero_s, sem, zsem, *, tg):
    @pl.when(pl.program_id(0) == 0)
    def _():
        zero_s[...] = jnp.zeros_like(zero_s)

        def zero_copy(e):
            start = pl.multiple_of(pend_ref[e] - MOE_ROWS, MOE_ROWS)
            return pltpu.make_async_copy(zero_s, xs_hbm.at[pl.ds(start, MOE_ROWS)], zsem)

        for e in range(N_EXPERTS):
            @pl.when(padded_ref[e] > 0)
            def _():
                zero_copy(e).start()
        for e in range(N_EXPERTS):
            @pl.when(padded_ref[e] > 0)
            def _():
                zero_copy(e).wait()

    def row_copy(t, d):
        return pltpu.make_async_copy(h2_ref.at[pl.ds(t, 1)], xs_hbm.at[pl.ds(d, 1)], sem)

    def issue(t, _):
        for j in range(TOP_K):
            row_copy(t, dest_ref[0, 0, t * TOP_K + j]).start()
        return 0

    lax.fori_loop(0, tg, issue, 0, unroll=8)
    for _ in range(TOP_K):
        pltpu.make_async_copy(h2_ref, xs_hbm.at[pl.ds(0, tg)], sem).wait()


def _dispatch(pad_end, padded, dest3, h2, n_rows, tg):
    T, D = h2.shape
    kern = functools.partial(_dispatch_kernel, tg=tg)
    grid_spec = pltpu.PrefetchScalarGridSpec(
        num_scalar_prefetch=2,
        grid=(T // tg,),
        in_specs=[pl.BlockSpec((1, 1, tg * TOP_K), lambda i, pe, pd: (i, 0, 0),
                               memory_space=pltpu.SMEM),
                  pl.BlockSpec((tg, D), lambda i, pe, pd: (i, 0))],
        out_specs=pl.BlockSpec(memory_space=pl.ANY),
        scratch_shapes=[pltpu.VMEM((MOE_ROWS, D), F32), pltpu.SemaphoreType.DMA(()),
                        pltpu.SemaphoreType.DMA(())],
    )
    return pl.pallas_call(
        kern,
        grid_spec=grid_spec,
        out_shape=jax.ShapeDtypeStruct((n_rows, D), F32),
        compiler_params=_cparams(("arbitrary",)),
        name="dispatch",
    )(pad_end, padded, dest3, h2)


PERM_CHUNK = 2 * LANES


def _expert_kernel(be_ref, na_ref, x_ref, w1_ref, bg_ref, bl_ref, w2_ref, b2_ref, perm_ref, y_ref,
                   wg_s, wl_s, w2_s):
    i = pl.program_id(0)
    active = i < na_ref[0]
    prev = be_ref[jnp.maximum(i - 1, 0)]
    changed = active & ((i == 0) | (be_ref[i] != prev))

    @pl.when(changed)
    def _():
        perm = perm_ref[...]
        for j in range(2 * D_FF // PERM_CHUNK):
            chunk = w1_ref[0, :, j * PERM_CHUNK:(j + 1) * PERM_CHUNK].astype(BF16)
            sep = jnp.dot(chunk, perm, preferred_element_type=F32).astype(BF16)
            wg_s[:, j * LANES:(j + 1) * LANES] = sep[:, :LANES]
            wl_s[:, j * LANES:(j + 1) * LANES] = sep[:, LANES:]
        w2_s[...] = w2_ref[0].astype(BF16)

    @pl.when(active)
    def _():
        x = x_ref[...].astype(BF16)
        zg = jnp.dot(x, wg_s[...], preferred_element_type=F32) + bg_ref[0]
        zl = jnp.dot(x, wl_s[...], preferred_element_type=F32) + bl_ref[0]
        xg = jnp.minimum(zg, SWIGLU_LIMIT)
        xl = jnp.clip(zl, -SWIGLU_LIMIT, SWIGLU_LIMIT)
        act = xg * jax.nn.sigmoid(SWIGLU_ALPHA * xg) * (xl + 1.0)
        y_ref[...] = jnp.dot(act.astype(BF16), w2_s[...], preferred_element_type=F32) + b2_ref[0]


def _experts(block_e, n_active, xs, w1, bg, bl, w2, b2, perm):
    n_rows, D = xs.shape
    nblk = n_rows // MOE_ROWS

    def row_map(i, be, na):
        return (jnp.minimum(i, na[0] - 1), 0)

    def w_map(i, be, na):
        return (be[i], 0, 0)

    grid_spec = pltpu.PrefetchScalarGridSpec(
        num_scalar_prefetch=2,
        grid=(nblk,),
        in_specs=[pl.BlockSpec((MOE_ROWS, D), row_map),
                  pl.BlockSpec((1, D, 2 * D_FF), w_map),
                  pl.BlockSpec((1, 1, D_FF), w_map),
                  pl.BlockSpec((1, 1, D_FF), w_map),
                  pl.BlockSpec((1, D_FF, D), w_map),
                  pl.BlockSpec((1, 1, D), w_map),
                  pl.BlockSpec((PERM_CHUNK, PERM_CHUNK), lambda i, be, na: (0, 0))],
        out_specs=pl.BlockSpec((MOE_ROWS, D), row_map),
        scratch_shapes=[pltpu.VMEM((D, D_FF), BF16), pltpu.VMEM((D, D_FF), BF16),
                        pltpu.VMEM((D_FF, D), BF16)],
    )
    return pl.pallas_call(
        _expert_kernel,
        grid_spec=grid_spec,
        out_shape=jax.ShapeDtypeStruct((n_rows, D), F32),
        compiler_params=_cparams(("arbitrary",)),
        name="experts",
    )(block_e, n_active, xs, w1, bg, bl, w2, b2, perm)


def _combine_kernel(dest_ref, destn_ref, gate_ref, x1_ref, mod_ref, ys_hbm, o_ref, buf, sem, *, tc):
    i = pl.program_id(0)
    slot = i % 2

    def issue_all(idx_ref, sl):
        def issue(t, _):
            for j in range(TOP_K):
                d = idx_ref[0, 0, t * TOP_K + j]
                pltpu.make_async_copy(ys_hbm.at[pl.ds(d, 1)], buf.at[sl, j, pl.ds(t, 1)],
                                      sem.at[sl]).start()
            return 0

        lax.fori_loop(0, tc, issue, 0, unroll=8)

    @pl.when(i == 0)
    def _():
        issue_all(dest_ref, 0)

    @pl.when(i + 1 < pl.num_programs(0))
    def _():
        issue_all(destn_ref, 1 - slot)

    for j in range(TOP_K):
        pltpu.make_async_copy(ys_hbm.at[pl.ds(0, tc)], buf.at[slot, j], sem.at[slot]).wait()
    gates = gate_ref[...]
    acc = gates[:, 0:1] * buf[slot, 0]
    for j in range(1, TOP_K):
        acc = acc + gates[:, j:j + 1] * buf[slot, j]
    o_ref[...] = x1_ref[...] + mod_ref[0, 5:6, :] * acc


def _combine(dest3, gates, x1, mod3, ys, tc, tiles_per_batch):
    T, D = x1.shape
    n_tiles = T // tc
    kern = functools.partial(_combine_kernel, tc=tc)
    return pl.pallas_call(
        kern,
        grid=(n_tiles,),
        in_specs=[pl.BlockSpec((1, 1, tc * TOP_K), lambda i: (i, 0, 0), memory_space=pltpu.SMEM),
                  pl.BlockSpec((1, 1, tc * TOP_K), lambda i: (jnp.minimum(i + 1, n_tiles - 1), 0, 0),
                               memory_space=pltpu.SMEM),
                  pl.BlockSpec((tc, TOP_K), lambda i: (i, 0)),
                  pl.BlockSpec((tc, D), lambda i: (i, 0)),
                  pl.BlockSpec((1, 6, D), lambda i: (i // tiles_per_batch, 0, 0)),
                  pl.BlockSpec(memory_space=pl.ANY)],
        out_specs=pl.BlockSpec((tc, D), lambda i: (i, 0)),
        out_shape=jax.ShapeDtypeStruct((T, D), F32),
        scratch_shapes=[pltpu.VMEM((2, TOP_K, tc, D), F32), pltpu.SemaphoreType.DMA((2,))],
        compiler_params=_cparams(("arbitrary",)),
        name="combine",
    )(dest3, dest3, gates, x1, mod3, ys)


def kernel(x, c, positions, ada_w, ada_b, norm1_w, w_in, q_norm_w, k_norm_w, lambda_q1, lambda_k1,
           lambda_q2, lambda_k2, subln_w, s5_lambda_re, s5_lambda_im, s5_log_step, s5_b_re, s5_b_im,
           s5_cmat_re, s5_cmat_im, s5_d, s5_glu_w, s5_glu_b, s5_out_norm_w, w_out, norm2_w,
           router_w, router_b, mlp1_w, mlp1_b, mlp2_w, mlp2_b):
    B, S, D = x.shape
    T = B * S
    l = 0

    c_pad = jnp.pad(c, ((0, SUBLANES - B % SUBLANES if B % SUBLANES else 0), (0, 0)))
    mod = _ada(c_pad, ada_w[l], ada_b[l][None, :])[:B]
    mod3 = mod.reshape(B, 6, D)

    tm = min(512, S)
    inv_freq = ROPE_THETA ** (-jnp.arange(0, ROT_DIM, 2, dtype=F32) / ROT_DIM)
    d_in_head = jnp.arange(LANES) % QK_DIM
    invf = jnp.where(d_in_head < ROT_DIM, inv_freq[d_in_head % (ROT_DIM // 2)], 0.0)[None, :]
    pos3 = positions.astype(F32)[..., None]
    gmat = jnp.kron(jnp.eye(W_QK // QK_DIM, dtype=F32), jnp.ones((QK_DIM, QK_DIM), F32)).astype(BF16)
    q_bound = jnp.max(jnp.abs(q_norm_w[l])) * (math.log2(math.e) / math.sqrt(QK_DIM))
    k_bound = jnp.max(jnp.abs(k_norm_w[l]))
    shift = jnp.exp2(jnp.floor(0.5 * jnp.log2(jnp.maximum(q_bound, 1e-30)
                                              / jnp.maximum(k_bound, 1e-30))))
    qw = jnp.tile(q_norm_w[l], W_QK // QK_DIM)[None, :] / shift
    kw = jnp.tile(k_norm_w[l], W_QK // QK_DIM)[None, :] * shift
    qt, k, vt, u = _inproj(x, mod3, norm1_w[l][None, :], pos3, invf, w_in[l].astype(BF16),
                          qw, kw, gmat, tm)

    lam_p = jnp.stack([lambda_q1[l], lambda_k1[l], lambda_q2[l], lambda_k2[l]]).astype(F32)
    sw2 = jnp.tile(subln_w[l], 2)[None, :]
    a_out = _attn(lam_p, sw2, qt, k, vt, tq=min(512, S), tk=min(256, S // 2))

    s5_rows = min(256, S)
    bmat, cmat, tab = _s5_tables(s5_lambda_re[l], s5_lambda_im[l], s5_log_step[l], s5_b_re[l],
                                 s5_b_im[l], s5_cmat_re[l], s5_cmat_im[l], s5_rows // SUBLANES)
    d3 = s5_d[l].astype(F32).reshape(W_S5 // LANES, 1, LANES)
    y_s5 = _s5(u, d3, bmat, cmat, tab, rows=s5_rows)

    rw = jnp.pad(router_w[l], ((0, 0), (0, LANES - N_EXPERTS)))
    rb = jnp.pad(router_b[l], (0, LANES - N_EXPERTS))[None, :]
    tp = min(256, S)
    x1, h2, eidx, gates, rank, counts = _post(
        a_out, y_s5, x, mod3, s5_glu_w[l].astype(BF16), s5_glu_b[l][None, :],
        s5_out_norm_w[l][None, :], w_out[l].astype(BF16), norm2_w[l][None, :], rw, rb, tp)

    counts = counts[0, :N_EXPERTS].astype(jnp.int32)
    padded = ((counts + MOE_ROWS - 1) // MOE_ROWS) * MOE_ROWS
    pad_end = jnp.cumsum(padded)
    pad_start = pad_end - padded
    eflat = eidx.reshape(T * TOP_K)
    dest = pad_start[eflat] + rank.reshape(T * TOP_K)
    n_rows = T * TOP_K + N_EXPERTS * MOE_ROWS
    nblk = n_rows // MOE_ROWS
    blk_row = jnp.arange(nblk, dtype=jnp.int32)[:, None] * MOE_ROWS
    block_e = jnp.minimum(jnp.sum((blk_row >= pad_end[None, :]).astype(jnp.int32), axis=1),
                          N_EXPERTS - 1)
    n_active = (pad_end[-1] // MOE_ROWS).astype(jnp.int32)[None]

    tg = min(1024, T)
    xs = _dispatch(pad_end.astype(jnp.int32), padded, dest.reshape(T // tg, 1, tg * TOP_K),
                   h2.reshape(T, D), n_rows, tg)

    bg = mlp1_b[l][:, None, 0::2]
    bl = mlp1_b[l][:, None, 1::2]
    src = jnp.arange(PERM_CHUNK)
    perm = (jnp.arange(PERM_CHUNK)[None, :] == ((src % 2) * LANES + src // 2)[:, None]).astype(BF16)
    ys = _experts(block_e, n_active, xs, mlp1_w[l], bg, bl, mlp2_w[l], mlp2_b[l][:, None, :], perm)

    tc = min(256, S)
    out = _combine(dest.reshape(T // tc, 1, tc * TOP_K), gates.reshape(T, TOP_K),
                   x1.reshape(T, D), mod3, ys, tc, S // tc)
    return out.reshape(B, S, D)
```

```python
import functools
import math

import jax
import jax.numpy as jnp
from jax import lax
from jax.experimental import pallas as pl
from jax.experimental.pallas import tpu as pltpu

F32 = jnp.float32
BF16 = jnp.bfloat16

D_MODEL = 1024
QK_DIM = 32
V_DIM = 64
N_HEADS = 8
W_QK = N_HEADS * 2 * QK_DIM
W_ATTN = N_HEADS * V_DIM
ROT_DIM = QK_DIM // 4
ROPE_THETA = 500000.0
S5_CH = 16
S5_STATE = 64
W_S5 = 512
S5_GROUPS = W_S5 // S5_CH
D_IN_PROJ = 2 * W_QK + W_ATTN + W_S5
N_EXPERTS = 32
TOP_K = 4
D_FF = D_MODEL
SWIGLU_ALPHA = 1.702
SWIGLU_LIMIT = 7.0
RMS_EPS = 1e-6
LAMBDA_INIT = 0.8 - 0.6 * math.exp(-0.3 * 0)

LANES = 128
SUBLANES = 8
VMEM_LIMIT = 56 * 1024 * 1024

GROUPS_PER_BLOCK = LANES // S5_CH
STATE_LANES = GROUPS_PER_BLOCK * S5_STATE
MOE_ROWS = 512
VT_ROWS = V_DIM + 16
ATTN_PAIRS_PER_ITER = 2
EXP_SLICES = 8
F8 = jnp.float8_e4m3fn
F8_MAX = 448.0
QK_CONTRACT = 4 * 2 * QK_DIM


def _cparams(sem):
    return pltpu.CompilerParams(dimension_semantics=sem, vmem_limit_bytes=VMEM_LIMIT)


def _rms(x, eps=RMS_EPS):
    return x * lax.rsqrt(jnp.mean(x * x, axis=-1, keepdims=True) + eps)


def _ada_kernel(c_ref, w_ref, b_ref, o_ref):
    c = c_ref[...]
    ca = c * jax.nn.sigmoid(c)
    o_ref[...] = jnp.dot(ca, w_ref[...], preferred_element_type=F32,
                         precision=lax.Precision.HIGHEST) + b_ref[...]


def _ada(c_pad, w, b):
    rows, d = c_pad.shape
    n = w.shape[1]
    tn = 1536
    return pl.pallas_call(
        _ada_kernel,
        grid=(n // tn,),
        in_specs=[pl.BlockSpec((rows, d), lambda j: (0, 0)),
                  pl.BlockSpec((d, tn), lambda j: (0, j)),
                  pl.BlockSpec((1, tn), lambda j: (0, j))],
        out_specs=pl.BlockSpec((rows, tn), lambda j: (0, j)),
        out_shape=jax.ShapeDtypeStruct((rows, n), F32),
        compiler_params=_cparams(("arbitrary",)),
        name="ada",
    )(c_pad, w, b)


def _inproj_kernel(x_ref, mod_ref, n1w_ref, pos_ref, invf_ref, win_ref, qw_ref, kw_ref, gm_ref,
                   qt_ref, k_ref, vt_ref, u_ref, *, q_scale):
    x = x_ref[0]
    tm = x.shape[0]
    sh1 = mod_ref[0, 0:1, :]
    sc1 = mod_ref[0, 1:2, :]
    h = (_rms(x) * n1w_ref[...]) * (1.0 + sc1) + sh1
    proj = jnp.dot(h.astype(BF16), win_ref[...], preferred_element_type=F32)

    ang = pos_ref[0] * invf_ref[...]
    cos = jnp.cos(ang)
    sin = jnp.sin(ang)
    d_in_head = lax.broadcasted_iota(jnp.int32, (1, LANES), 1) % QK_DIM
    half = ROT_DIM // 2
    s_lo = jnp.where(d_in_head < half, -sin, 0.0)
    s_hi = jnp.where((d_in_head >= half) & (d_in_head < ROT_DIM), sin, 0.0)
    reps = W_QK // LANES
    cos = jnp.concatenate([cos] * reps, axis=1)
    s_lo = jnp.concatenate([s_lo] * reps, axis=1)
    s_hi = jnp.concatenate([s_hi] * reps, axis=1)

    def qk_norm_rope(t, w_ref, scale):
        ssq = jnp.dot((t * t).astype(BF16), gm_ref[...], preferred_element_type=F32)
        tn = t * lax.rsqrt(ssq * (1.0 / QK_DIM) + RMS_EPS) * w_ref[...]
        r = (tn * cos + pltpu.roll(tn, W_QK - half, 1) * s_lo + pltpu.roll(tn, half, 1) * s_hi)
        return r * scale

    q = qk_norm_rope(proj[:, 0:W_QK], qw_ref, q_scale)
    k = qk_norm_rope(proj[:, W_QK:2 * W_QK], kw_ref, 1.0)
    v = proj[:, 2 * W_QK:2 * W_QK + W_ATTN]
    u_ref[0] = proj[:, 2 * W_QK + W_ATTN:]
    qt = q.T
    vt = v.T
    hw = 2 * QK_DIM

    def split8(x):
        xc = jnp.clip(x, -F8_MAX, F8_MAX)
        hi = xc.astype(F8)
        return hi, (xc - hi.astype(F32)).astype(F8)

    k_hi, k_lo = split8(k)
    map_of_row = (lax.broadcasted_iota(jnp.int32, (W_QK, tm), 0) // QK_DIM) % 2
    q_parts = [split8(jnp.where(map_of_row == c, qt, 0.0)) for c in range(2)]
    tail = jnp.where(lax.broadcasted_iota(jnp.int32, (VT_ROWS - V_DIM, tm), 0) == 0,
                     1.0, 0.0).astype(BF16)
    zk = jnp.zeros((tm, hw), F8)
    zq = jnp.zeros((hw, tm), F8)
    for hd in range(N_HEADS):
        cols = slice(hd * hw, (hd + 1) * hw)
        k_ref[0, hd, :, 0 * hw:1 * hw] = k_hi[:, cols]
        k_ref[0, hd, :, 1 * hw:2 * hw] = k_lo[:, cols]
        k_ref[0, hd, :, 2 * hw:3 * hw] = k_hi[:, cols]
        k_ref[0, hd, :, 3 * hw:4 * hw] = zk
        for c in range(2):
            q_hi, q_lo = q_parts[c]
            qt_ref[0, hd, c, 0 * hw:1 * hw, :] = q_hi[cols, :]
            qt_ref[0, hd, c, 1 * hw:2 * hw, :] = q_hi[cols, :]
            qt_ref[0, hd, c, 2 * hw:3 * hw, :] = q_lo[cols, :]
            qt_ref[0, hd, c, 3 * hw:4 * hw, :] = zq
        vt_ref[0, hd, 0:V_DIM, :] = vt[hd * V_DIM:(hd + 1) * V_DIM, :].astype(BF16)
        vt_ref[0, hd, V_DIM:, :] = tail


def _inproj(x, mod3, n1w, pos3, invf, win_bf, qw, kw, gmat, tm):
    B, S, D = x.shape
    q_scale = math.log2(math.e) / math.sqrt(QK_DIM)
    kern = functools.partial(_inproj_kernel, q_scale=q_scale)
    const2 = lambda b, i: (0, 0)
    return pl.pallas_call(
        kern,
        grid=(B, S // tm),
        in_specs=[pl.BlockSpec((1, tm, D), lambda b, i: (b, i, 0)),
                  pl.BlockSpec((1, 6, D), lambda b, i: (b, 0, 0)),
                  pl.BlockSpec((1, D), const2),
                  pl.BlockSpec((1, tm, 1), lambda b, i: (b, i, 0)),
                  pl.BlockSpec((1, LANES), const2),
                  pl.BlockSpec((D, D_IN_PROJ), const2),
                  pl.BlockSpec((1, W_QK), const2),
                  pl.BlockSpec((1, W_QK), const2),
                  pl.BlockSpec((W_QK, W_QK), const2)],
        out_specs=[pl.BlockSpec((1, N_HEADS, 2, QK_CONTRACT, tm), lambda b, i: (b, 0, 0, 0, i)),
                   pl.BlockSpec((1, N_HEADS, tm, QK_CONTRACT), lambda b, i: (b, 0, i, 0)),
                   pl.BlockSpec((1, N_HEADS, VT_ROWS, tm), lambda b, i: (b, 0, 0, i)),
                   pl.BlockSpec((1, tm, W_S5), lambda b, i: (b, i, 0))],
        out_shape=[jax.ShapeDtypeStruct((B, N_HEADS, 2, QK_CONTRACT, S), F8),
                   jax.ShapeDtypeStruct((B, N_HEADS, S, QK_CONTRACT), F8),
                   jax.ShapeDtypeStruct((B, N_HEADS, VT_ROWS, S), BF16),
                   jax.ShapeDtypeStruct((B, S, W_S5), F32)],
        compiler_params=_cparams(("arbitrary", "arbitrary")),
        name="inproj",
    )(x, mod3, n1w, pos3, invf, win_bf, qw, kw, gmat)


def _attn_kernel(lam_ref, sw_ref, qt_ref, k_ref, vt_ref, o_ref, st_a, st_b, pt_a, pt_b, acc_s, *,
                 tk, heads_per_step):
    S = k_ref.shape[2]
    tq = qt_ref.shape[4]
    lp = lam_ref[...]
    lam = (jnp.exp(jnp.sum(lp[0:1] * lp[1:2], axis=-1, keepdims=True))
           - jnp.exp(jnp.sum(lp[2:3] * lp[3:4], axis=-1, keepdims=True)) + LAMBDA_INIT)
    chains = [(hh, c) for hh in range(heads_per_step) for c in range(2)]
    qts = [qt_ref[0, hh, c] for hh, c in chains]

    def scores(kb, st_buf):
        off = pl.multiple_of(kb * tk, tk)
        for ci, ((hh, c), qt) in enumerate(zip(chains, qts)):
            st_buf[ci] = jnp.dot(k_ref[0, hh, pl.ds(off, tk), :], qt,
                                 preferred_element_type=F32).astype(BF16)

    def softmax(st_buf, pt_buf, ms):
        m_new, alphas = [], []
        for ci, m in enumerate(ms):
            mn = jnp.maximum(m, jnp.max(st_buf[ci], axis=0, keepdims=True).astype(F32))
            alphas.append(jnp.exp2(m - mn))
            pt_buf[ci] = jnp.exp2(st_buf[ci] - mn.astype(BF16))
            m_new.append(mn)
        return tuple(m_new), tuple(alphas)

    def accumulate(kb, pt_buf, alphas):
        off = pl.multiple_of(kb * tk, tk)
        for ci, ((hh, c), alpha) in enumerate(zip(chains, alphas)):
            pv = jnp.dot(vt_ref[0, hh, :, pl.ds(off, tk)], pt_buf[ci],
                         preferred_element_type=F32)
            acc_s[ci] = alpha * acc_s[ci] + pv

    def tied_zero(src):
        u = pltpu.bitcast(src, jnp.uint32)
        u = lax.shift_right_logical(lax.shift_right_logical(u, jnp.uint32(16)), jnp.uint32(16))
        return pltpu.bitcast(u, F32)

    n_ch = len(chains)
    rs = tk // EXP_SLICES
    qk_rows = tk // EXP_SLICES
    pv_groups = VT_ROWS // 16
    qk_events = [(ci, r) for ci in range(n_ch) for r in range(EXP_SLICES)]
    pv_events = [(ci, g) for ci in range(n_ch) for g in range(pv_groups)]
    dt_qk, dt_pv = 8.0 * qk_rows / 32.0, 10.0
    t_qk, t_pv = dt_qk * len(qk_events), dt_pv * len(pv_events)

    def event_for(i, n):
        t = (i + 0.5) * (t_qk + t_pv) / n
        if t < t_qk:
            return ("qk",) + qk_events[int(t / dt_qk)]
        return ("pv",) + pv_events[min(int((t - t_qk) / dt_pv), len(pv_events) - 1)]

    def step(kb, st_cur, pt_cur, st_nxt, pt_prev, ms, alphas_prev):
        off_n = pl.multiple_of((kb + 1) * tk, tk)
        off_p = pl.multiple_of((kb - 1) * tk, tk)
        res, pvs = [], []
        for ci, ((hh, c), qt) in enumerate(zip(chains, qts)):
            r = jnp.dot(k_ref[0, hh, pl.ds(off_n, tk), :], qt, preferred_element_type=F32)
            st_nxt[ci] = r.astype(BF16)
            res.append(r)
        for ci, ((hh, c), alpha) in enumerate(zip(chains, alphas_prev)):
            pv = jnp.dot(vt_ref[0, hh, :, pl.ds(off_p, tk)], pt_prev[ci],
                         preferred_element_type=F32)
            acc_s[ci] = alpha * acc_s[ci] + pv
            pvs.append(pv)
        m_new, alphas = [], []
        for ci, m in enumerate(ms):
            mn = jnp.maximum(m, jnp.max(st_cur[ci], axis=0, keepdims=True).astype(F32))
            alphas.append(jnp.exp2(m - mn))
            m_new.append(mn)
        slices = [(ci, r) for r in range(EXP_SLICES) for ci in range(n_ch)]
        for i, (ci, r) in enumerate(slices):
            kind, cj, e = event_for(i, len(slices))
            src = (res[cj][e * qk_rows:e * qk_rows + 1, :] if kind == "qk"
                   else pvs[cj][e * 16:e * 16 + 1, :])
            mt = (m_new[ci] + tied_zero(src)).astype(BF16)
            rows = slice(r * rs, (r + 1) * rs)
            pt_cur[ci, rows, :] = jnp.exp2(st_cur[ci, rows, :] - mt)
        return tuple(m_new), tuple(alphas)

    def pair(kb, carry):
        ms, alphas = step(kb, st_b, pt_b, st_a, pt_a, *carry)
        return step(kb + 1, st_a, pt_a, st_b, pt_b, ms, alphas)

    def body(jj, carry):
        kb = 2 * ATTN_PAIRS_PER_ITER * jj + 1
        for r in range(ATTN_PAIRS_PER_ITER):
            carry = pair(kb + 2 * r, carry)
        return carry

    nkb = S // tk
    assert nkb % 2 == 0
    n_pairs = nkb // 2 - 1
    acc_s[...] = jnp.zeros_like(acc_s)
    m0 = tuple(jnp.full((1, tq), -jnp.inf, F32) for _ in chains)
    scores(0, st_a)
    scores(1, st_b)
    carry = softmax(st_a, pt_a, m0)
    carry = lax.fori_loop(0, n_pairs // ATTN_PAIRS_PER_ITER, body, carry)
    for r in range(n_pairs - n_pairs % ATTN_PAIRS_PER_ITER, n_pairs):
        carry = pair(2 * r + 1, carry)
    ms, alphas = carry
    accumulate(nkb - 2, pt_a, alphas)
    ms, alphas = softmax(st_b, pt_b, ms)
    accumulate(nkb - 1, pt_b, alphas)
    res = tuple((None, acc_s[ci]) for ci in range(len(chains)))
    outs = []
    for hh in range(heads_per_step):
        a0 = res[2 * hh][1]
        a1 = res[2 * hh + 1][1]
        ot = (a0[:V_DIM, :] / a0[V_DIM:V_DIM + 1, :]
              - lam * (a1[:V_DIM, :] / a1[V_DIM:V_DIM + 1, :]))
        ot = ot * lax.rsqrt(jnp.mean(ot * ot, axis=0, keepdims=True) + RMS_EPS)
        outs.append(ot)
    o = jnp.concatenate(outs, axis=0).T
    o_ref[0] = (o * sw_ref[...] * (1.0 - LAMBDA_INIT)).astype(o_ref.dtype)


def _attn(lam_p, sw2, qt, k, vt, tq, tk):
    B, H, S, _ = k.shape
    hps = 2
    kern = functools.partial(_attn_kernel, tk=tk, heads_per_step=hps)
    return pl.pallas_call(
        kern,
        grid=(B, H // hps, S // tq),
        in_specs=[pl.BlockSpec((4, QK_DIM), lambda b, h, i: (0, 0)),
                  pl.BlockSpec((1, hps * V_DIM), lambda b, h, i: (0, 0)),
                  pl.BlockSpec((1, hps, 2, QK_CONTRACT, tq), lambda b, h, i: (b, h, 0, 0, i)),
                  pl.BlockSpec((1, hps, S, QK_CONTRACT), lambda b, h, i: (b, h, 0, 0)),
                  pl.BlockSpec((1, hps, VT_ROWS, S), lambda b, h, i: (b, h, 0, 0))],
        out_specs=pl.BlockSpec((1, tq, hps * V_DIM), lambda b, h, i: (b, i, h)),
        out_shape=jax.ShapeDtypeStruct((B, S, W_ATTN), BF16),
        scratch_shapes=[pltpu.VMEM((2 * hps, tk, tq), BF16), pltpu.VMEM((2 * hps, tk, tq), BF16),
                        pltpu.VMEM((2 * hps, tk, tq), BF16), pltpu.VMEM((2 * hps, tk, tq), BF16),
                        pltpu.VMEM((2 * hps, VT_ROWS, tq), F32)],
        compiler_params=_cparams(("arbitrary", "arbitrary", "arbitrary")),
        name="attn",
    )(lam_p, sw2, qt, k, vt)


def _s5_kernel(u_ref, d_ref, bm_ref, cm_ref, tab_ref, y_ref, xr_s, xi_s, xr_b, xi_b, *, rows):
    S = u_ref.shape[1]
    R = rows
    seg = R // SUBLANES
    nchunk = S // R
    P = STATE_LANES

    def cmul_add(ar, ai, xr, xi, br, bi):
        return ar * xr - ai * xi + br, ar * xi + ai * xr + bi

    def scan_chunk(di, c, carry, xr_s, xi_s):
        bm = bm_ref[di, 0]
        cm = cm_ref[di, 0]
        ar = tab_ref[di, 0, 0:1, :]
        ai = tab_ref[di, 0, 1:2, :]
        asr = tab_ref[di, 0, 2:3, :]
        asi = tab_ref[di, 0, 3:4, :]
        steps = list(range(seg)) if di == 0 else list(range(seg - 1, -1, -1))
        segs = list(range(SUBLANES)) if di == 0 else list(range(SUBLANES - 1, -1, -1))
        r0 = pl.multiple_of(c * R, R)
        u = jnp.concatenate([u_ref[0, pl.ds(r0 + j, SUBLANES, stride=seg), :]
                             for j in range(seg)], axis=0)
        bu = jnp.dot(u.astype(BF16), bm, preferred_element_type=F32)
        ar8 = jnp.broadcast_to(ar, (SUBLANES, P))
        ai8 = jnp.broadcast_to(ai, (SUBLANES, P))
        zero8 = jnp.zeros((SUBLANES, P), F32)
        xr, xi = zero8, zero8
        for j in steps:
            xr, xi = cmul_add(ar8, ai8, xr, xi, bu[j * SUBLANES:(j + 1) * SUBLANES, :P],
                              bu[j * SUBLANES:(j + 1) * SUBLANES, P:])
        cr, cim = carry
        ent_r, ent_i = [None] * SUBLANES, [None] * SUBLANES
        for s_ in segs:
            ent_r[s_], ent_i[s_] = cr, cim
            cr, cim = cmul_add(asr, asi, cr, cim, xr[s_:s_ + 1, :], xi[s_:s_ + 1, :])
        xr = jnp.concatenate(ent_r, axis=0)
        xi = jnp.concatenate(ent_i, axis=0)
        for j in steps:
            xr, xi = cmul_add(ar8, ai8, xr, xi, bu[j * SUBLANES:(j + 1) * SUBLANES, :P],
                              bu[j * SUBLANES:(j + 1) * SUBLANES, P:])
            xr_s[j * SUBLANES:(j + 1) * SUBLANES, :] = xr
            xi_s[j * SUBLANES:(j + 1) * SUBLANES, :] = xi
        yc = (jnp.dot(xr_s[...].astype(BF16), cm[:P], preferred_element_type=F32)
              + jnp.dot(xi_s[...].astype(BF16), cm[P:], preferred_element_type=F32))
        return yc, r0, (cr, cim)

    def add_rows(r0, yc):
        for j in range(seg):
            rows_j = pl.ds(r0 + j, SUBLANES, stride=seg)
            y_ref[0, rows_j, :] = y_ref[0, rows_j, :] + yc[j * SUBLANES:(j + 1) * SUBLANES, :]

    assert nchunk % 2 == 0
    y_ref[0] = u_ref[0] * d_ref[0]

    def body(ci, carry):
        cf, cb = carry
        yf, r0f, cf = scan_chunk(0, ci, cf, xr_s, xi_s)
        yb, r0b, cb = scan_chunk(1, nchunk - 1 - ci, cb, xr_b, xi_b)
        add_rows(r0f, yf)
        add_rows(r0b, yb)
        return cf, cb

    zero = (jnp.zeros((1, P), F32), jnp.zeros((1, P), F32))
    lax.fori_loop(0, nchunk, body, (zero, zero))


def _s5(u, d3, bmat, cmat, tab, rows):
    B, S, W = u.shape
    nb = W // LANES
    P = STATE_LANES
    kern = functools.partial(_s5_kernel, rows=rows)
    return pl.pallas_call(
        kern,
        grid=(B, nb),
        in_specs=[pl.BlockSpec((1, S, LANES), lambda b, g: (b, 0, g)),
                  pl.BlockSpec((1, 1, LANES), lambda b, g: (g, 0, 0)),
                  pl.BlockSpec((2, 1, LANES, 2 * P), lambda b, g: (0, g, 0, 0)),
                  pl.BlockSpec((2, 1, 2 * P, LANES), lambda b, g: (0, g, 0, 0)),
                  pl.BlockSpec((2, 1, 4, P), lambda b, g: (0, g, 0, 0))],
        out_specs=pl.BlockSpec((1, S, LANES), lambda b, g: (b, 0, g)),
        out_shape=jax.ShapeDtypeStruct((B, S, W), F32),
        scratch_shapes=[pltpu.VMEM((rows, P), F32) for _ in range(4)],
        compiler_params=_cparams(("arbitrary", "arbitrary")),
        name="s5",
    )(u, d3, bmat, cmat, tab)


def _s5_tables(lam_re, lam_im, log_step, b_re, b_im, cm_re, cm_im, seg):
    nb = S5_GROUPS // GROUPS_PER_BLOCK
    eye = jnp.eye(GROUPS_PER_BLOCK, dtype=F32)
    bmats, cmats, tabs = [], [], []
    for di in range(2):
        lr = lam_re[di].astype(F32)
        li = lam_im[di].astype(F32)
        delta = jnp.exp(log_step[di].astype(F32))[:, None]
        mag = jnp.exp(lr * delta)
        a_re = mag * jnp.cos(li * delta)
        a_im = mag * jnp.sin(li * delta)
        den = lr * lr + li * li
        num_re = a_re - 1.0
        f_re = (num_re * lr + a_im * li) / den
        f_im = (a_im * lr - num_re * li) / den
        br = b_re[di].astype(F32)
        bi = b_im[di].astype(F32)
        bbar_re = f_re[..., None] * br - f_im[..., None] * bi
        bbar_im = f_re[..., None] * bi + f_im[..., None] * br

        def blockdiag_in(bb):
            bb = bb.reshape(nb, GROUPS_PER_BLOCK, S5_STATE, S5_CH)
            m = jnp.einsum('bgpc,gh->bgchp', bb, eye)
            return m.reshape(nb, LANES, STATE_LANES)

        def blockdiag_out(cc):
            cc = cc.reshape(nb, GROUPS_PER_BLOCK, S5_CH, S5_STATE)
            m = jnp.einsum('bgcp,gh->bgphc', cc, eye)
            return m.reshape(nb, STATE_LANES, LANES)

        bmats.append(jnp.concatenate([blockdiag_in(bbar_re), blockdiag_in(bbar_im)], axis=2))
        cmats.append(jnp.concatenate([blockdiag_out(cm_re[di].astype(F32)),
                                      -blockdiag_out(cm_im[di].astype(F32))], axis=1))
        sr, si = a_re, a_im
        n = 1
        while n < seg:
            sr, si = sr * sr - si * si, 2.0 * sr * si
            n *= 2
        assert n == seg
        tabs.append(jnp.stack([a_re, a_im, sr, si]).reshape(4, nb, STATE_LANES)
                    .transpose(1, 0, 2))
    return jnp.stack(bmats).astype(BF16), jnp.stack(cmats).astype(BF16), jnp.stack(tabs)


def _post_kernel(a_ref, ys_ref, x_ref, mod_ref, gluw_ref, glub_ref, onw_ref, wout_ref, n2w_ref,
                 rw_ref, rb_ref, x1_ref, h2_ref, eidx_ref, gate_ref, rank_ref, cnt_ref, carry_s):
    first = (pl.program_id(0) == 0) & (pl.program_id(1) == 0)

    @pl.when(first)
    def _():
        carry_s[...] = jnp.zeros_like(carry_s)

    tm = x_ref.shape[1]
    y = ys_ref[0]
    y = 0.5 * y * (1.0 + jnp.tanh(math.sqrt(2.0 / math.pi) * (y + 0.044715 * (y * y * y))))
    g = jnp.dot(y.astype(BF16), gluw_ref[...], preferred_element_type=F32) + glub_ref[...]
    y = y * jax.nn.sigmoid(g)
    s = _rms(y) * onw_ref[...]
    mix = (jnp.dot(a_ref[0], wout_ref[0:W_ATTN, :], preferred_element_type=F32)
           + jnp.dot(s.astype(BF16), wout_ref[W_ATTN:, :], preferred_element_type=F32))
    g1 = mod_ref[0, 2:3, :]
    sh2 = mod_ref[0, 3:4, :]
    sc2 = mod_ref[0, 4:5, :]
    x1 = x_ref[0] + g1 * mix
    x1_ref[0] = x1
    h2 = (_rms(x1) * n2w_ref[...]) * (1.0 + sc2) + sh2
    h2_ref[0] = h2
    logits = jnp.dot(h2, rw_ref[...], preferred_element_type=F32,
                     precision=lax.Precision.HIGHEST) + rb_ref[...]
    lane = lax.broadcasted_iota(jnp.int32, (tm, LANES), 1)
    neg = jnp.float32(-jnp.inf)
    work = jnp.where(lane < N_EXPERTS, logits, neg)
    vals, idxs = [], []
    for _ in range(TOP_K):
        m = jnp.max(work, axis=-1, keepdims=True)
        idx = jnp.min(jnp.where(work == m, lane, LANES), axis=-1, keepdims=True)
        vals.append(m)
        idxs.append(idx)
        work = jnp.where(lane == idx, neg, work)
    es = [jnp.exp(v - vals[0]) for v in vals]
    den = es[0] + es[1] + es[2] + es[3]
    onehot = jnp.zeros((tm, LANES), F32)
    for idx in idxs:
        onehot = onehot + jnp.where(lane == idx, 1.0, 0.0)
    r_i = lax.broadcasted_iota(jnp.int32, (tm, tm), 0)
    c_i = lax.broadcasted_iota(jnp.int32, (tm, tm), 1)
    ltri = jnp.where(c_i < r_i, 1.0, 0.0).astype(BF16)
    before = jnp.dot(ltri, onehot.astype(BF16), preferred_element_type=F32) + carry_s[...]
    lane4 = lax.broadcasted_iota(jnp.int32, (tm, TOP_K), 1)
    e_out = jnp.zeros((tm, TOP_K), jnp.int32)
    g_out = jnp.zeros((tm, TOP_K), F32)
    r_out = jnp.zeros((tm, TOP_K), jnp.int32)
    for j in range(TOP_K):
        rk = jnp.sum(jnp.where(lane == idxs[j], before, 0.0), axis=-1, keepdims=True)
        e_out = jnp.where(lane4 == j, idxs[j], e_out)
        g_out = jnp.where(lane4 == j, es[j] / den, g_out)
        r_out = jnp.where(lane4 == j, rk.astype(jnp.int32), r_out)
    eidx_ref[0] = e_out
    gate_ref[0] = g_out
    rank_ref[0] = r_out
    carry_s[...] = carry_s[...] + jnp.sum(onehot, axis=0, keepdims=True)
    cnt_ref[...] = carry_s[...]


def _post(a, ys, x, mod3, gluw, glub, onw, wout, n2w, rw, rb, tm):
    B, S, D = x.shape
    c2 = lambda b, i: (0, 0)
    tok = lambda b, i: (b, i, 0)
    return pl.pallas_call(
        _post_kernel,
        grid=(B, S // tm),
        in_specs=[pl.BlockSpec((1, tm, W_ATTN), tok),
                  pl.BlockSpec((1, tm, W_S5), tok),
                  pl.BlockSpec((1, tm, D), tok),
                  pl.BlockSpec((1, 6, D), lambda b, i: (b, 0, 0)),
                  pl.BlockSpec((W_S5, W_S5), c2),
                  pl.BlockSpec((1, W_S5), c2),
                  pl.BlockSpec((1, W_S5), c2),
                  pl.BlockSpec((D, D), c2),
                  pl.BlockSpec((1, D), c2),
                  pl.BlockSpec((D, LANES), c2),
                  pl.BlockSpec((1, LANES), c2)],
        out_specs=[pl.BlockSpec((1, tm, D), tok),
                   pl.BlockSpec((1, tm, D), tok),
                   pl.BlockSpec((1, tm, TOP_K), tok),
                   pl.BlockSpec((1, tm, TOP_K), tok),
                   pl.BlockSpec((1, tm, TOP_K), tok),
                   pl.BlockSpec((1, LANES), c2)],
        out_shape=[jax.ShapeDtypeStruct((B, S, D), F32),
                   jax.ShapeDtypeStruct((B, S, D), F32),
                   jax.ShapeDtypeStruct((B, S, TOP_K), jnp.int32),
                   jax.ShapeDtypeStruct((B, S, TOP_K), F32),
                   jax.ShapeDtypeStruct((B, S, TOP_K), jnp.int32),
                   jax.ShapeDtypeStruct((1, LANES), F32)],
        scratch_shapes=[pltpu.VMEM((1, LANES), F32)],
        compiler_params=_cparams(("arbitrary", "arbitrary")),
        name="post",
    )(a, ys, x, mod3, gluw, glub, onw, wout, n2w, rw, rb)


def _dispatch_kernel(pend_ref, padded_ref, dest_ref, h2_ref, xs_hbm, zero_s, sem, zsem, *, tg):
    @pl.when(pl.program_id(0) == 0)
    def _():
        zero_s[...] = jnp.zeros_like(zero_s)

        def zero_copy(e):
            start = pl.multiple_of(pend_ref[e] - MOE_ROWS, MOE_ROWS)
            return pltpu.make_async_copy(zero_s, xs_hbm.at[pl.ds(start, MOE_ROWS)], zsem)

        for e in range(N_EXPERTS):
            @pl.when(padded_ref[e] > 0)
            def _():
                zero_copy(e).start()
        for e in range(N_EXPERTS):
            @pl.when(padded_ref[e] > 0)
            def _():
                zero_copy(e).wait()

    def row_copy(t, d):
        return pltpu.make_async_copy(h2_ref.at[pl.ds(t, 1)], xs_hbm.at[pl.ds(d, 1)], sem)

    def issue(t, _):
        for j in range(TOP_K):
            row_copy(t, dest_ref[0, 0, t * TOP_K + j]).start(priority=j % 2)
        return 0

    lax.fori_loop(0, tg, issue, 0, unroll=8)
    for _ in range(TOP_K):
        pltpu.make_async_copy(h2_ref, xs_hbm.at[pl.ds(0, tg)], sem).wait()


def _dispatch(pad_end, padded, dest3, h2, n_rows, tg):
    T, D = h2.shape
    kern = functools.partial(_dispatch_kernel, tg=tg)
    grid_spec = pltpu.PrefetchScalarGridSpec(
        num_scalar_prefetch=2,
        grid=(T // tg,),
        in_specs=[pl.BlockSpec((1, 1, tg * TOP_K), lambda i, pe, pd: (i, 0, 0),
                               memory_space=pltpu.SMEM),
                  pl.BlockSpec((tg, D), lambda i, pe, pd: (i, 0))],
        out_specs=pl.BlockSpec(memory_space=pl.ANY),
        scratch_shapes=[pltpu.VMEM((MOE_ROWS, D), F32), pltpu.SemaphoreType.DMA(()),
                        pltpu.SemaphoreType.DMA(())],
    )
    return pl.pallas_call(
        kern,
        grid_spec=grid_spec,
        out_shape=jax.ShapeDtypeStruct((n_rows, D), F32),
        compiler_params=_cparams(("arbitrary",)),
        name="dispatch",
    )(pad_end, padded, dest3, h2)


PERM_CHUNK = 2 * LANES


def _expert_kernel(be_ref, na_ref, x_ref, w1_ref, bg_ref, bl_ref, w2_ref, b2_ref, perm_ref, y_ref,
                   wg_s, wl_s, w2_s):
    i = pl.program_id(0)
    active = i < na_ref[0]
    prev = be_ref[jnp.maximum(i - 1, 0)]
    changed = active & ((i == 0) | (be_ref[i] != prev))

    @pl.when(changed)
    def _():
        perm = perm_ref[...]
        for j in range(2 * D_FF // PERM_CHUNK):
            chunk = w1_ref[0, :, j * PERM_CHUNK:(j + 1) * PERM_CHUNK].astype(BF16)
            sep = jnp.dot(chunk, perm, preferred_element_type=F32).astype(BF16)
            wg_s[:, j * LANES:(j + 1) * LANES] = sep[:, :LANES]
            wl_s[:, j * LANES:(j + 1) * LANES] = sep[:, LANES:]
        w2_s[...] = w2_ref[0].astype(BF16)

    @pl.when(active)
    def _():
        x = x_ref[...].astype(BF16)
        zg = jnp.dot(x, wg_s[...], preferred_element_type=F32) + bg_ref[0]
        zl = jnp.dot(x, wl_s[...], preferred_element_type=F32) + bl_ref[0]
        xg = jnp.minimum(zg, SWIGLU_LIMIT)
        xl = jnp.clip(zl, -SWIGLU_LIMIT, SWIGLU_LIMIT)
        act = xg * jax.nn.sigmoid(SWIGLU_ALPHA * xg) * (xl + 1.0)
        y_ref[...] = jnp.dot(act.astype(BF16), w2_s[...], preferred_element_type=F32) + b2_ref[0]


def _experts(block_e, n_active, xs, w1, bg, bl, w2, b2, perm):
    n_rows, D = xs.shape
    nblk = n_rows // MOE_ROWS

    def row_map(i, be, na):
        return (jnp.minimum(i, na[0] - 1), 0)

    def w_map(i, be, na):
        return (be[i], 0, 0)

    grid_spec = pltpu.PrefetchScalarGridSpec(
        num_scalar_prefetch=2,
        grid=(nblk,),
        in_specs=[pl.BlockSpec((MOE_ROWS, D), row_map),
                  pl.BlockSpec((1, D, 2 * D_FF), w_map),
                  pl.BlockSpec((1, 1, D_FF), w_map),
                  pl.BlockSpec((1, 1, D_FF), w_map),
                  pl.BlockSpec((1, D_FF, D), w_map),
                  pl.BlockSpec((1, 1, D), w_map),
                  pl.BlockSpec((PERM_CHUNK, PERM_CHUNK), lambda i, be, na: (0, 0))],
        out_specs=pl.BlockSpec((MOE_ROWS, D), row_map),
        scratch_shapes=[pltpu.VMEM((D, D_FF), BF16), pltpu.VMEM((D, D_FF), BF16),
                        pltpu.VMEM((D_FF, D), BF16)],
    )
    return pl.pallas_call(
        _expert_kernel,
        grid_spec=grid_spec,
        out_shape=jax.ShapeDtypeStruct((n_rows, D), F32),
        compiler_params=_cparams(("arbitrary",)),
        name="experts",
    )(block_e, n_active, xs, w1, bg, bl, w2, b2, perm)


def _combine_kernel(dest_ref, destn_ref, gate_ref, x1_ref, mod_ref, ys_hbm, o_ref, buf, sem, *, tc):
    i = pl.program_id(0)
    slot = i % 2

    def issue_all(idx_ref, sl):
        def issue(t, _):
            for j in range(TOP_K):
                d = idx_ref[0, 0, t * TOP_K + j]
                pltpu.make_async_copy(ys_hbm.at[pl.ds(d, 1)], buf.at[sl, j, pl.ds(t, 1)],
                                      sem.at[sl]).start(priority=j % 2)
            return 0

        lax.fori_loop(0, tc, issue, 0, unroll=8)

    @pl.when(i == 0)
    def _():
        issue_all(dest_ref, 0)

    @pl.when(i + 1 < pl.num_programs(0))
    def _():
        issue_all(destn_ref, 1 - slot)

    for j in range(TOP_K):
        pltpu.make_async_copy(ys_hbm.at[pl.ds(0, tc)], buf.at[slot, j], sem.at[slot]).wait()
    gates = gate_ref[...]
    acc = gates[:, 0:1] * buf[slot, 0]
    for j in range(1, TOP_K):
        acc = acc + gates[:, j:j + 1] * buf[slot, j]
    o_ref[...] = x1_ref[...] + mod_ref[0, 5:6, :] * acc


def _combine(dest3, gates, x1, mod3, ys, tc, tiles_per_batch):
    T, D = x1.shape
    n_tiles = T // tc
    kern = functools.partial(_combine_kernel, tc=tc)
    return pl.pallas_call(
        kern,
        grid=(n_tiles,),
        in_specs=[pl.BlockSpec((1, 1, tc * TOP_K), lambda i: (i, 0, 0), memory_space=pltpu.SMEM),
                  pl.BlockSpec((1, 1, tc * TOP_K), lambda i: (jnp.minimum(i + 1, n_tiles - 1), 0, 0),
                               memory_space=pltpu.SMEM),
                  pl.BlockSpec((tc, TOP_K), lambda i: (i, 0)),
                  pl.BlockSpec((tc, D), lambda i: (i, 0)),
                  pl.BlockSpec((1, 6, D), lambda i: (i // tiles_per_batch, 0, 0)),
                  pl.BlockSpec(memory_space=pl.ANY)],
        out_specs=pl.BlockSpec((tc, D), lambda i: (i, 0)),
        out_shape=jax.ShapeDtypeStruct((T, D), F32),
        scratch_shapes=[pltpu.VMEM((2, TOP_K, tc, D), F32), pltpu.SemaphoreType.DMA((2,))],
        compiler_params=_cparams(("arbitrary",)),
        name="combine",
    )(dest3, dest3, gates, x1, mod3, ys)


def kernel(x, c, positions, ada_w, ada_b, norm1_w, w_in, q_norm_w, k_norm_w, lambda_q1, lambda_k1,
           lambda_q2, lambda_k2, subln_w, s5_lambda_re, s5_lambda_im, s5_log_step, s5_b_re, s5_b_im,
           s5_cmat_re, s5_cmat_im, s5_d, s5_glu_w, s5_glu_b, s5_out_norm_w, w_out, norm2_w,
           router_w, router_b, mlp1_w, mlp1_b, mlp2_w, mlp2_b):
    B, S, D = x.shape
    T = B * S
    l = 0

    c_pad = jnp.pad(c, ((0, SUBLANES - B % SUBLANES if B % SUBLANES else 0), (0, 0)))
    mod = _ada(c_pad, ada_w[l], ada_b[l][None, :])[:B]
    mod3 = mod.reshape(B, 6, D)

    tm = min(512, S)
    inv_freq = ROPE_THETA ** (-jnp.arange(0, ROT_DIM, 2, dtype=F32) / ROT_DIM)
    d_in_head = jnp.arange(LANES) % QK_DIM
    invf = jnp.where(d_in_head < ROT_DIM, inv_freq[d_in_head % (ROT_DIM // 2)], 0.0)[None, :]
    pos3 = positions.astype(F32)[..., None]
    gmat = jnp.kron(jnp.eye(W_QK // QK_DIM, dtype=F32), jnp.ones((QK_DIM, QK_DIM), F32)).astype(BF16)
    q_bound = jnp.max(jnp.abs(q_norm_w[l])) * (math.log2(math.e) / math.sqrt(QK_DIM))
    k_bound = jnp.max(jnp.abs(k_norm_w[l]))
    shift = jnp.exp2(jnp.floor(0.5 * jnp.log2(jnp.maximum(q_bound, 1e-30)
                                              / jnp.maximum(k_bound, 1e-30))))
    qw = jnp.tile(q_norm_w[l], W_QK // QK_DIM)[None, :] / shift
    kw = jnp.tile(k_norm_w[l], W_QK // QK_DIM)[None, :] * shift
    qt, k, vt, u = _inproj(x, mod3, norm1_w[l][None, :], pos3, invf, w_in[l].astype(BF16),
                          qw, kw, gmat, tm)

    lam_p = jnp.stack([lambda_q1[l], lambda_k1[l], lambda_q2[l], lambda_k2[l]]).astype(F32)
    sw2 = jnp.tile(subln_w[l], 2)[None, :]
    a_out = _attn(lam_p, sw2, qt, k, vt, tq=min(512, S), tk=min(256, S // 2))

    s5_rows = min(256, S)
    bmat, cmat, tab = _s5_tables(s5_lambda_re[l], s5_lambda_im[l], s5_log_step[l], s5_b_re[l],
                                 s5_b_im[l], s5_cmat_re[l], s5_cmat_im[l], s5_rows // SUBLANES)
    d3 = s5_d[l].astype(F32).reshape(W_S5 // LANES, 1, LANES)
    y_s5 = _s5(u, d3, bmat, cmat, tab, rows=s5_rows)

    rw = jnp.pad(router_w[l], ((0, 0), (0, LANES - N_EXPERTS)))
    rb = jnp.pad(router_b[l], (0, LANES - N_EXPERTS))[None, :]
    tp = min(256, S)
    x1, h2, eidx, gates, rank, counts = _post(
        a_out, y_s5, x, mod3, s5_glu_w[l].astype(BF16), s5_glu_b[l][None, :],
        s5_out_norm_w[l][None, :], w_out[l].astype(BF16), norm2_w[l][None, :], rw, rb, tp)

    counts = counts[0, :N_EXPERTS].astype(jnp.int32)
    padded = ((counts + MOE_ROWS - 1) // MOE_ROWS) * MOE_ROWS
    pad_end = jnp.cumsum(padded)
    pad_start = pad_end - padded
    eflat = eidx.reshape(T * TOP_K)
    dest = pad_start[eflat] + rank.reshape(T * TOP_K)
    n_rows = T * TOP_K + N_EXPERTS * MOE_ROWS
    nblk = n_rows // MOE_ROWS
    blk_row = jnp.arange(nblk, dtype=jnp.int32)[:, None] * MOE_ROWS
    block_e = jnp.minimum(jnp.sum((blk_row >= pad_end[None, :]).astype(jnp.int32), axis=1),
                          N_EXPERTS - 1)
    n_active = (pad_end[-1] // MOE_ROWS).astype(jnp.int32)[None]

    tg = min(1024, T)
    xs = _dispatch(pad_end.astype(jnp.int32), padded, dest.reshape(T // tg, 1, tg * TOP_K),
                   h2.reshape(T, D), n_rows, tg)

    bg = mlp1_b[l][:, None, 0::2]
    bl = mlp1_b[l][:, None, 1::2]
    src = jnp.arange(PERM_CHUNK)
    perm = (jnp.arange(PERM_CHUNK)[None, :] == ((src % 2) * LANES + src // 2)[:, None]).astype(BF16)
    ys = _experts(block_e, n_active, xs, mlp1_w[l], bg, bl, mlp2_w[l], mlp2_b[l][:, None, :], perm)

    tc = min(256, S)
    out = _combine(dest.reshape(T // tc, 1, tc * TOP_K), gates.reshape(T, TOP_K),
                   x1.reshape(T, D), mod3, ys, tc, S // tc)
    return out.reshape(B, S, D)
```
